```python
import math
import jax, jax.numpy as jnp
from jax import lax
import numpy as np

D_MODEL = 1024
BATCH = 16
SEQ = 2048
DEPTH = 1

MLA_HEADS = 4
Q_LORA_RANK = 256
KV_LORA_RANK = 256
QK_NOPE_DIM = 128
QK_ROPE_DIM = 64
QK_HEAD_DIM = QK_NOPE_DIM + QK_ROPE_DIM
V_HEAD_DIM = 128
MLA_WIDTH = MLA_HEADS * V_HEAD_DIM
ROPE_THETA = 10000.0
Q_BLOCK = 128
GDN_HEADS = 4
GDN_HEAD_DIM = 128
GDN_WIDTH = GDN_HEADS * GDN_HEAD_DIM
CONV_WIDTH = 4
CHUNK = 64
MIX_WIDTH = MLA_WIDTH + GDN_WIDTH
D_FF = 4 * D_MODEL
EPS = 1e-6
IN_SPLITS = (Q_LORA_RANK, KV_LORA_RANK, QK_ROPE_DIM,
             GDN_WIDTH, GDN_WIDTH, GDN_WIDTH, GDN_WIDTH, GDN_HEADS, GDN_HEADS)
D_IN = sum(IN_SPLITS)

kernel_name = "hymba_mla_gdn_sqrelu_layer"


def rms_norm(x, w):
    xf = x.astype(jnp.float32)
    y = xf * lax.rsqrt(jnp.mean(xf * xf, axis=-1, keepdims=True) + EPS)
    return (y * w.astype(jnp.float32)).astype(x.dtype)


def l2_norm(x):
    return x * lax.rsqrt(jnp.sum(x * x, axis=-1, keepdims=True) + EPS)


def split_cols(t, sizes):
    offs = np.cumsum(sizes)[:-1].tolist()
    return jnp.split(t, offs, axis=-1)


def rope_angles(positions):
    half = QK_ROPE_DIM // 2
    inv_freq = ROPE_THETA ** (-jnp.arange(half, dtype=jnp.float32) / half)
    ang = positions.astype(jnp.float32)[..., None] * inv_freq
    return jnp.cos(ang)[:, :, None, :], jnp.sin(ang)[:, :, None, :]


def apply_rope(t, cos, sin):
    tf = t.astype(jnp.float32)
    t1, t2 = jnp.split(tf, 2, axis=-1)
    return jnp.concatenate([t1 * cos - t2 * sin, t2 * cos + t1 * sin], axis=-1).astype(t.dtype)


def causal_attention(q, k, v):
    B, S, H, _ = q.shape
    n_blocks = S // Q_BLOCK
    scale = QK_HEAD_DIM ** -0.5
    qb = jnp.moveaxis(q.reshape(B, n_blocks, Q_BLOCK, H, QK_HEAD_DIM), 1, 0)
    key_pos = jnp.arange(S)

    def one_block(args):
        q_blk, blk = args
        s = jnp.einsum('bqhd,bkhd->bhqk', q_blk, k,
                       preferred_element_type=jnp.float32) * scale
        q_pos = blk * Q_BLOCK + jnp.arange(Q_BLOCK)
        s = jnp.where(key_pos[None, :] <= q_pos[:, None], s, -jnp.inf)
        p = jax.nn.softmax(s, axis=-1).astype(v.dtype)
        return jnp.einsum('bhqk,bkhd->bqhd', p, v)

    o = lax.map(one_block, (qb, jnp.arange(n_blocks)))
    return jnp.moveaxis(o, 0, 1).reshape(B, S, H, V_HEAD_DIM)


def mla_group(q_lat, kv_lat, k_pe, cos, sin, q_lat_norm_w, w_uq, kv_lat_norm_w, w_ukv,
              q_norm_w, k_norm_w, mla_out_norm_w):
    B, S, _ = q_lat.shape
    q = (rms_norm(q_lat, q_lat_norm_w) @ w_uq).reshape(B, S, MLA_HEADS, QK_HEAD_DIM)
    kv = (rms_norm(kv_lat, kv_lat_norm_w) @ w_ukv).reshape(B, S, MLA_HEADS, QK_NOPE_DIM + V_HEAD_DIM)
    k_nope, v = jnp.split(kv, [QK_NOPE_DIM], axis=-1)
    q_nope = rms_norm(q[..., :QK_NOPE_DIM], q_norm_w[:QK_NOPE_DIM])
    q_pe = apply_rope(rms_norm(q[..., QK_NOPE_DIM:], q_norm_w[QK_NOPE_DIM:]), cos, sin)
    k_nope = rms_norm(k_nope, k_norm_w[:QK_NOPE_DIM])
    k_pe = apply_rope(rms_norm(k_pe[:, :, None, :], k_norm_w[QK_NOPE_DIM:]), cos, sin)
    q = jnp.concatenate([q_nope, q_pe], axis=-1)
    k = jnp.concatenate([k_nope, jnp.broadcast_to(k_pe, (B, S, MLA_HEADS, QK_ROPE_DIM))], axis=-1)
    o = causal_attention(q, k, v)
    o = rms_norm(o, mla_out_norm_w)
    return o.reshape(B, S, MLA_WIDTH)


def causal_conv(x, w):
    S = x.shape[1]
    xp = jnp.pad(x, ((0, 0), (CONV_WIDTH - 1, 0), (0, 0)))
    return sum(w[i] * xp[:, i:i + S] for i in range(CONV_WIDTH))


def chunk_gated_delta(q, k, v, g, beta):
    B, H, S, D = q.shape
    N = S // CHUNK
    q, k, v = [t.reshape(B, H, N, CHUNK, D) for t in (q, k, v)]
    g = g.reshape(B, H, N, CHUNK)
    beta = beta.reshape(B, H, N, CHUNK)
    G = jnp.cumsum(g, axis=-1)
    idx = jnp.arange(CHUNK)
    causal = idx[:, None] >= idx[None, :]
    strict = idx[:, None] > idx[None, :]
    decay = jnp.exp(jnp.where(causal, G[..., :, None] - G[..., None, :], -jnp.inf))
    kk = jnp.einsum('bhncd,bhnjd->bhncj', k, k)
    L = jnp.where(strict, beta[..., :, None] * kk * decay, 0.0)
    A = L + jnp.eye(CHUNK, dtype=L.dtype)
    rhs = jnp.concatenate([v * beta[..., None], k * (beta * jnp.exp(G))[..., None]], axis=-1)
    sol = lax.linalg.triangular_solve(A, rhs, left_side=True, lower=True, unit_diagonal=True)
    u, w = jnp.split(sol, 2, axis=-1)
    attn_intra = jnp.einsum('bhncd,bhnjd->bhncj', q, k) * decay
    q_dec = q * jnp.exp(G)[..., None]
    k_dec = k * jnp.exp(G[..., -1:] - G)[..., None]
    chunk_decay = jnp.exp(G[..., -1])

    def step(state, xs):
        u_c, w_c, a_c, qd_c, kd_c, cd_c = xs
        v_new = u_c - jnp.einsum('bhcd,bhde->bhce', w_c, state)
        o_c = jnp.einsum('bhcd,bhde->bhce', qd_c, state) + jnp.einsum('bhcj,bhje->bhce', a_c, v_new)
        state = state * cd_c[..., None, None] + jnp.einsum('bhcd,bhce->bhde', kd_c, v_new)
        return state, o_c

    xs = tuple(jnp.moveaxis(t, 2, 0) for t in (u, w, attn_intra, q_dec, k_dec, chunk_decay))
    state0 = jnp.zeros((B, H, D, D), jnp.float32)
    _, o = lax.scan(step, state0, xs)
    return jnp.moveaxis(o, 0, 2).reshape(B, H, S, D)


def gdn_group(q, k, v, z, a, b, conv_w, a_log, dt_bias, gdn_norm_w):
    B, S, _ = q.shape
    qkv = jax.nn.silu(causal_conv(jnp.concatenate([q, k, v], axis=-1), conv_w))
    q, k, v = [t.reshape(B, S, GDN_HEADS, GDN_HEAD_DIM).transpose(0, 2, 1, 3).astype(jnp.float32)
               for t in jnp.split(qkv, 3, axis=-1)]
    q = l2_norm(q) * (GDN_HEAD_DIM ** -0.5)
    k = l2_norm(k)
    beta = jax.nn.sigmoid(b.astype(jnp.float32)).transpose(0, 2, 1)
    g = (-jnp.exp(a_log.astype(jnp.float32))
         * jax.nn.softplus(a.astype(jnp.float32) + dt_bias.astype(jnp.float32))).transpose(0, 2, 1)
    o = chunk_gated_delta(q, k, v, g, beta).transpose(0, 2, 1, 3).astype(z.dtype)
    zh = z.reshape(B, S, GDN_HEADS, GDN_HEAD_DIM)
    o = rms_norm(o, gdn_norm_w) * jax.nn.silu(zh)
    return o.reshape(B, S, GDN_WIDTH)


def setup_inputs(seed: int = 0) -> dict:
    key = jax.random.key(seed)
    ks = jax.random.split(key, 20)
    L = DEPTH

    def normal(k, shape, fan_in):
        return jax.random.normal(k, shape, jnp.float32) * (fan_in ** -0.5)

    def gain(k, shape):
        return 1.0 + 0.02 * jax.random.normal(k, shape, jnp.float32)

    return {
        "x": jax.random.normal(ks[0], (BATCH, SEQ, D_MODEL), jnp.float32),
        "positions": jnp.broadcast_to(jnp.arange(SEQ, dtype=jnp.int32), (BATCH, SEQ)),
        "attn_norm_w": gain(ks[1], (L, D_MODEL)),
        "w_in": normal(ks[2], (L, D_MODEL, D_IN), D_MODEL),
        "q_lat_norm_w": gain(ks[3], (L, Q_LORA_RANK)),
        "w_uq": normal(ks[4], (L, Q_LORA_RANK, MLA_HEADS * QK_HEAD_DIM), Q_LORA_RANK),
        "kv_lat_norm_w": gain(ks[5], (L, KV_LORA_RANK)),
        "w_ukv": normal(ks[6], (L, KV_LORA_RANK, MLA_HEADS * (QK_NOPE_DIM + V_HEAD_DIM)), KV_LORA_RANK),
        "q_norm_w": gain(ks[7], (L, QK_HEAD_DIM)),
        "k_norm_w": gain(ks[8], (L, QK_HEAD_DIM)),
        "mla_out_norm_w": gain(ks[9], (L, MLA_HEADS, V_HEAD_DIM)),
        "conv_w": normal(ks[10], (L, CONV_WIDTH, 3 * GDN_WIDTH), CONV_WIDTH),
        "a_log": jnp.log(jax.random.uniform(ks[11], (L, GDN_HEADS), jnp.float32, 1.0, 16.0)),
        "dt_bias": 0.1 * jax.random.normal(ks[12], (L, GDN_HEADS), jnp.float32),
        "gdn_norm_w": gain(ks[13], (L, GDN_HEAD_DIM)),
        "w_out": normal(ks[14], (L, MIX_WIDTH, D_MODEL), MIX_WIDTH),
        "mlp_norm_w": gain(ks[15], (L, D_MODEL)),
        "w_up": normal(ks[16], (L, D_MODEL, D_FF), D_MODEL),
        "w_down": normal(ks[17], (L, D_FF, D_MODEL), D_FF),
    }


def reference(x, positions, attn_norm_w, w_in, q_lat_norm_w, w_uq, kv_lat_norm_w, w_ukv,
              q_norm_w, k_norm_w, mla_out_norm_w, conv_w, a_log, dt_bias, gdn_norm_w,
              w_out, mlp_norm_w, w_up, w_down):
    cos, sin = rope_angles(positions)
    h = x
    for l in range(DEPTH):
        xn = rms_norm(h, attn_norm_w[l])
        proj = xn @ w_in[l]
        q_lat, kv_lat, k_pe, gq, gk, gv, gz, ga, gb = split_cols(proj, IN_SPLITS)
        mla_o = mla_group(q_lat, kv_lat, k_pe, cos, sin, q_lat_norm_w[l], w_uq[l],
                          kv_lat_norm_w[l], w_ukv[l], q_norm_w[l], k_norm_w[l], mla_out_norm_w[l])
        gdn_o = gdn_group(gq, gk, gv, gz, ga, gb, conv_w[l], a_log[l], dt_bias[l], gdn_norm_w[l])
        h = h + jnp.concatenate([mla_o, gdn_o], axis=-1) @ w_out[l]
        hn = rms_norm(h, mlp_norm_w[l])
        h = h + jnp.square(jax.nn.relu(hn @ w_up[l])) @ w_down[l]
    return h
```

```python
import functools

import jax
import jax.numpy as jnp
from jax import lax
from jax.experimental import pallas as pl
from jax.experimental.pallas import tpu as pltpu

F32 = jnp.float32
BF16 = jnp.bfloat16

D_MODEL = 1024
N_HEADS = 4
LORA = 256
NOPE = 128
ROPE = 64
HEAD = 128
QK_HEAD = NOPE + ROPE
ROPE_THETA = 10000.0
CONV_W = 4
CHUNK = 64
D_FF = 4 * D_MODEL
EPS = 1e-6
LANES = 128
MIX = N_HEADS * HEAD

PREP_TM = 256
ATTN_TQ = 256
MLP_TM = 512
MLP_FC = 1024
VMEM_LIMIT = 56 * 1024 * 1024


def _mm(a, b):
    return jnp.dot(a.astype(BF16), b.astype(BF16), preferred_element_type=F32)


def _mm_nt(a, b):
    return lax.dot_general(a.astype(BF16), b.astype(BF16), (((1,), (1,)), ((), ())),
                           preferred_element_type=F32)


def _mm_tn(a, b):
    return lax.dot_general(a.astype(BF16), b.astype(BF16), (((0,), (0,)), ((), ())),
                           preferred_element_type=F32)


def _split3(x):
    hi = x.astype(BF16)
    r1 = x - hi.astype(F32)
    mid = r1.astype(BF16)
    lo = (r1 - mid.astype(F32)).astype(BF16)
    return hi, mid, lo


def _rms(x, w, n):
    return x * lax.rsqrt(jnp.sum(x * x, axis=-1, keepdims=True) * (1.0 / n) + EPS) * w


def _sigmoid(x):
    return 1.0 / (1.0 + jnp.exp(-x))


def _rope(x, cosf, sinf, lane):
    rot = jnp.where(lane < ROPE // 2, pltpu.roll(x, LANES - ROPE // 2, 1),
                    pltpu.roll(x, ROPE // 2, 1))
    return x * cosf + rot * sinf


def _prep_kernel(x_ref, pos_ref, anw_ref, wlat_ref, wkpe_ref, wab_ref, wg_ref, wz_ref,
                 qlnw_ref, kvlnw_ref, wuqn_ref, wuqp_ref, wukk_ref, wukv_ref,
                 qnwn_ref, qnwp_ref, knwn_ref, knwp_ref, invf_ref, convw_ref,
                 alog_ref, dtb_ref,
                 q_out, k_out, v_out, gq_out, gk_out, gv_out, z_out, gb_out, rows_out,
                 cbuf, *, tm, tiles_per_seq):
    i = pl.program_id(0)
    x = x_ref[...]
    xb = _rms(x, anw_ref[...], D_MODEL).astype(BF16)

    lane = lax.broadcasted_iota(jnp.int32, (tm, LANES), 1)
    ang = pos_ref[...].astype(F32) * invf_ref[...]
    in_rope = lane < ROPE
    cosf = jnp.where(in_rope, jnp.cos(ang), 0.0)
    sinf = jnp.where(in_rope, jnp.sin(ang), 0.0)
    sinf = jnp.where(lane < ROPE // 2, -sinf, sinf)

    lat = jnp.dot(xb, wlat_ref[...], preferred_element_type=F32)
    qn = _rms(lat[:, :LORA], qlnw_ref[...], LORA).astype(BF16)
    kvn = _rms(lat[:, LORA:], kvlnw_ref[...], LORA).astype(BF16)
    q_nope = jnp.dot(qn, wuqn_ref[...], preferred_element_type=F32)
    q_pe = jnp.dot(qn, wuqp_ref[...], preferred_element_type=F32)
    k_nope = jnp.dot(kvn, wukk_ref[...], preferred_element_type=F32)
    v_out[...] = jnp.dot(kvn, wukv_ref[...], preferred_element_type=F32).astype(BF16)
    k_pe = jnp.dot(xb, wkpe_ref[...], preferred_element_type=F32)
    k_pe = _rope(_rms(k_pe, knwp_ref[...], ROPE), cosf, sinf, lane).astype(BF16)

    scale = QK_HEAD ** -0.5
    qwn = qnwn_ref[...] * scale
    qwp = qnwp_ref[...] * scale
    for h in range(N_HEADS):
        lo = h * HEAD
        base = h * 2 * HEAD
        qn_h = _rms(q_nope[:, lo:lo + HEAD], qwn, NOPE)
        qp_h = _rope(_rms(q_pe[:, lo:lo + HEAD], qwp, ROPE), cosf, sinf, lane)
        q_out[:, base:base + HEAD] = qn_h.astype(BF16)
        q_out[:, base + HEAD:base + 2 * HEAD] = qp_h.astype(BF16)
        k_out[:, base:base + HEAD] = _rms(k_nope[:, lo:lo + HEAD], knwn_ref[...], NOPE).astype(BF16)
        k_out[:, base + HEAD:base + 2 * HEAD] = k_pe

    @pl.when(i % tiles_per_seq == 0)
    def _():
        cbuf[0:8, :] = jnp.zeros((8, 3 * MIX), F32)

    g_all = jnp.dot(xb, wg_ref[...], preferred_element_type=F32)
    cbuf[8:8 + tm, :] = g_all
    conv = convw_ref[CONV_W - 1:CONV_W, :] * g_all
    for j in range(CONV_W - 1):
        conv = conv + convw_ref[j:j + 1, :] * cbuf[pl.ds(8 - (CONV_W - 1) + j, tm), :]
    cbuf[0:8, :] = g_all[tm - 8:tm, :]
    act = conv * _sigmoid(conv)
    for h in range(N_HEADS):
        lo = h * HEAD
        gq = act[:, lo:lo + HEAD]
        gk = act[:, MIX + lo:MIX + lo + HEAD]
        gq = gq * lax.rsqrt(jnp.sum(gq * gq, axis=-1, keepdims=True) + EPS) * (HEAD ** -0.5)
        gk = gk * lax.rsqrt(jnp.sum(gk * gk, axis=-1, keepdims=True) + EPS)
        gq_out[:, lo:lo + HEAD] = gq.astype(BF16)
        gk_out[:, lo:lo + HEAD] = gk.astype(BF16)
    gv_out[...] = act[:, 2 * MIX:].astype(BF16)
    z_out[...] = jnp.dot(xb, wz_ref[...], preferred_element_type=F32).astype(BF16)

    ab = jnp.dot(xb, wab_ref[...], preferred_element_type=F32)
    sp_in = ab + dtb_ref[...]
    softplus = jnp.maximum(sp_in, 0.0) + jnp.log(1.0 + jnp.exp(-jnp.abs(sp_in)))
    g = -jnp.exp(alog_ref[...]) * softplus
    beta = _sigmoid(ab)
    rr = lax.broadcasted_iota(jnp.int32, (tm, tm), 0)
    cc = lax.broadcasted_iota(jnp.int32, (tm, tm), 1)
    tri = jnp.where((cc <= rr) & ((cc // CHUNK) == (rr // CHUNK)), 1.0, 0.0).astype(BF16)
    g_hi, g_mid, g_lo = _split3(g)
    gcum = (jnp.dot(tri, g_hi, preferred_element_type=F32)
            + jnp.dot(tri, g_mid, preferred_element_type=F32)
            + jnp.dot(tri, g_lo, preferred_element_type=F32))
    gb = jnp.where(lane < N_HEADS, gcum, jnp.where(lane < 2 * N_HEADS, beta, 0.0))
    gb_out[...] = gb
    er = lax.broadcasted_iota(jnp.int32, (8, LANES), 0)
    ec = lax.broadcasted_iota(jnp.int32, (8, LANES), 1)
    eye8 = jnp.where(er == ec, 1.0, 0.0).astype(BF16)
    for c in range(tm // CHUNK):
        parts = _split3(gb[c * CHUNK:(c + 1) * CHUNK, :])
        rows = None
        for p in parts:
            t = lax.dot_general(eye8, p, (((1,), (1,)), ((), ())), preferred_element_type=F32)
            rows = t if rows is None else rows + t
        rows_out[c] = rows


def _attn_kernel(q_ref, k_ref, v_ref, w_ref, o_ref, *, tq):
    i = pl.program_id(2)
    q = q_ref[...]

    def block(j, carry, masked):
        m, l, acc = carry
        r0 = pl.multiple_of(j * tq, tq)
        kb = k_ref[pl.ds(r0, tq), :]
        vb = v_ref[pl.ds(r0, tq), :]
        s = lax.dot_general(q, kb, (((1,), (1,)), ((), ())), preferred_element_type=F32)
        if masked:
            rr = lax.broadcasted_iota(jnp.int32, (tq, tq), 0)
            cc = lax.broadcasted_iota(jnp.int32, (tq, tq), 1)
            s = jnp.where(cc <= rr, s, -jnp.inf)
        m_new = jnp.maximum(m, jnp.max(s, axis=-1, keepdims=True))
        alpha = jnp.exp(m - m_new)
        p = jnp.exp(s - m_new)
        l = alpha * l + jnp.sum(p, axis=-1, keepdims=True)
        acc = alpha * acc + jnp.dot(p.astype(BF16), vb, preferred_element_type=F32)
        return m_new, l, acc

    init = (jnp.full((tq, 1), -jnp.inf, F32), jnp.zeros((tq, 1), F32), jnp.zeros((tq, HEAD), F32))
    carry = lax.fori_loop(0, i, lambda j, c: block(j, c, False), init)
    m, l, acc = block(i, carry, True)
    o = acc / l
    o_ref[...] = _rms(o, w_ref[...], HEAD).astype(BF16)


def _gdn_kernel(q_ref, k_ref, v_ref, z_ref, gb_ref, rows_ref, gnw_ref, o_ref, s_ref, *, n_chunks):
    s_ref[...] = jnp.zeros_like(s_ref)
    ii = lax.broadcasted_iota(jnp.int32, (CHUNK, CHUNK), 0)
    jj = lax.broadcasted_iota(jnp.int32, (CHUNK, CHUNK), 1)
    eye = jnp.where(ii == jj, 1.0, 0.0)
    gnw = gnw_ref[...]

    def chunk(n, carry):
        r0 = pl.multiple_of(n * CHUNK, CHUNK)
        scal = gb_ref[pl.ds(r0, CHUNK), :]
        rows = rows_ref[n]
        for h in range(N_HEADS):
            lo = h * HEAD
            q = q_ref[pl.ds(r0, CHUNK), lo:lo + HEAD]
            k = k_ref[pl.ds(r0, CHUNK), lo:lo + HEAD]
            v = v_ref[pl.ds(r0, CHUNK), lo:lo + HEAD]
            g_col = jnp.broadcast_to(scal[:, h:h + 1], (CHUNK, LANES))
            b_col = jnp.broadcast_to(scal[:, N_HEADS + h:N_HEADS + h + 1], (CHUNK, CHUNK))
            g_row = rows[h:h + 1, :]
            b_row = rows[N_HEADS + h:N_HEADS + h + 1, :]
            decay = jnp.exp(jnp.where(ii >= jj, g_col[:, :CHUNK] - g_row, -jnp.inf))
            kk = _mm_nt(k, k)
            qk = _mm_nt(q, k)
            lmat = jnp.where(ii > jj, b_col * kk * decay, 0.0)
            inv = eye - lmat
            pw = _mm(lmat, lmat)
            for t in range(5):
                inv = inv + _mm(inv, pw)
                if t < 4:
                    pw = _mm(pw, pw)
            u = _mm(inv * b_row, v)
            w = _mm(inv * (b_row * jnp.exp(g_row)), k)
            attn = qk * decay
            state = s_ref[h]
            sb = state.astype(BF16)
            v_new = u - _mm(w, sb)
            o = jnp.exp(g_col) * _mm(q, sb) + _mm(attn, v_new)
            g_last = g_col[CHUNK - 1:CHUNK, :]
            kd = k.astype(F32) * jnp.exp(g_last - g_col)
            s_ref[h] = state * jnp.exp(g_last) + _mm_tn(kd, v_new)
            zh = z_ref[pl.ds(r0, CHUNK), lo:lo + HEAD].astype(F32)
            gated = _rms(o, gnw, HEAD) * (zh * _sigmoid(zh))
            o_ref[pl.ds(r0, CHUNK), lo:lo + HEAD] = gated.astype(BF16)
        return carry

    lax.fori_loop(0, n_chunks, chunk, 0)


def _mlp_kernel(x_ref, mla_ref, gdn_ref, woa_ref, wob_ref, nw_ref, wup_ref, wdn_ref, o_ref):
    h = (x_ref[...]
         + jnp.dot(mla_ref[...], woa_ref[...], preferred_element_type=F32)
         + jnp.dot(gdn_ref[...], wob_ref[...], preferred_element_type=F32))
    hn = _rms(h, nw_ref[...], D_MODEL).astype(BF16)
    o_ref[...] = h
    for c in range(D_FF // MLP_FC):
        u = jnp.dot(hn, wup_ref[:, c * MLP_FC:(c + 1) * MLP_FC], preferred_element_type=F32)
        a = jnp.square(jnp.maximum(u, 0.0)).astype(BF16)
        o_ref[...] += jnp.dot(a, wdn_ref[c * MLP_FC:(c + 1) * MLP_FC, :], preferred_element_type=F32)


def _const_spec(shape):
    nd = len(shape)
    return pl.BlockSpec(shape, lambda *_: (0,) * nd, pipeline_mode=pl.Buffered(1))


def _pad_lanes(v, width=LANES):
    v = v.reshape(1, -1).astype(F32)
    return jnp.pad(v, ((0, 0), (0, width - v.shape[1])))


def _layer(h, pos2, inv_freq, p):
    B, S, _ = h.shape
    T = B * S
    x2 = h.reshape(T, D_MODEL)

    w_in = p["w_in"]
    o_q, o_kv, o_pe = 0, LORA, 2 * LORA
    o_g = o_pe + ROPE
    o_z = o_g + 3 * MIX
    o_a = o_z + MIX
    w_lat = w_in[:, o_q:o_pe].astype(BF16)
    w_kpe = jnp.pad(w_in[:, o_pe:o_g], ((0, 0), (0, LANES - ROPE))).astype(BF16)
    w_g = w_in[:, o_g:o_z].astype(BF16)
    w_z = w_in[:, o_z:o_a].astype(BF16)
    w_ab = jnp.pad(w_in[:, o_a:], ((0, 0), (0, LANES - 2 * N_HEADS))).astype(BF16)
    w_uq = p["w_uq"].reshape(LORA, N_HEADS, QK_HEAD)
    w_uq_n = w_uq[:, :, :NOPE].reshape(LORA, N_HEADS * NOPE).astype(BF16)
    w_uq_p = jnp.pad(w_uq[:, :, NOPE:], ((0, 0), (0, 0), (0, LANES - ROPE))).reshape(
        LORA, N_HEADS * LANES).astype(BF16)
    w_ukv = p["w_ukv"].reshape(LORA, N_HEADS, NOPE + HEAD)
    w_uk_k = w_ukv[:, :, :NOPE].reshape(LORA, N_HEADS * NOPE).astype(BF16)
    w_uk_v = w_ukv[:, :, NOPE:].reshape(LORA, N_HEADS * HEAD).astype(BF16)

    tm = PREP_TM
    n_tiles = T // tm
    row_spec = lambda w: pl.BlockSpec((tm, w), lambda i: (i, 0))
    prep_in = [
        (x2, row_spec(D_MODEL)),
        (pos2, row_spec(1)),
        (p["attn_norm_w"].reshape(1, D_MODEL), None),
        (w_lat, None), (w_kpe, None), (w_ab, None), (w_g, None), (w_z, None),
        (p["q_lat_norm_w"].reshape(1, LORA), None),
        (p["kv_lat_norm_w"].reshape(1, LORA), None),
        (w_uq_n, None), (w_uq_p, None), (w_uk_k, None), (w_uk_v, None),
        (p["q_norm_w"][:NOPE].reshape(1, NOPE), None),
        (_pad_lanes(p["q_norm_w"][NOPE:]), None),
        (p["k_norm_w"][:NOPE].reshape(1, NOPE), None),
        (_pad_lanes(p["k_norm_w"][NOPE:]), None),
        (inv_freq, None),
        (p["conv_w"], None),
        (_pad_lanes(p["a_log"]), None),
        (_pad_lanes(p["dt_bias"]), None),
    ]
    prep_args = [a for a, _ in prep_in]
    prep_specs = [s if s is not None else _const_spec(a.shape) for a, s in prep_in]
    out_shapes = [
        jax.ShapeDtypeStruct((T, 2 * MIX), BF16),
        jax.ShapeDtypeStruct((T, 2 * MIX), BF16),
        jax.ShapeDtypeStruct((T, MIX), BF16),
        jax.ShapeDtypeStruct((T, MIX), BF16),
        jax.ShapeDtypeStruct((T, MIX), BF16),
        jax.ShapeDtypeStruct((T, MIX), BF16),
        jax.ShapeDtypeStruct((T, MIX), BF16),
        jax.ShapeDtypeStruct((T, LANES), F32),
        jax.ShapeDtypeStruct((T // CHUNK, 8, CHUNK), F32),
    ]
    out_specs = [row_spec(2 * MIX), row_spec(2 * MIX)] + [row_spec(MIX)] * 5 + [
        row_spec(LANES),
        pl.BlockSpec((tm // CHUNK, 8, CHUNK), lambda i: (i, 0, 0)),
    ]
    q, k, v, gq, gk, gv, gz, gb, rows = pl.pallas_call(
        functools.partial(_prep_kernel, tm=tm, tiles_per_seq=S // tm),
        grid=(n_tiles,),
        in_specs=prep_specs,
        out_specs=out_specs,
        out_shape=out_shapes,
        scratch_shapes=[pltpu.VMEM((tm + 8, 3 * MIX), F32)],
        compiler_params=pltpu.CompilerParams(dimension_semantics=("arbitrary",),
                                             vmem_limit_bytes=VMEM_LIMIT),
        name="prep",
    )(*prep_args)

    tq = ATTN_TQ
    mla_o = pl.pallas_call(
        functools.partial(_attn_kernel, tq=tq),
        grid=(B, N_HEADS, S // tq),
        in_specs=[
            pl.BlockSpec((None, tq, 2 * HEAD), lambda b, hh, i: (b, i, hh)),
            pl.BlockSpec((None, S, 2 * HEAD), lambda b, hh, i: (b, 0, hh)),
            pl.BlockSpec((None, S, HEAD), lambda b, hh, i: (b, 0, hh)),
            pl.BlockSpec((None, 1, HEAD), lambda b, hh, i: (hh, 0, 0)),
        ],
        out_specs=pl.BlockSpec((None, tq, HEAD), lambda b, hh, i: (b, i, hh)),
        out_shape=jax.ShapeDtypeStruct((B, S, MIX), BF16),
        compiler_params=pltpu.CompilerParams(
            dimension_semantics=("arbitrary", "arbitrary", "arbitrary"),
            vmem_limit_bytes=VMEM_LIMIT),
        name="attn",
    )(q.reshape(B, S, 2 * MIX), k.reshape(B, S, 2 * MIX), v.reshape(B, S, MIX),
      p["mla_out_norm_w"].reshape(N_HEADS, 1, HEAD))

    n_chunks = S // CHUNK
    seq_spec = lambda w: pl.BlockSpec((None, S, w), lambda b: (b, 0, 0))
    gdn_o = pl.pallas_call(
        functools.partial(_gdn_kernel, n_chunks=n_chunks),
        grid=(B,),
        in_specs=[seq_spec(MIX)] * 4 + [
            seq_spec(LANES),
            pl.BlockSpec((None, n_chunks, 8, CHUNK), lambda b: (b, 0, 0, 0)),
            _const_spec((1, HEAD)),
        ],
        out_specs=seq_spec(MIX),
        out_shape=jax.ShapeDtypeStruct((B, S, MIX), BF16),
        scratch_shapes=[pltpu.VMEM((N_HEADS, HEAD, HEAD), F32)],
        compiler_params=pltpu.CompilerParams(dimension_semantics=("arbitrary",),
                                             vmem_limit_bytes=VMEM_LIMIT),
        name="gdn",
    )(gq.reshape(B, S, MIX), gk.reshape(B, S, MIX), gv.reshape(B, S, MIX), gz.reshape(B, S, MIX),
      gb.reshape(B, S, LANES), rows.reshape(B, n_chunks, 8, CHUNK),
      p["gdn_norm_w"].reshape(1, HEAD))

    tm2 = MLP_TM
    w_out = p["w_out"].astype(BF16)
    tok_spec = lambda w: pl.BlockSpec((tm2, w), lambda i: (i, 0))
    out = pl.pallas_call(
        _mlp_kernel,
        grid=(T // tm2,),
        in_specs=[
            tok_spec(D_MODEL), tok_spec(MIX), tok_spec(MIX),
            _const_spec((MIX, D_MODEL)), _const_spec((MIX, D_MODEL)),
            _const_spec((1, D_MODEL)),
            _const_spec((D_MODEL, D_FF)), _const_spec((D_FF, D_MODEL)),
        ],
        out_specs=tok_spec(D_MODEL),
        out_shape=jax.ShapeDtypeStruct((T, D_MODEL), F32),
        compiler_params=pltpu.CompilerParams(dimension_semantics=("arbitrary",),
                                             vmem_limit_bytes=VMEM_LIMIT),
        name="mlp",
    )(x2, mla_o.reshape(T, MIX), gdn_o.reshape(T, MIX), w_out[:MIX], w_out[MIX:],
      p["mlp_norm_w"].reshape(1, D_MODEL), p["w_up"].astype(BF16), p["w_down"].astype(BF16))
    return out.reshape(B, S, D_MODEL)


def kernel(x, positions, attn_norm_w, w_in, q_lat_norm_w, w_uq, kv_lat_norm_w, w_ukv, q_norm_w,
           k_norm_w, mla_out_norm_w, conv_w, a_log, dt_bias, gdn_norm_w, w_out, mlp_norm_w, w_up,
           w_down):
    B, S, _ = x.shape
    half = ROPE // 2
    inv_freq = ROPE_THETA ** (-jnp.arange(half, dtype=F32) / half)
    inv_freq = jnp.tile(inv_freq, LANES // half).reshape(1, LANES)
    pos2 = positions.reshape(B * S, 1)
    params = dict(attn_norm_w=attn_norm_w, w_in=w_in, q_lat_norm_w=q_lat_norm_w, w_uq=w_uq,
                  kv_lat_norm_w=kv_lat_norm_w, w_ukv=w_ukv, q_norm_w=q_norm_w, k_norm_w=k_norm_w,
                  mla_out_norm_w=mla_out_norm_w, conv_w=conv_w, a_log=a_log, dt_bias=dt_bias,
                  gdn_norm_w=gdn_norm_w, w_out=w_out, mlp_norm_w=mlp_norm_w, w_up=w_up,
                  w_down=w_down)
    h = x
    for l in range(attn_norm_w.shape[0]):
        h = _layer(h, pos2, inv_freq, {name: val[l] for name, val in params.items()})
    return h
```

```python
import functools

import jax
import jax.numpy as jnp
from jax import lax
from jax.experimental import pallas as pl
from jax.experimental.pallas import tpu as pltpu

F32 = jnp.float32
BF16 = jnp.bfloat16

D_MODEL = 1024
N_HEADS = 4
LORA = 256
NOPE = 128
ROPE = 64
HEAD = 128
QK_HEAD = NOPE + ROPE
ROPE_THETA = 10000.0
CONV_W = 4
CHUNK = 64
D_FF = 4 * D_MODEL
EPS = 1e-6
LANES = 128
MIX = N_HEADS * HEAD

PREP_TM = 256
ATTN_TQ = 256
GDN_UNROLL = 4
MLP_TM = 512
MLP_FC = 1024
VMEM_LIMIT = 56 * 1024 * 1024


def _mm(a, b):
    return jnp.dot(a.astype(BF16), b.astype(BF16), preferred_element_type=F32)


def _mm_nt(a, b):
    return lax.dot_general(a.astype(BF16), b.astype(BF16), (((1,), (1,)), ((), ())),
                           preferred_element_type=F32)


def _mm_tn(a, b):
    return lax.dot_general(a.astype(BF16), b.astype(BF16), (((0,), (0,)), ((), ())),
                           preferred_element_type=F32)


def _split3(x):
    hi = x.astype(BF16)
    r1 = x - hi.astype(F32)
    mid = r1.astype(BF16)
    lo = (r1 - mid.astype(F32)).astype(BF16)
    return hi, mid, lo


def _rms(x, w, n):
    return x * lax.rsqrt(jnp.sum(x * x, axis=-1, keepdims=True) * (1.0 / n) + EPS) * w


def _sigmoid(x):
    return 1.0 / (1.0 + jnp.exp(-x))


def _rope(x, cosf, sinf, lane):
    rot = jnp.where(lane < ROPE // 2, pltpu.roll(x, LANES - ROPE // 2, 1),
                    pltpu.roll(x, ROPE // 2, 1))
    return x * cosf + rot * sinf


def _prep_kernel(x_ref, pos_ref, anw_ref, wlat_ref, wkpe_ref, wab_ref, wg_ref, wz_ref,
                 qlnw_ref, kvlnw_ref, wuqn_ref, wuqp_ref, wukk_ref, wukv_ref,
                 qnwn_ref, qnwp_ref, knwn_ref, knwp_ref, invf_ref, convw_ref,
                 alog_ref, dtb_ref,
                 q_out, k_out, v_out, gq_out, gk_out, gv_out, z_out, gb_out, rows_out,
                 cbuf, *, tm, tiles_per_seq):
    i = pl.program_id(0)
    x = x_ref[...]
    xb = _rms(x, anw_ref[...], D_MODEL).astype(BF16)

    lane = lax.broadcasted_iota(jnp.int32, (tm, LANES), 1)
    ang = pos_ref[...].astype(F32) * invf_ref[...]
    in_rope = lane < ROPE
    cosf = jnp.where(in_rope, jnp.cos(ang), 0.0)
    sinf = jnp.where(in_rope, jnp.sin(ang), 0.0)
    sinf = jnp.where(lane < ROPE // 2, -sinf, sinf)

    lat = jnp.dot(xb, wlat_ref[...], preferred_element_type=F32)
    qn = _rms(lat[:, :LORA], qlnw_ref[...], LORA).astype(BF16)
    kvn = _rms(lat[:, LORA:], kvlnw_ref[...], LORA).astype(BF16)
    q_nope = jnp.dot(qn, wuqn_ref[...], preferred_element_type=F32)
    q_pe = jnp.dot(qn, wuqp_ref[...], preferred_element_type=F32)
    k_nope = jnp.dot(kvn, wukk_ref[...], preferred_element_type=F32)
    v_out[...] = jnp.dot(kvn, wukv_ref[...], preferred_element_type=F32).astype(BF16)
    k_pe = jnp.dot(xb, wkpe_ref[...], preferred_element_type=F32)
    k_pe = _rope(_rms(k_pe, knwp_ref[...], ROPE), cosf, sinf, lane).astype(BF16)

    scale = QK_HEAD ** -0.5
    qwn = qnwn_ref[...] * scale
    qwp = qnwp_ref[...] * scale
    for h in range(N_HEADS):
        lo = h * HEAD
        base = h * 2 * HEAD
        qn_h = _rms(q_nope[:, lo:lo + HEAD], qwn, NOPE)
        qp_h = _rope(_rms(q_pe[:, lo:lo + HEAD], qwp, ROPE), cosf, sinf, lane)
        q_out[:, base:base + HEAD] = qn_h.astype(BF16)
        q_out[:, base + HEAD:base + 2 * HEAD] = qp_h.astype(BF16)
        k_out[:, base:base + HEAD] = _rms(k_nope[:, lo:lo + HEAD], knwn_ref[...], NOPE).astype(BF16)
        k_out[:, base + HEAD:base + 2 * HEAD] = k_pe

    @pl.when(i % tiles_per_seq == 0)
    def _():
        cbuf[0:8, :] = jnp.zeros((8, 3 * MIX), F32)

    g_all = jnp.dot(xb, wg_ref[...], preferred_element_type=F32)
    cbuf[8:8 + tm, :] = g_all
    conv = convw_ref[CONV_W - 1:CONV_W, :] * g_all
    for j in range(CONV_W - 1):
        conv = conv + convw_ref[j:j + 1, :] * cbuf[pl.ds(8 - (CONV_W - 1) + j, tm), :]
    cbuf[0:8, :] = g_all[tm - 8:tm, :]
    act = conv * _sigmoid(conv)
    for h in range(N_HEADS):
        lo = h * HEAD
        gq = act[:, lo:lo + HEAD]
        gk = act[:, MIX + lo:MIX + lo + HEAD]
        gq = gq * lax.rsqrt(jnp.sum(gq * gq, axis=-1, keepdims=True) + EPS) * (HEAD ** -0.5)
        gk = gk * lax.rsqrt(jnp.sum(gk * gk, axis=-1, keepdims=True) + EPS)
        gq_out[:, lo:lo + HEAD] = gq.astype(BF16)
        gk_out[:, lo:lo + HEAD] = gk.astype(BF16)
    gv_out[...] = act[:, 2 * MIX:].astype(BF16)
    z_out[...] = jnp.dot(xb, wz_ref[...], preferred_element_type=F32).astype(BF16)

    ab = jnp.dot(xb, wab_ref[...], preferred_element_type=F32)
    sp_in = ab + dtb_ref[...]
    softplus = jnp.maximum(sp_in, 0.0) + jnp.log(1.0 + jnp.exp(-jnp.abs(sp_in)))
    g = -jnp.exp(alog_ref[...]) * softplus
    beta = _sigmoid(ab)
    rr = lax.broadcasted_iota(jnp.int32, (tm, tm), 0)
    cc = lax.broadcasted_iota(jnp.int32, (tm, tm), 1)
    tri = jnp.where((cc <= rr) & ((cc // CHUNK) == (rr // CHUNK)), 1.0, 0.0).astype(BF16)
    g_hi, g_mid, g_lo = _split3(g)
    gcum = (jnp.dot(tri, g_hi, preferred_element_type=F32)
            + jnp.dot(tri, g_mid, preferred_element_type=F32)
            + jnp.dot(tri, g_lo, preferred_element_type=F32))
    gb = jnp.where(lane < N_HEADS, gcum, jnp.where(lane < 2 * N_HEADS, beta, 0.0))
    gb_out[...] = gb
    er = lax.broadcasted_iota(jnp.int32, (8, LANES), 0)
    ec = lax.broadcasted_iota(jnp.int32, (8, LANES), 1)
    eye8 = jnp.where(er == ec, 1.0, 0.0).astype(BF16)
    for c in range(tm // CHUNK):
        parts = _split3(gb[c * CHUNK:(c + 1) * CHUNK, :])
        rows = None
        for p in parts:
            t = lax.dot_general(eye8, p, (((1,), (1,)), ((), ())), preferred_element_type=F32)
            rows = t if rows is None else rows + t
        rows_out[c] = rows


def _attn_kernel(q_ref, k_ref, v_ref, w_ref, o_ref, m_s, l_s, acc_s, *, tq):
    i = pl.program_id(1)
    heads = range(N_HEADS)
    m_s[...] = jnp.full(m_s.shape, -jnp.inf, F32)
    l_s[...] = jnp.zeros(l_s.shape, F32)
    acc_s[...] = jnp.zeros(acc_s.shape, F32)

    def block(j, masked):
        r0 = pl.multiple_of(j * tq, tq)
        s = [lax.dot_general(q_ref[:, 2 * HEAD * h:2 * HEAD * (h + 1)],
                             k_ref[pl.ds(r0, tq), 2 * HEAD * h:2 * HEAD * (h + 1)],
                             (((1,), (1,)), ((), ())), preferred_element_type=F32) for h in heads]
        if masked:
            rr = lax.broadcasted_iota(jnp.int32, (tq, tq), 0)
            cc = lax.broadcasted_iota(jnp.int32, (tq, tq), 1)
            s = [jnp.where(cc <= rr, x, -jnp.inf) for x in s]
        m_old = [m_s[h] for h in heads]
        m_new = [jnp.maximum(m_old[h], jnp.max(s[h], axis=-1, keepdims=True)) for h in heads]
        alpha = [jnp.exp(m_old[h] - m_new[h]) for h in heads]
        p = [jnp.exp(s[h] - m_new[h]) for h in heads]
        pv = [jnp.dot(p[h].astype(BF16), v_ref[pl.ds(r0, tq), HEAD * h:HEAD * (h + 1)],
                      preferred_element_type=F32) for h in heads]
        for h in heads:
            m_s[h] = m_new[h]
            l_s[h] = alpha[h] * l_s[h] + jnp.sum(p[h], axis=-1, keepdims=True)
            acc_s[h] = alpha[h] * acc_s[h] + pv[h]

    def body(j, carry):
        block(j, False)
        return carry

    lax.fori_loop(0, i, body, 0)
    block(i, True)
    for h in heads:
        o = acc_s[h] / l_s[h]
        o_ref[:, HEAD * h:HEAD * (h + 1)] = _rms(o, w_ref[h], HEAD).astype(BF16)


def _gdn_kernel(q_ref, k_ref, v_ref, z_ref, gb_ref, rows_ref, gnw_ref, o_ref,
                s_ref, u_s, wq_s, kd_s, at_s, *, n_chunks, unroll):
    ii = lax.broadcasted_iota(jnp.int32, (CHUNK, CHUNK), 0)
    jj = lax.broadcasted_iota(jnp.int32, (CHUNK, CHUNK), 1)
    eye = jnp.where(ii == jj, 1.0, 0.0)
    heads = range(N_HEADS)

    def solve(t, carry):
        ns = [t * unroll + c for c in range(unroll)]
        r0 = [pl.multiple_of(n * CHUNK, CHUNK) for n in ns]
        scal = [gb_ref[pl.ds(r, CHUNK), :] for r in r0]
        rows = [rows_ref[n] for n in ns]
        items = [(c, h) for c in range(unroll) for h in heads]
        cols = lambda h: slice(h * HEAD, (h + 1) * HEAD)
        q = [q_ref[pl.ds(r0[c], CHUNK), cols(h)] for c, h in items]
        k = [k_ref[pl.ds(r0[c], CHUNK), cols(h)] for c, h in items]
        v = [v_ref[pl.ds(r0[c], CHUNK), cols(h)] for c, h in items]
        g_col = [jnp.broadcast_to(scal[c][:, h:h + 1], (CHUNK, LANES)) for c, h in items]
        b_col = [jnp.broadcast_to(scal[c][:, N_HEADS + h:N_HEADS + h + 1], (CHUNK, CHUNK))
                 for c, h in items]
        g_row = [rows[c][h:h + 1, :] for c, h in items]
        b_row = [rows[c][N_HEADS + h:N_HEADS + h + 1, :] for c, h in items]
        n_it = range(len(items))
        decay = [jnp.exp(jnp.where(ii >= jj, g_col[x][:, :CHUNK] - g_row[x], -jnp.inf)) for x in n_it]
        qkk = [_mm_nt(jnp.concatenate([q[x], k[x]], axis=0), k[x]) for x in n_it]
        lmat = [jnp.where(ii > jj, b_col[x] * qkk[x][CHUNK:] * decay[x], 0.0) for x in n_it]
        inv = [eye - lmat[x] for x in n_it]
        pw = [_mm(lmat[x], lmat[x]) for x in n_it]
        for step in range(5):
            inv = [inv[x] + _mm(inv[x], pw[x]) for x in n_it]
            if step < 4:
                pw = [_mm(pw[x], pw[x]) for x in n_it]
        u = [_mm(inv[x] * b_row[x], v[x]) for x in n_it]
        w = [_mm(inv[x] * (b_row[x] * jnp.exp(g_row[x])), k[x]) for x in n_it]
        for x, (c, h) in enumerate(items):
            idx = ns[c] * N_HEADS + h
            g_last = g_col[x][CHUNK - 1:CHUNK, :]
            u_s[idx] = u[x]
            wq_s[idx, 0:CHUNK, :] = w[x].astype(BF16)
            wq_s[idx, CHUNK:2 * CHUNK, :] = (q[x].astype(F32) * jnp.exp(g_col[x])).astype(BF16)
            kd_s[idx] = (k[x].astype(F32) * jnp.exp(g_last - g_col[x])).astype(BF16)
            at_s[idx] = (qkk[x][:CHUNK] * decay[x]).astype(BF16)
        return carry

    lax.fori_loop(0, n_chunks // unroll, solve, 0)

    s_ref[...] = jnp.zeros_like(s_ref)
    gnw = gnw_ref[...]

    def scan(n, carry):
        r0 = pl.multiple_of(n * CHUNK, CHUNK)
        g_last = gb_ref[pl.ds(r0 + CHUNK - 1, 1), :]
        state = [s_ref[h] for h in heads]
        sb = [x.astype(BF16) for x in state]
        ws = [_mm(wq_s[n * N_HEADS + h], sb[h]) for h in heads]
        v_new = [(u_s[n * N_HEADS + h] - ws[h][:CHUNK]).astype(BF16) for h in heads]
        o = [ws[h][CHUNK:] + _mm(at_s[n * N_HEADS + h], v_new[h]) for h in heads]
        upd = [_mm_tn(kd_s[n * N_HEADS + h], v_new[h]) for h in heads]
        for h in heads:
            c_dec = jnp.exp(jnp.broadcast_to(g_last[:, h:h + 1], (HEAD, HEAD)))
            s_ref[h] = state[h] * c_dec + upd[h]
            zh = z_ref[pl.ds(r0, CHUNK), h * HEAD:(h + 1) * HEAD].astype(F32)
            gated = _rms(o[h], gnw, HEAD) * (zh * _sigmoid(zh))
            o_ref[pl.ds(r0, CHUNK), h * HEAD:(h + 1) * HEAD] = gated.astype(BF16)
        return carry

    lax.fori_loop(0, n_chunks, scan, 0)


def _mlp_kernel(x_ref, mla_ref, gdn_ref, woa_ref, wob_ref, nw_ref, wup_ref, wdn_ref, o_ref):
    h = (x_ref[...]
         + jnp.dot(mla_ref[...], woa_ref[...], preferred_element_type=F32)
         + jnp.dot(gdn_ref[...], wob_ref[...], preferred_element_type=F32))
    hn = _rms(h, nw_ref[...], D_MODEL).astype(BF16)
    o_ref[...] = h
    for c in range(D_FF // MLP_FC):
        u = jnp.dot(hn, wup_ref[:, c * MLP_FC:(c + 1) * MLP_FC], preferred_element_type=F32)
        a = jnp.square(jnp.maximum(u, 0.0)).astype(BF16)
        o_ref[...] += jnp.dot(a, wdn_ref[c * MLP_FC:(c + 1) * MLP_FC, :], preferred_element_type=F32)


def _const_spec(shape):
    nd = len(shape)
    return pl.BlockSpec(shape, lambda *_: (0,) * nd, pipeline_mode=pl.Buffered(1))


def _pad_lanes(v, width=LANES):
    v = v.reshape(1, -1).astype(F32)
    return jnp.pad(v, ((0, 0), (0, width - v.shape[1])))


def _layer(h, pos2, inv_freq, p):
    B, S, _ = h.shape
    T = B * S
    x2 = h.reshape(T, D_MODEL)

    w_in = p["w_in"]
    o_q, o_kv, o_pe = 0, LORA, 2 * LORA
    o_g = o_pe + ROPE
    o_z = o_g + 3 * MIX
    o_a = o_z + MIX
    w_lat = w_in[:, o_q:o_pe].astype(BF16)
    w_kpe = jnp.pad(w_in[:, o_pe:o_g], ((0, 0), (0, LANES - ROPE))).astype(BF16)
    w_g = w_in[:, o_g:o_z].astype(BF16)
    w_z = w_in[:, o_z:o_a].astype(BF16)
    w_ab = jnp.pad(w_in[:, o_a:], ((0, 0), (0, LANES - 2 * N_HEADS))).astype(BF16)
    w_uq = p["w_uq"].reshape(LORA, N_HEADS, QK_HEAD)
    w_uq_n = w_uq[:, :, :NOPE].reshape(LORA, N_HEADS * NOPE).astype(BF16)
    w_uq_p = jnp.pad(w_uq[:, :, NOPE:], ((0, 0), (0, 0), (0, LANES - ROPE))).reshape(
        LORA, N_HEADS * LANES).astype(BF16)
    w_ukv = p["w_ukv"].reshape(LORA, N_HEADS, NOPE + HEAD)
    w_uk_k = w_ukv[:, :, :NOPE].reshape(LORA, N_HEADS * NOPE).astype(BF16)
    w_uk_v = w_ukv[:, :, NOPE:].reshape(LORA, N_HEADS * HEAD).astype(BF16)

    tm = PREP_TM
    n_tiles = T // tm
    row_spec = lambda w: pl.BlockSpec((tm, w), lambda i: (i, 0))
    prep_in = [
        (x2, row_spec(D_MODEL)),
        (pos2, row_spec(1)),
        (p["attn_norm_w"].reshape(1, D_MODEL), None),
        (w_lat, None), (w_kpe, None), (w_ab, None), (w_g, None), (w_z, None),
        (p["q_lat_norm_w"].reshape(1, LORA), None),
        (p["kv_lat_norm_w"].reshape(1, LORA), None),
        (w_uq_n, None), (w_uq_p, None), (w_uk_k, None), (w_uk_v, None),
        (p["q_norm_w"][:NOPE].reshape(1, NOPE), None),
        (_pad_lanes(p["q_norm_w"][NOPE:]), None),
        (p["k_norm_w"][:NOPE].reshape(1, NOPE), None),
        (_pad_lanes(p["k_norm_w"][NOPE:]), None),
        (inv_freq, None),
        (p["conv_w"], None),
        (_pad_lanes(p["a_log"]), None),
        (_pad_lanes(p["dt_bias"]), None),
    ]
    prep_args = [a for a, _ in prep_in]
    prep_specs = [s if s is not None else _const_spec(a.shape) for a, s in prep_in]
    out_shapes = [
        jax.ShapeDtypeStruct((T, 2 * MIX), BF16),
        jax.ShapeDtypeStruct((T, 2 * MIX), BF16),
        jax.ShapeDtypeStruct((T, MIX), BF16),
        jax.ShapeDtypeStruct((T, MIX), BF16),
        jax.ShapeDtypeStruct((T, MIX), BF16),
        jax.ShapeDtypeStruct((T, MIX), BF16),
        jax.ShapeDtypeStruct((T, MIX), BF16),
        jax.ShapeDtypeStruct((T, LANES), F32),
        jax.ShapeDtypeStruct((T // CHUNK, 8, CHUNK), F32),
    ]
    out_specs = [row_spec(2 * MIX), row_spec(2 * MIX)] + [row_spec(MIX)] * 5 + [
        row_spec(LANES),
        pl.BlockSpec((tm // CHUNK, 8, CHUNK), lambda i: (i, 0, 0)),
    ]
    q, k, v, gq, gk, gv, gz, gb, rows = pl.pallas_call(
        functools.partial(_prep_kernel, tm=tm, tiles_per_seq=S // tm),
        grid=(n_tiles,),
        in_specs=prep_specs,
        out_specs=out_specs,
        out_shape=out_shapes,
        scratch_shapes=[pltpu.VMEM((tm + 8, 3 * MIX), F32)],
        compiler_params=pltpu.CompilerParams(dimension_semantics=("arbitrary",),
                                             vmem_limit_bytes=VMEM_LIMIT),
        name="prep",
    )(*prep_args)

    tq = ATTN_TQ
    mla_o = pl.pallas_call(
        functools.partial(_attn_kernel, tq=tq),
        grid=(B, S // tq),
        in_specs=[
            pl.BlockSpec((None, tq, 2 * MIX), lambda b, i: (b, i, 0)),
            pl.BlockSpec((None, S, 2 * MIX), lambda b, i: (b, 0, 0)),
            pl.BlockSpec((None, S, MIX), lambda b, i: (b, 0, 0)),
            _const_spec((N_HEADS, 1, HEAD)),
        ],
        out_specs=pl.BlockSpec((None, tq, MIX), lambda b, i: (b, i, 0)),
        out_shape=jax.ShapeDtypeStruct((B, S, MIX), BF16),
        scratch_shapes=[pltpu.VMEM((N_HEADS, tq, 1), F32), pltpu.VMEM((N_HEADS, tq, 1), F32),
                        pltpu.VMEM((N_HEADS, tq, HEAD), F32)],
        compiler_params=pltpu.CompilerParams(
            dimension_semantics=("arbitrary", "arbitrary"),
            vmem_limit_bytes=VMEM_LIMIT),
        name="attn",
    )(q.reshape(B, S, 2 * MIX), k.reshape(B, S, 2 * MIX), v.reshape(B, S, MIX),
      p["mla_out_norm_w"].reshape(N_HEADS, 1, HEAD))

    n_chunks = S // CHUNK
    seq_spec = lambda w: pl.BlockSpec((None, S, w), lambda b: (b, 0, 0))
    gdn_o = pl.pallas_call(
        functools.partial(_gdn_kernel, n_chunks=n_chunks, unroll=GDN_UNROLL),
        grid=(B,),
        in_specs=[seq_spec(MIX)] * 4 + [
            seq_spec(LANES),
            pl.BlockSpec((None, n_chunks, 8, CHUNK), lambda b: (b, 0, 0, 0)),
            _const_spec((1, HEAD)),
        ],
        out_specs=seq_spec(MIX),
        out_shape=jax.ShapeDtypeStruct((B, S, MIX), BF16),
        scratch_shapes=[
            pltpu.VMEM((N_HEADS, HEAD, HEAD), F32),
            pltpu.VMEM((n_chunks * N_HEADS, CHUNK, HEAD), F32),
            pltpu.VMEM((n_chunks * N_HEADS, 2 * CHUNK, HEAD), BF16),
            pltpu.VMEM((n_chunks * N_HEADS, CHUNK, HEAD), BF16),
            pltpu.VMEM((n_chunks * N_HEADS, CHUNK, CHUNK), BF16),
        ],
        compiler_params=pltpu.CompilerParams(dimension_semantics=("arbitrary",),
                                             vmem_limit_bytes=VMEM_LIMIT),
        name="gdn",
    )(gq.reshape(B, S, MIX), gk.reshape(B, S, MIX), gv.reshape(B, S, MIX), gz.reshape(B, S, MIX),
      gb.reshape(B, S, LANES), rows.reshape(B, n_chunks, 8, CHUNK),
      p["gdn_norm_w"].reshape(1, HEAD))

    tm2 = MLP_TM
    w_out = p["w_out"].astype(BF16)
    tok_spec = lambda w: pl.BlockSpec((tm2, w), lambda i: (i, 0))
    out = pl.pallas_call(
        _mlp_kernel,
        grid=(T // tm2,),
        in_specs=[
            tok_spec(D_MODEL), tok_spec(MIX), tok_spec(MIX),
            _const_spec((MIX, D_MODEL)), _const_spec((MIX, D_MODEL)),
            _const_spec((1, D_MODEL)),
            _const_spec((D_MODEL, D_FF)), _const_spec((D_FF, D_MODEL)),
        ],
        out_specs=tok_spec(D_MODEL),
        out_shape=jax.ShapeDtypeStruct((T, D_MODEL), F32),
        compiler_params=pltpu.CompilerParams(dimension_semantics=("arbitrary",),
                                             vmem_limit_bytes=VMEM_LIMIT),
        name="mlp",
    )(x2, mla_o.reshape(T, MIX), gdn_o.reshape(T, MIX), w_out[:MIX], w_out[MIX:],
      p["mlp_norm_w"].reshape(1, D_MODEL), p["w_up"].astype(BF16), p["w_down"].astype(BF16))
    return out.reshape(B, S, D_MODEL)


def kernel(x, positions, attn_norm_w, w_in, q_lat_norm_w, w_uq, kv_lat_norm_w, w_ukv, q_norm_w,
           k_norm_w, mla_out_norm_w, conv_w, a_log, dt_bias, gdn_norm_w, w_out, mlp_norm_w, w_up,
           w_down):
    B, S, _ = x.shape
    half = ROPE // 2
    inv_freq = ROPE_THETA ** (-jnp.arange(half, dtype=F32) / half)
    inv_freq = jnp.tile(inv_freq, LANES // half).reshape(1, LANES)
    pos2 = positions.reshape(B * S, 1)
    params = dict(attn_norm_w=attn_norm_w, w_in=w_in, q_lat_norm_w=q_lat_norm_w, w_uq=w_uq,
                  kv_lat_norm_w=kv_lat_norm_w, w_ukv=w_ukv, q_norm_w=q_norm_w, k_norm_w=k_norm_w,
                  mla_out_norm_w=mla_out_norm_w, conv_w=conv_w, a_log=a_log, dt_bias=dt_bias,
                  gdn_norm_w=gdn_norm_w, w_out=w_out, mlp_norm_w=mlp_norm_w, w_up=w_up,
                  w_down=w_down)
    h = x
    for l in range(attn_norm_w.shape[0]):
        h = _layer(h, pos2, inv_freq, {name: val[l] for name, val in params.items()})
    return h
```

```python
import functools

import jax
import jax.numpy as jnp
from jax import lax
from jax.experimental import pallas as pl
from jax.experimental.pallas import tpu as pltpu

F32 = jnp.float32
BF16 = jnp.bfloat16

D_MODEL = 1024
N_HEADS = 4
LORA = 256
NOPE = 128
ROPE = 64
HEAD = 128
QK_HEAD = NOPE + ROPE
ROPE_THETA = 10000.0
CONV_W = 4
CHUNK = 64
D_FF = 4 * D_MODEL
EPS = 1e-6
LANES = 128
MIX = N_HEADS * HEAD

PREP_TM = 256
GDN_UNROLL = 4
MLP_TM = 512
MLP_FC = 1024
VMEM_LIMIT = 56 * 1024 * 1024


def _mm(a, b):
    return jnp.dot(a.astype(BF16), b.astype(BF16), preferred_element_type=F32)


def _mm_nt(a, b):
    return lax.dot_general(a.astype(BF16), b.astype(BF16), (((1,), (1,)), ((), ())),
                           preferred_element_type=F32)


def _mm_tn(a, b):
    return lax.dot_general(a.astype(BF16), b.astype(BF16), (((0,), (0,)), ((), ())),
                           preferred_element_type=F32)


def _split3(x):
    hi = x.astype(BF16)
    r1 = x - hi.astype(F32)
    mid = r1.astype(BF16)
    lo = (r1 - mid.astype(F32)).astype(BF16)
    return hi, mid, lo


def _rms(x, w, n):
    return x * lax.rsqrt(jnp.sum(x * x, axis=-1, keepdims=True) * (1.0 / n) + EPS) * w


def _sigmoid(x):
    return 1.0 / (1.0 + jnp.exp(-x))


def _rope(x, cosf, sinf, lane):
    rot = jnp.where(lane < ROPE // 2, pltpu.roll(x, LANES - ROPE // 2, 1),
                    pltpu.roll(x, ROPE // 2, 1))
    return x * cosf + rot * sinf


def _prep_kernel(x_ref, pos_ref, posr_ref, anw_ref, wlat_ref, wkpe_ref, wab_ref, wg_ref, wz_ref,
                 qlnw_ref, kvlnw_ref, wuqnt_ref, wuqpt_ref, wukk_ref, wukvt_ref,
                 qnwn_ref, qnwp_ref, knwn_ref, knwp_ref, invf_ref, invft_ref, convw_ref,
                 alog_ref, dtb_ref,
                 qt_out, k_out, vt_out, gq_out, gk_out, gv_out, z_out, gb_out, rows_out,
                 cbuf, *, tm, tiles_per_seq):
    i = pl.program_id(0)
    x = x_ref[...]
    xb = _rms(x, anw_ref[...], D_MODEL).astype(BF16)

    lane = lax.broadcasted_iota(jnp.int32, (tm, LANES), 1)
    ang = pos_ref[...].astype(F32) * invf_ref[...]
    in_rope = lane < ROPE
    cosf = jnp.where(in_rope, jnp.cos(ang), 0.0)
    sinf = jnp.where(in_rope, jnp.sin(ang), 0.0)
    sinf = jnp.where(lane < ROPE // 2, -sinf, sinf)
    ang_t = invft_ref[...] * posr_ref[...].astype(F32)
    cos_t = jnp.cos(ang_t)
    sin_t = jnp.sin(ang_t)

    lat = jnp.dot(xb, wlat_ref[...], preferred_element_type=F32)
    qn = _rms(lat[:, :LORA], qlnw_ref[...], LORA)
    kvn = _rms(lat[:, LORA:], kvlnw_ref[...], LORA)
    qn_t = qn.T.astype(BF16)
    kvn_t = kvn.T.astype(BF16)
    kvn = kvn.astype(BF16)
    qt_nope = jnp.dot(wuqnt_ref[...], qn_t, preferred_element_type=F32)
    qt_pe = jnp.dot(wuqpt_ref[...], qn_t, preferred_element_type=F32)
    vt_out[...] = jnp.dot(wukvt_ref[...], kvn_t, preferred_element_type=F32).astype(BF16)
    k_nope = jnp.dot(kvn, wukk_ref[...], preferred_element_type=F32)
    k_pe = jnp.dot(xb, wkpe_ref[...], preferred_element_type=F32)
    k_pe = _rope(_rms(k_pe, knwp_ref[...], ROPE), cosf, sinf, lane).astype(BF16)

    scale = QK_HEAD ** -0.5
    qwn = qnwn_ref[...] * scale
    qwp = qnwp_ref[...] * scale
    half = ROPE // 2
    for h in range(N_HEADS):
        lo = h * HEAD
        base = h * 2 * HEAD
        xn = qt_nope[lo:lo + NOPE, :]
        xn = xn * lax.rsqrt(jnp.sum(xn * xn, axis=0, keepdims=True) * (1.0 / NOPE) + EPS) * qwn
        xp = qt_pe[h * ROPE:(h + 1) * ROPE, :]
        xp = xp * lax.rsqrt(jnp.sum(xp * xp, axis=0, keepdims=True) * (1.0 / ROPE) + EPS) * qwp
        t1, t2 = xp[:half], xp[half:]
        qt_out[base:base + NOPE, :] = xn.astype(BF16)
        qt_out[base + NOPE:base + NOPE + half, :] = (t1 * cos_t - t2 * sin_t).astype(BF16)
        qt_out[base + NOPE + half:base + QK_HEAD, :] = (t2 * cos_t + t1 * sin_t).astype(BF16)
        qt_out[base + QK_HEAD:base + 2 * HEAD, :] = jnp.zeros((2 * HEAD - QK_HEAD, tm), BF16)
        k_out[:, base:base + HEAD] = _rms(k_nope[:, lo:lo + HEAD], knwn_ref[...], NOPE).astype(BF16)
        k_out[:, base + HEAD:base + 2 * HEAD] = k_pe

    @pl.when(i % tiles_per_seq == 0)
    def _():
        cbuf[0:8, :] = jnp.zeros((8, 3 * MIX), F32)

    g_all = jnp.dot(xb, wg_ref[...], preferred_element_type=F32)
    cbuf[8:8 + tm, :] = g_all
    conv = convw_ref[CONV_W - 1:CONV_W, :] * g_all
    for j in range(CONV_W - 1):
        conv = conv + convw_ref[j:j + 1, :] * cbuf[pl.ds(8 - (CONV_W - 1) + j, tm), :]
    cbuf[0:8, :] = g_all[tm - 8:tm, :]
    act = conv * _sigmoid(conv)
    for h in range(N_HEADS):
        lo = h * HEAD
        gq = act[:, lo:lo + HEAD]
        gk = act[:, MIX + lo:MIX + lo + HEAD]
        gq = gq * lax.rsqrt(jnp.sum(gq * gq, axis=-1, keepdims=True) + EPS) * (HEAD ** -0.5)
        gk = gk * lax.rsqrt(jnp.sum(gk * gk, axis=-1, keepdims=True) + EPS)
        gq_out[:, lo:lo + HEAD] = gq.astype(BF16)
        gk_out[:, lo:lo + HEAD] = gk.astype(BF16)
    gv_out[...] = act[:, 2 * MIX:].astype(BF16)
    z_out[...] = jnp.dot(xb, wz_ref[...], preferred_element_type=F32).astype(BF16)

    ab = jnp.dot(xb, wab_ref[...], preferred_element_type=F32)
    sp_in = ab + dtb_ref[...]
    softplus = jnp.maximum(sp_in, 0.0) + jnp.log(1.0 + jnp.exp(-jnp.abs(sp_in)))
    g = -jnp.exp(alog_ref[...]) * softplus
    beta = _sigmoid(ab)
    rr = lax.broadcasted_iota(jnp.int32, (tm, tm), 0)
    cc = lax.broadcasted_iota(jnp.int32, (tm, tm), 1)
    tri = jnp.where((cc <= rr) & ((cc // CHUNK) == (rr // CHUNK)), 1.0, 0.0).astype(BF16)
    g_hi, g_mid, g_lo = _split3(g)
    gcum = (jnp.dot(tri, g_hi, preferred_element_type=F32)
            + jnp.dot(tri, g_mid, preferred_element_type=F32)
            + jnp.dot(tri, g_lo, preferred_element_type=F32))
    gb = jnp.where(lane < N_HEADS, gcum, jnp.where(lane < 2 * N_HEADS, beta, 0.0))
    gb_out[...] = gb
    er = lax.broadcasted_iota(jnp.int32, (8, LANES), 0)
    ec = lax.broadcasted_iota(jnp.int32, (8, LANES), 1)
    eye8 = jnp.where(er == ec, 1.0, 0.0).astype(BF16)
    for c in range(tm // CHUNK):
        parts = _split3(gb[c * CHUNK:(c + 1) * CHUNK, :])
        rows = None
        for p in parts:
            t = lax.dot_general(eye8, p, (((1,), (1,)), ((), ())), preferred_element_type=F32)
            rows = t if rows is None else rows + t
        rows_out[c] = rows


def _attn_kernel(qt_ref, k_ref, vt_ref, w_ref, o_ref, m_s, l_s, acc_s, *, tq):
    i = pl.program_id(1)
    heads = range(N_HEADS)
    ones_rows = 16
    m_s[...] = jnp.full(m_s.shape, -jnp.inf, F32)
    l_s[...] = jnp.zeros(l_s.shape, F32)
    acc_s[...] = jnp.zeros(acc_s.shape, F32)

    def block(j, masked):
        r0 = pl.multiple_of(j * tq, tq)
        st = [jnp.dot(k_ref[pl.ds(r0, tq), 2 * HEAD * h:2 * HEAD * (h + 1)],
                      qt_ref[2 * HEAD * h:2 * HEAD * (h + 1), :],
                      preferred_element_type=F32) for h in heads]
        if masked:
            kk = lax.broadcasted_iota(jnp.int32, (tq, tq), 0)
            qq = lax.broadcasted_iota(jnp.int32, (tq, tq), 1)
            st = [jnp.where(kk <= qq, x, -jnp.inf) for x in st]
        m_old = [m_s[h] for h in heads]
        m_new = [jnp.maximum(m_old[h], jnp.max(st[h], axis=0, keepdims=True)) for h in heads]
        alpha = [jnp.exp(m_old[h] - m_new[h]) for h in heads]
        p = [jnp.exp(st[h] - m_new[h]).astype(BF16) for h in heads]
        ones = jnp.ones((ones_rows, tq), BF16)
        pv = [jnp.dot(jnp.concatenate([vt_ref[j, HEAD * h:HEAD * (h + 1), :], ones], axis=0), p[h],
                      preferred_element_type=F32) for h in heads]
        for h in heads:
            m_s[h] = m_new[h]
            l_s[h] = alpha[h] * l_s[h] + pv[h][HEAD:HEAD + 1]
            acc_s[h] = alpha[h] * acc_s[h] + pv[h][:HEAD]

    def body(j, carry):
        block(j, False)
        return carry

    lax.fori_loop(0, i, body, 0)
    block(i, True)
    for h in heads:
        o = (acc_s[h] / l_s[h]).T
        o_ref[:, HEAD * h:HEAD * (h + 1)] = _rms(o, w_ref[h], HEAD).astype(BF16)


def _gdn_kernel(q_ref, k_ref, v_ref, z_ref, gb_ref, rows_ref, gnw_ref, o_ref,
                s_ref, u_s, wq_s, kd_s, at_s, *, n_chunks, unroll):
    ii = lax.broadcasted_iota(jnp.int32, (CHUNK, CHUNK), 0)
    jj = lax.broadcasted_iota(jnp.int32, (CHUNK, CHUNK), 1)
    eye = jnp.where(ii == jj, 1.0, 0.0)
    heads = range(N_HEADS)

    def solve(t, carry):
        ns = [t * unroll + c for c in range(unroll)]
        r0 = [pl.multiple_of(n * CHUNK, CHUNK) for n in ns]
        scal = [gb_ref[pl.ds(r, CHUNK), :] for r in r0]
        rows = [rows_ref[n] for n in ns]
        items = [(c, h) for c in range(unroll) for h in heads]
        cols = lambda h: slice(h * HEAD, (h + 1) * HEAD)
        q = [q_ref[pl.ds(r0[c], CHUNK), cols(h)] for c, h in items]
        k = [k_ref[pl.ds(r0[c], CHUNK), cols(h)] for c, h in items]
        v = [v_ref[pl.ds(r0[c], CHUNK), cols(h)] for c, h in items]
        g_col = [jnp.broadcast_to(scal[c][:, h:h + 1], (CHUNK, LANES)) for c, h in items]
        b_col = [jnp.broadcast_to(scal[c][:, N_HEADS + h:N_HEADS + h + 1], (CHUNK, CHUNK))
                 for c, h in items]
        g_row = [rows[c][h:h + 1, :] for c, h in items]
        b_row = [rows[c][N_HEADS + h:N_HEADS + h + 1, :] for c, h in items]
        n_it = range(len(items))
        decay = [jnp.exp(jnp.where(ii >= jj, g_col[x][:, :CHUNK] - g_row[x], -jnp.inf)) for x in n_it]
        qkk = [_mm_nt(jnp.concatenate([q[x], k[x]], axis=0), k[x]) for x in n_it]
        lmat = [jnp.where(ii > jj, b_col[x] * qkk[x][CHUNK:] * decay[x], 0.0) for x in n_it]
        inv = [eye - lmat[x] for x in n_it]
        pw = [_mm(lmat[x], lmat[x]) for x in n_it]
        for step in range(5):
            inv = [inv[x] + _mm(inv[x], pw[x]) for x in n_it]
            if step < 4:
                pw = [_mm(pw[x], pw[x]) for x in n_it]
        u = [_mm(inv[x] * b_row[x], v[x]) for x in n_it]
        w = [_mm(inv[x] * (b_row[x] * jnp.exp(g_row[x])), k[x]) for x in n_it]
        for x, (c, h) in enumerate(items):
            idx = ns[c] * N_HEADS + h
            g_last = g_col[x][CHUNK - 1:CHUNK, :]
            u_s[idx] = u[x]
            wq_s[idx, 0:CHUNK, :] = w[x].astype(BF16)
            wq_s[idx, CHUNK:2 * CHUNK, :] = (q[x].astype(F32) * jnp.exp(g_col[x])).astype(BF16)
            kd_s[idx] = (k[x].astype(F32) * jnp.exp(g_last - g_col[x])).astype(BF16)
            at_s[idx] = (qkk[x][:CHUNK] * decay[x]).astype(BF16)
        return carry

    lax.fori_loop(0, n_chunks // unroll, solve, 0)

    s_ref[...] = jnp.zeros_like(s_ref)
    gnw = gnw_ref[...]

    def scan(n, carry):
        r0 = pl.multiple_of(n * CHUNK, CHUNK)
        g_last = gb_ref[pl.ds(r0 + CHUNK - 1, 1), :]
        state = [s_ref[h] for h in heads]
        sb = [x.astype(BF16) for x in state]
        ws = [_mm(wq_s[n * N_HEADS + h], sb[h]) for h in heads]
        v_new = [(u_s[n * N_HEADS + h] - ws[h][:CHUNK]).astype(BF16) for h in heads]
        o = [ws[h][CHUNK:] + _mm(at_s[n * N_HEADS + h], v_new[h]) for h in heads]
        upd = [_mm_tn(kd_s[n * N_HEADS + h], v_new[h]) for h in heads]
        for h in heads:
            c_dec = jnp.exp(jnp.broadcast_to(g_last[:, h:h + 1], (HEAD, HEAD)))
            s_ref[h] = state[h] * c_dec + upd[h]
            zh = z_ref[pl.ds(r0, CHUNK), h * HEAD:(h + 1) * HEAD].astype(F32)
            gated = _rms(o[h], gnw, HEAD) * (zh * _sigmoid(zh))
            o_ref[pl.ds(r0, CHUNK), h * HEAD:(h + 1) * HEAD] = gated.astype(BF16)
        return carry

    lax.fori_loop(0, n_chunks, scan, 0)


def _mlp_kernel(x_ref, mla_ref, gdn_ref, woa_ref, wob_ref, nw_ref, wup_ref, wdn_ref, o_ref):
    h = (x_ref[...]
         + jnp.dot(mla_ref[...], woa_ref[...], preferred_element_type=F32)
         + jnp.dot(gdn_ref[...], wob_ref[...], preferred_element_type=F32))
    hn = _rms(h, nw_ref[...], D_MODEL).astype(BF16)
    o_ref[...] = h
    for c in range(D_FF // MLP_FC):
        u = jnp.dot(hn, wup_ref[:, c * MLP_FC:(c + 1) * MLP_FC], preferred_element_type=F32)
        a = jnp.square(jnp.maximum(u, 0.0)).astype(BF16)
        o_ref[...] += jnp.dot(a, wdn_ref[c * MLP_FC:(c + 1) * MLP_FC, :], preferred_element_type=F32)


def _const_spec(shape):
    nd = len(shape)
    return pl.BlockSpec(shape, lambda *_: (0,) * nd, pipeline_mode=pl.Buffered(1))


def _pad_lanes(v, width=LANES):
    v = v.reshape(1, -1).astype(F32)
    return jnp.pad(v, ((0, 0), (0, width - v.shape[1])))


def _layer(h, pos2, inv_freq, p):
    B, S, _ = h.shape
    T = B * S
    x2 = h.reshape(T, D_MODEL)

    w_in = p["w_in"]
    o_q, o_kv, o_pe = 0, LORA, 2 * LORA
    o_g = o_pe + ROPE
    o_z = o_g + 3 * MIX
    o_a = o_z + MIX
    w_lat = w_in[:, o_q:o_pe].astype(BF16)
    w_kpe = jnp.pad(w_in[:, o_pe:o_g], ((0, 0), (0, LANES - ROPE))).astype(BF16)
    w_g = w_in[:, o_g:o_z].astype(BF16)
    w_z = w_in[:, o_z:o_a].astype(BF16)
    w_ab = jnp.pad(w_in[:, o_a:], ((0, 0), (0, LANES - 2 * N_HEADS))).astype(BF16)
    w_uq = p["w_uq"].reshape(LORA, N_HEADS, QK_HEAD)
    w_uq_nt = w_uq[:, :, :NOPE].reshape(LORA, N_HEADS * NOPE).T.astype(BF16)
    w_uq_pt = w_uq[:, :, NOPE:].reshape(LORA, N_HEADS * ROPE).T.astype(BF16)
    w_ukv = p["w_ukv"].reshape(LORA, N_HEADS, NOPE + HEAD)
    w_uk_k = w_ukv[:, :, :NOPE].reshape(LORA, N_HEADS * NOPE).astype(BF16)
    w_uk_vt = w_ukv[:, :, NOPE:].reshape(LORA, N_HEADS * HEAD).T.astype(BF16)

    tm = PREP_TM
    n_tiles = T // tm
    row_spec = lambda w: pl.BlockSpec((tm, w), lambda i: (i, 0))
    col_bcast = lambda v: jnp.broadcast_to(v.astype(F32)[:, None], (v.shape[0], tm))
    prep_in = [
        (x2, row_spec(D_MODEL)),
        (pos2, row_spec(1)),
        (pos2.reshape(n_tiles, 1, tm), pl.BlockSpec((None, 1, tm), lambda i: (i, 0, 0))),
        (p["attn_norm_w"].reshape(1, D_MODEL), None),
        (w_lat, None), (w_kpe, None), (w_ab, None), (w_g, None), (w_z, None),
        (p["q_lat_norm_w"].reshape(1, LORA), None),
        (p["kv_lat_norm_w"].reshape(1, LORA), None),
        (w_uq_nt, None), (w_uq_pt, None), (w_uk_k, None), (w_uk_vt, None),
        (col_bcast(p["q_norm_w"][:NOPE]), None),
        (col_bcast(p["q_norm_w"][NOPE:]), None),
        (p["k_norm_w"][:NOPE].reshape(1, NOPE), None),
        (_pad_lanes(p["k_norm_w"][NOPE:]), None),
        (inv_freq, None),
        (col_bcast(inv_freq[0, :ROPE // 2]), None),
        (p["conv_w"], None),
        (_pad_lanes(p["a_log"]), None),
        (_pad_lanes(p["dt_bias"]), None),
    ]
    prep_args = [a for a, _ in prep_in]
    prep_specs = [s if s is not None else _const_spec(a.shape) for a, s in prep_in]
    out_shapes = [
        jax.ShapeDtypeStruct((n_tiles, 2 * MIX, tm), BF16),
        jax.ShapeDtypeStruct((T, 2 * MIX), BF16),
        jax.ShapeDtypeStruct((n_tiles, MIX, tm), BF16),
        jax.ShapeDtypeStruct((T, MIX), BF16),
        jax.ShapeDtypeStruct((T, MIX), BF16),
        jax.ShapeDtypeStruct((T, MIX), BF16),
        jax.ShapeDtypeStruct((T, MIX), BF16),
        jax.ShapeDtypeStruct((T, LANES), F32),
        jax.ShapeDtypeStruct((T // CHUNK, 8, CHUNK), F32),
    ]
    tile_spec = lambda r: pl.BlockSpec((None, r, tm), lambda i: (i, 0, 0))
    out_specs = [tile_spec(2 * MIX), row_spec(2 * MIX), tile_spec(MIX)] + [row_spec(MIX)] * 4 + [
        row_spec(LANES),
        pl.BlockSpec((tm // CHUNK, 8, CHUNK), lambda i: (i, 0, 0)),
    ]
    qt, k, vt, gq, gk, gv, gz, gb, rows = pl.pallas_call(
        functools.partial(_prep_kernel, tm=tm, tiles_per_seq=S // tm),
        grid=(n_tiles,),
        in_specs=prep_specs,
        out_specs=out_specs,
        out_shape=out_shapes,
        scratch_shapes=[pltpu.VMEM((tm + 8, 3 * MIX), F32)],
        compiler_params=pltpu.CompilerParams(dimension_semantics=("arbitrary",),
                                             vmem_limit_bytes=VMEM_LIMIT),
        name="prep",
    )(*prep_args)

    tq = tm
    nq = S // tq
    mla_o = pl.pallas_call(
        functools.partial(_attn_kernel, tq=tq),
        grid=(B, nq),
        in_specs=[
            pl.BlockSpec((None, 2 * MIX, tq), lambda b, i: (b * nq + i, 0, 0)),
            pl.BlockSpec((None, S, 2 * MIX), lambda b, i: (b, 0, 0)),
            pl.BlockSpec((None, nq, MIX, tq), lambda b, i: (b, 0, 0, 0)),
            _const_spec((N_HEADS, 1, HEAD)),
        ],
        out_specs=pl.BlockSpec((None, tq, MIX), lambda b, i: (b, i, 0)),
        out_shape=jax.ShapeDtypeStruct((B, S, MIX), BF16),
        scratch_shapes=[pltpu.VMEM((N_HEADS, 1, tq), F32), pltpu.VMEM((N_HEADS, 1, tq), F32),
                        pltpu.VMEM((N_HEADS, HEAD, tq), F32)],
        compiler_params=pltpu.CompilerParams(
            dimension_semantics=("arbitrary", "arbitrary"),
            vmem_limit_bytes=VMEM_LIMIT),
        name="attn",
    )(qt, k.reshape(B, S, 2 * MIX), vt.reshape(B, nq, MIX, tq),
      p["mla_out_norm_w"].reshape(N_HEADS, 1, HEAD))

    n_chunks = S // CHUNK
    seq_spec = lambda w: pl.BlockSpec((None, S, w), lambda b: (b, 0, 0))
    gdn_o = pl.pallas_call(
        functools.partial(_gdn_kernel, n_chunks=n_chunks, unroll=GDN_UNROLL),
        grid=(B,),
        in_specs=[seq_spec(MIX)] * 4 + [
            seq_spec(LANES),
            pl.BlockSpec((None, n_chunks, 8, CHUNK), lambda b: (b, 0, 0, 0)),
            _const_spec((1, HEAD)),
        ],
        out_specs=seq_spec(MIX),
        out_shape=jax.ShapeDtypeStruct((B, S, MIX), BF16),
        scratch_shapes=[
            pltpu.VMEM((N_HEADS, HEAD, HEAD), F32),
            pltpu.VMEM((n_chunks * N_HEADS, CHUNK, HEAD), F32),
            pltpu.VMEM((n_chunks * N_HEADS, 2 * CHUNK, HEAD), BF16),
            pltpu.VMEM((n_chunks * N_HEADS, CHUNK, HEAD), BF16),
            pltpu.VMEM((n_chunks * N_HEADS, CHUNK, CHUNK), BF16),
        ],
        compiler_params=pltpu.CompilerParams(dimension_semantics=("arbitrary",),
                                             vmem_limit_bytes=VMEM_LIMIT),
        name="gdn",
    )(gq.reshape(B, S, MIX), gk.reshape(B, S, MIX), gv.reshape(B, S, MIX), gz.reshape(B, S, MIX),
      gb.reshape(B, S, LANES), rows.reshape(B, n_chunks, 8, CHUNK),
      p["gdn_norm_w"].reshape(1, HEAD))

    tm2 = MLP_TM
    w_out = p["w_out"].astype(BF16)
    tok_spec = lambda w: pl.BlockSpec((tm2, w), lambda i: (i, 0))
    out = pl.pallas_call(
        _mlp_kernel,
        grid=(T // tm2,),
        in_specs=[
            tok_spec(D_MODEL), tok_spec(MIX), tok_spec(MIX),
            _const_spec((MIX, D_MODEL)), _const_spec((MIX, D_MODEL)),
            _const_spec((1, D_MODEL)),
            _const_spec((D_MODEL, D_FF)), _const_spec((D_FF, D_MODEL)),
        ],
        out_specs=tok_spec(D_MODEL),
        out_shape=jax.ShapeDtypeStruct((T, D_MODEL), F32),
        compiler_params=pltpu.CompilerParams(dimension_semantics=("arbitrary",),
                                             vmem_limit_bytes=VMEM_LIMIT),
        name="mlp",
    )(x2, mla_o.reshape(T, MIX), gdn_o.reshape(T, MIX), w_out[:MIX], w_out[MIX:],
      p["mlp_norm_w"].reshape(1, D_MODEL), p["w_up"].astype(BF16), p["w_down"].astype(BF16))
    return out.reshape(B, S, D_MODEL)


def kernel(x, positions, attn_norm_w, w_in, q_lat_norm_w, w_uq, kv_lat_norm_w, w_ukv, q_norm_w,
           k_norm_w, mla_out_norm_w, conv_w, a_log, dt_bias, gdn_norm_w, w_out, mlp_norm_w, w_up,
           w_down):
    B, S, _ = x.shape
    half = ROPE // 2
    inv_freq = ROPE_THETA ** (-jnp.arange(half, dtype=F32) / half)
    inv_freq = jnp.tile(inv_freq, LANES // half).reshape(1, LANES)
    pos2 = positions.reshape(B * S, 1)
    params = dict(attn_norm_w=attn_norm_w, w_in=w_in, q_lat_norm_w=q_lat_norm_w, w_uq=w_uq,
                  kv_lat_norm_w=kv_lat_norm_w, w_ukv=w_ukv, q_norm_w=q_norm_w, k_norm_w=k_norm_w,
                  mla_out_norm_w=mla_out_norm_w, conv_w=conv_w, a_log=a_log, dt_bias=dt_bias,
                  gdn_norm_w=gdn_norm_w, w_out=w_out, mlp_norm_w=mlp_norm_w, w_up=w_up,
                  w_down=w_down)
    h = x
    for l in range(attn_norm_w.shape[0]):
        h = _layer(h, pos2, inv_freq, {name: val[l] for name, val in params.items()})
    return h
```

```python
import functools

import jax
import jax.numpy as jnp
from jax import lax
from jax.experimental import pallas as pl
from jax.experimental.pallas import tpu as pltpu

F32 = jnp.float32
BF16 = jnp.bfloat16

D_MODEL = 1024
N_HEADS = 4
LORA = 256
NOPE = 128
ROPE = 64
HEAD = 128
QK_HEAD = NOPE + ROPE
ROPE_THETA = 10000.0
CONV_W = 4
CHUNK = 64
D_FF = 4 * D_MODEL
EPS = 1e-6
LANES = 128
MIX = N_HEADS * HEAD

PREP_TM = 256
GDN_UNROLL = 4
MLP_TM = 512
MLP_FC = 1024
VMEM_LIMIT = 56 * 1024 * 1024


def _mm(a, b):
    return jnp.dot(a.astype(BF16), b.astype(BF16), preferred_element_type=F32)


def _mm_nt(a, b):
    return lax.dot_general(a.astype(BF16), b.astype(BF16), (((1,), (1,)), ((), ())),
                           preferred_element_type=F32)


def _mm_tn(a, b):
    return lax.dot_general(a.astype(BF16), b.astype(BF16), (((0,), (0,)), ((), ())),
                           preferred_element_type=F32)


def _split3(x):
    hi = x.astype(BF16)
    r1 = x - hi.astype(F32)
    mid = r1.astype(BF16)
    lo = (r1 - mid.astype(F32)).astype(BF16)
    return hi, mid, lo


def _interleave(*gens):
    live = list(gens)
    while live:
        for g in list(live):
            try:
                next(g)
            except StopIteration:
                live.remove(g)


def _rms(x, w, n):
    return x * lax.rsqrt(jnp.sum(x * x, axis=-1, keepdims=True) * (1.0 / n) + EPS) * w


def _sigmoid(x):
    return 1.0 / (1.0 + jnp.exp(-x))


def _rope(x, cosf, sinf, lane):
    rot = jnp.where(lane < ROPE // 2, pltpu.roll(x, LANES - ROPE // 2, 1),
                    pltpu.roll(x, ROPE // 2, 1))
    return x * cosf + rot * sinf


def _prep_kernel(x_ref, pos_ref, posr_ref, anw_ref, wlat_ref, wkpe_ref, wab_ref, wg_ref, wz_ref,
                 qlnw_ref, kvlnw_ref, wuqnt_ref, wuqpt_ref, wukk_ref, wukvt_ref,
                 qnwn_ref, qnwp_ref, knwn_ref, knwp_ref, invf_ref, invft_ref, convw_ref,
                 alog_ref, dtb_ref,
                 qt_out, k_out, vt_out, gq_out, gk_out, gv_out, z_out, gb_out, rows_out,
                 cbuf, *, tm, tiles_per_seq):
    i = pl.program_id(0)
    x = x_ref[...]
    xb = _rms(x, anw_ref[...], D_MODEL).astype(BF16)

    lane = lax.broadcasted_iota(jnp.int32, (tm, LANES), 1)
    ang = pos_ref[...].astype(F32) * invf_ref[...]
    in_rope = lane < ROPE
    cosf = jnp.where(in_rope, jnp.cos(ang), 0.0)
    sinf = jnp.where(in_rope, jnp.sin(ang), 0.0)
    sinf = jnp.where(lane < ROPE // 2, -sinf, sinf)
    ang_t = invft_ref[...] * posr_ref[...].astype(F32)
    cos_t = jnp.cos(ang_t)
    sin_t = jnp.sin(ang_t)

    lat = jnp.dot(xb, wlat_ref[...], preferred_element_type=F32)
    qn = _rms(lat[:, :LORA], qlnw_ref[...], LORA)
    kvn = _rms(lat[:, LORA:], kvlnw_ref[...], LORA)
    qn_t = qn.T.astype(BF16)
    kvn_t = kvn.T.astype(BF16)
    kvn = kvn.astype(BF16)
    qt_nope = jnp.dot(wuqnt_ref[...], qn_t, preferred_element_type=F32)
    qt_pe = jnp.dot(wuqpt_ref[...], qn_t, preferred_element_type=F32)
    vt_out[...] = jnp.dot(wukvt_ref[...], kvn_t, preferred_element_type=F32).astype(BF16)
    k_nope = jnp.dot(kvn, wukk_ref[...], preferred_element_type=F32)
    k_pe = jnp.dot(xb, wkpe_ref[...], preferred_element_type=F32)
    k_pe = _rope(_rms(k_pe, knwp_ref[...], ROPE), cosf, sinf, lane).astype(BF16)

    scale = QK_HEAD ** -0.5
    qwn = qnwn_ref[...] * scale
    qwp = qnwp_ref[...] * scale
    half = ROPE // 2
    for h in range(N_HEADS):
        lo = h * HEAD
        base = h * 2 * HEAD
        xn = qt_nope[lo:lo + NOPE, :]
        xn = xn * lax.rsqrt(jnp.sum(xn * xn, axis=0, keepdims=True) * (1.0 / NOPE) + EPS) * qwn
        xp = qt_pe[h * ROPE:(h + 1) * ROPE, :]
        xp = xp * lax.rsqrt(jnp.sum(xp * xp, axis=0, keepdims=True) * (1.0 / ROPE) + EPS) * qwp
        t1, t2 = xp[:half], xp[half:]
        qt_out[base:base + NOPE, :] = xn.astype(BF16)
        qt_out[base + NOPE:base + NOPE + half, :] = (t1 * cos_t - t2 * sin_t).astype(BF16)
        qt_out[base + NOPE + half:base + QK_HEAD, :] = (t2 * cos_t + t1 * sin_t).astype(BF16)
        qt_out[base + QK_HEAD:base + 2 * HEAD, :] = jnp.zeros((2 * HEAD - QK_HEAD, tm), BF16)
        k_out[:, base:base + HEAD] = _rms(k_nope[:, lo:lo + HEAD], knwn_ref[...], NOPE).astype(BF16)
        k_out[:, base + HEAD:base + 2 * HEAD] = k_pe

    @pl.when(i % tiles_per_seq == 0)
    def _():
        cbuf[0:8, :] = jnp.zeros((8, 3 * MIX), F32)

    g_all = jnp.dot(xb, wg_ref[...], preferred_element_type=F32)
    cbuf[8:8 + tm, :] = g_all
    conv = convw_ref[CONV_W - 1:CONV_W, :] * g_all
    for j in range(CONV_W - 1):
        conv = conv + convw_ref[j:j + 1, :] * cbuf[pl.ds(8 - (CONV_W - 1) + j, tm), :]
    cbuf[0:8, :] = g_all[tm - 8:tm, :]
    act = conv * _sigmoid(conv)
    for h in range(N_HEADS):
        lo = h * HEAD
        gq = act[:, lo:lo + HEAD]
        gk = act[:, MIX + lo:MIX + lo + HEAD]
        gq = gq * lax.rsqrt(jnp.sum(gq * gq, axis=-1, keepdims=True) + EPS) * (HEAD ** -0.5)
        gk = gk * lax.rsqrt(jnp.sum(gk * gk, axis=-1, keepdims=True) + EPS)
        gq_out[:, lo:lo + HEAD] = gq.astype(BF16)
        gk_out[:, lo:lo + HEAD] = gk.astype(BF16)
    gv_out[...] = act[:, 2 * MIX:].astype(BF16)
    z_out[...] = jnp.dot(xb, wz_ref[...], preferred_element_type=F32).astype(BF16)

    ab = jnp.dot(xb, wab_ref[...], preferred_element_type=F32)
    sp_in = ab + dtb_ref[...]
    softplus = jnp.maximum(sp_in, 0.0) + jnp.log(1.0 + jnp.exp(-jnp.abs(sp_in)))
    g = -jnp.exp(alog_ref[...]) * softplus
    beta = _sigmoid(ab)
    rr = lax.broadcasted_iota(jnp.int32, (tm, tm), 0)
    cc = lax.broadcasted_iota(jnp.int32, (tm, tm), 1)
    tri = jnp.where((cc <= rr) & ((cc // CHUNK) == (rr // CHUNK)), 1.0, 0.0).astype(BF16)
    g_hi, g_mid, g_lo = _split3(g)
    gcum = (jnp.dot(tri, g_hi, preferred_element_type=F32)
            + jnp.dot(tri, g_mid, preferred_element_type=F32)
            + jnp.dot(tri, g_lo, preferred_element_type=F32))
    gb = jnp.where(lane < N_HEADS, gcum, jnp.where(lane < 2 * N_HEADS, beta, 0.0))
    gb_out[...] = gb
    er = lax.broadcasted_iota(jnp.int32, (8, LANES), 0)
    ec = lax.broadcasted_iota(jnp.int32, (8, LANES), 1)
    eye8 = jnp.where(er == ec, 1.0, 0.0).astype(BF16)
    for c in range(tm // CHUNK):
        parts = _split3(gb[c * CHUNK:(c + 1) * CHUNK, :])
        rows = None
        for p in parts:
            t = lax.dot_general(eye8, p, (((1,), (1,)), ((), ())), preferred_element_type=F32)
            rows = t if rows is None else rows + t
        rows_out[c] = rows


def _attn_kernel(qt_ref, k_ref, vt_ref, w_ref, o_ref, m_s, l_s, acc_s, *, tq):
    i = pl.program_id(1)
    heads = range(N_HEADS)
    ones_rows = 16
    m_s[...] = jnp.full(m_s.shape, -jnp.inf, F32)
    l_s[...] = jnp.zeros(l_s.shape, F32)
    acc_s[...] = jnp.zeros(acc_s.shape, F32)

    def block(j, masked):
        r0 = pl.multiple_of(j * tq, tq)
        st = [jnp.dot(k_ref[pl.ds(r0, tq), 2 * HEAD * h:2 * HEAD * (h + 1)],
                      qt_ref[2 * HEAD * h:2 * HEAD * (h + 1), :],
                      preferred_element_type=F32) for h in heads]
        if masked:
            kk = lax.broadcasted_iota(jnp.int32, (tq, tq), 0)
            qq = lax.broadcasted_iota(jnp.int32, (tq, tq), 1)
            st = [jnp.where(kk <= qq, x, -jnp.inf) for x in st]
        m_old = [m_s[h] for h in heads]
        m_new = [jnp.maximum(m_old[h], jnp.max(st[h], axis=0, keepdims=True)) for h in heads]
        alpha = [jnp.exp(m_old[h] - m_new[h]) for h in heads]
        p = [jnp.exp(st[h] - m_new[h]).astype(BF16) for h in heads]
        ones = jnp.ones((ones_rows, tq), BF16)
        pv = [jnp.dot(jnp.concatenate([vt_ref[j, HEAD * h:HEAD * (h + 1), :], ones], axis=0), p[h],
                      preferred_element_type=F32) for h in heads]
        for h in heads:
            m_s[h] = m_new[h]
            l_s[h] = alpha[h] * l_s[h] + pv[h][HEAD:HEAD + 1]
            acc_s[h] = alpha[h] * acc_s[h] + pv[h][:HEAD]

    def body(j, carry):
        block(j, False)
        return carry

    lax.fori_loop(0, i, body, 0)
    block(i, True)
    for h in heads:
        o = (acc_s[h] / l_s[h]).T
        o_ref[:, HEAD * h:HEAD * (h + 1)] = _rms(o, w_ref[h], HEAD).astype(BF16)


def _gdn_kernel(q_ref, k_ref, v_ref, z_ref, gb_ref, rows_ref, gnw_ref, o_ref,
                s_ref, u_s, wq_s, kd_s, at_s, *, n_chunks, unroll):
    ii = lax.broadcasted_iota(jnp.int32, (CHUNK, CHUNK), 0)
    jj = lax.broadcasted_iota(jnp.int32, (CHUNK, CHUNK), 1)
    eye = jnp.where(ii == jj, 1.0, 0.0)
    heads = range(N_HEADS)

    def solve(t):
        ns = [t * unroll + c for c in range(unroll)]
        r0 = [pl.multiple_of(n * CHUNK, CHUNK) for n in ns]
        scal = [gb_ref[pl.ds(r, CHUNK), :] for r in r0]
        rows = [rows_ref[n] for n in ns]
        items = [(c, h) for c in range(unroll) for h in heads]
        cols = lambda h: slice(h * HEAD, (h + 1) * HEAD)
        q = [q_ref[pl.ds(r0[c], CHUNK), cols(h)] for c, h in items]
        k = [k_ref[pl.ds(r0[c], CHUNK), cols(h)] for c, h in items]
        v = [v_ref[pl.ds(r0[c], CHUNK), cols(h)] for c, h in items]
        g_col = [jnp.broadcast_to(scal[c][:, h:h + 1], (CHUNK, LANES)) for c, h in items]
        b_col = [jnp.broadcast_to(scal[c][:, N_HEADS + h:N_HEADS + h + 1], (CHUNK, CHUNK))
                 for c, h in items]
        g_row = [rows[c][h:h + 1, :] for c, h in items]
        b_row = [rows[c][N_HEADS + h:N_HEADS + h + 1, :] for c, h in items]
        n_it = range(len(items))
        decay = [jnp.exp(jnp.where(ii >= jj, g_col[x][:, :CHUNK] - g_row[x], -jnp.inf)) for x in n_it]
        qkk = [_mm_nt(jnp.concatenate([q[x], k[x]], axis=0), k[x]) for x in n_it]
        yield
        lmat = [jnp.where(ii > jj, b_col[x] * qkk[x][CHUNK:] * decay[x], 0.0) for x in n_it]
        inv = [eye - lmat[x] for x in n_it]
        pw = [_mm(lmat[x], lmat[x]) for x in n_it]
        yield
        for step in range(5):
            inv = [inv[x] + _mm(inv[x], pw[x]) for x in n_it]
            if step < 4:
                pw = [_mm(pw[x], pw[x]) for x in n_it]
            yield
        u = [_mm(inv[x] * b_row[x], v[x]) for x in n_it]
        w = [_mm(inv[x] * (b_row[x] * jnp.exp(g_row[x])), k[x]) for x in n_it]
        yield
        for x, (c, h) in enumerate(items):
            idx = ns[c] * N_HEADS + h
            g_last = g_col[x][CHUNK - 1:CHUNK, :]
            u_s[idx] = u[x]
            wq_s[idx, 0:CHUNK, :] = w[x].astype(BF16)
            wq_s[idx, CHUNK:2 * CHUNK, :] = (q[x].astype(F32) * jnp.exp(g_col[x])).astype(BF16)
            kd_s[idx] = (k[x].astype(F32) * jnp.exp(g_last - g_col[x])).astype(BF16)
            at_s[idx] = (qkk[x][:CHUNK] * decay[x]).astype(BF16)

    gnw = gnw_ref[...]

    def scan(t):
        for c in range(unroll):
            n = t * unroll + c
            r0 = pl.multiple_of(n * CHUNK, CHUNK)
            g_last = gb_ref[pl.ds(r0 + CHUNK - 1, 1), :]
            state = [s_ref[h] for h in heads]
            sb = [x.astype(BF16) for x in state]
            ws = [_mm(wq_s[n * N_HEADS + h], sb[h]) for h in heads]
            yield
            v_new = [(u_s[n * N_HEADS + h] - ws[h][:CHUNK]).astype(BF16) for h in heads]
            o = [ws[h][CHUNK:] + _mm(at_s[n * N_HEADS + h], v_new[h]) for h in heads]
            upd = [_mm_tn(kd_s[n * N_HEADS + h], v_new[h]) for h in heads]
            yield
            for h in heads:
                c_dec = jnp.exp(jnp.broadcast_to(g_last[:, h:h + 1], (HEAD, HEAD)))
                s_ref[h] = state[h] * c_dec + upd[h]
                zh = z_ref[pl.ds(r0, CHUNK), h * HEAD:(h + 1) * HEAD].astype(F32)
                gated = _rms(o[h], gnw, HEAD) * (zh * _sigmoid(zh))
                o_ref[pl.ds(r0, CHUNK), h * HEAD:(h + 1) * HEAD] = gated.astype(BF16)

    n_groups = n_chunks // unroll
    s_ref[...] = jnp.zeros_like(s_ref)
    _interleave(solve(0))

    def body(t, carry):
        _interleave(solve(t), scan(t - 1))
        return carry

    lax.fori_loop(1, n_groups, body, 0)
    _interleave(scan(n_groups - 1))


def _mlp_kernel(x_ref, mla_ref, gdn_ref, woa_ref, wob_ref, nw_ref, wup_ref, wdn_ref, o_ref):
    h = (x_ref[...]
         + jnp.dot(mla_ref[...], woa_ref[...], preferred_element_type=F32)
         + jnp.dot(gdn_ref[...], wob_ref[...], preferred_element_type=F32))
    hn = _rms(h, nw_ref[...], D_MODEL).astype(BF16)
    o_ref[...] = h
    for c in range(D_FF // MLP_FC):
        u = jnp.dot(hn, wup_ref[:, c * MLP_FC:(c + 1) * MLP_FC], preferred_element_type=F32)
        a = jnp.square(jnp.maximum(u, 0.0)).astype(BF16)
        o_ref[...] += jnp.dot(a, wdn_ref[c * MLP_FC:(c + 1) * MLP_FC, :], preferred_element_type=F32)


def _const_spec(shape):
    nd = len(shape)
    return pl.BlockSpec(shape, lambda *_: (0,) * nd, pipeline_mode=pl.Buffered(1))


def _pad_lanes(v, width=LANES):
    v = v.reshape(1, -1).astype(F32)
    return jnp.pad(v, ((0, 0), (0, width - v.shape[1])))


def _layer(h, pos2, inv_freq, p):
    B, S, _ = h.shape
    T = B * S
    x2 = h.reshape(T, D_MODEL)

    w_in = p["w_in"]
    o_q, o_kv, o_pe = 0, LORA, 2 * LORA
    o_g = o_pe + ROPE
    o_z = o_g + 3 * MIX
    o_a = o_z + MIX
    w_lat = w_in[:, o_q:o_pe].astype(BF16)
    w_kpe = jnp.pad(w_in[:, o_pe:o_g], ((0, 0), (0, LANES - ROPE))).astype(BF16)
    w_g = w_in[:, o_g:o_z].astype(BF16)
    w_z = w_in[:, o_z:o_a].astype(BF16)
    w_ab = jnp.pad(w_in[:, o_a:], ((0, 0), (0, LANES - 2 * N_HEADS))).astype(BF16)
    w_uq = p["w_uq"].reshape(LORA, N_HEADS, QK_HEAD)
    w_uq_nt = w_uq[:, :, :NOPE].reshape(LORA, N_HEADS * NOPE).T.astype(BF16)
    w_uq_pt = w_uq[:, :, NOPE:].reshape(LORA, N_HEADS * ROPE).T.astype(BF16)
    w_ukv = p["w_ukv"].reshape(LORA, N_HEADS, NOPE + HEAD)
    w_uk_k = w_ukv[:, :, :NOPE].reshape(LORA, N_HEADS * NOPE).astype(BF16)
    w_uk_vt = w_ukv[:, :, NOPE:].reshape(LORA, N_HEADS * HEAD).T.astype(BF16)

    tm = PREP_TM
    n_tiles = T // tm
    row_spec = lambda w: pl.BlockSpec((tm, w), lambda i: (i, 0))
    col_bcast = lambda v: jnp.broadcast_to(v.astype(F32)[:, None], (v.shape[0], tm))
    prep_in = [
        (x2, row_spec(D_MODEL)),
        (pos2, row_spec(1)),
        (pos2.reshape(n_tiles, 1, tm), pl.BlockSpec((None, 1, tm), lambda i: (i, 0, 0))),
        (p["attn_norm_w"].reshape(1, D_MODEL), None),
        (w_lat, None), (w_kpe, None), (w_ab, None), (w_g, None), (w_z, None),
        (p["q_lat_norm_w"].reshape(1, LORA), None),
        (p["kv_lat_norm_w"].reshape(1, LORA), None),
        (w_uq_nt, None), (w_uq_pt, None), (w_uk_k, None), (w_uk_vt, None),
        (col_bcast(p["q_norm_w"][:NOPE]), None),
        (col_bcast(p["q_norm_w"][NOPE:]), None),
        (p["k_norm_w"][:NOPE].reshape(1, NOPE), None),
        (_pad_lanes(p["k_norm_w"][NOPE:]), None),
        (inv_freq, None),
        (col_bcast(inv_freq[0, :ROPE // 2]), None),
        (p["conv_w"], None),
        (_pad_lanes(p["a_log"]), None),
        (_pad_lanes(p["dt_bias"]), None),
    ]
    prep_args = [a for a, _ in prep_in]
    prep_specs = [s if s is not None else _const_spec(a.shape) for a, s in prep_in]
    out_shapes = [
        jax.ShapeDtypeStruct((n_tiles, 2 * MIX, tm), BF16),
        jax.ShapeDtypeStruct((T, 2 * MIX), BF16),
        jax.ShapeDtypeStruct((n_tiles, MIX, tm), BF16),
        jax.ShapeDtypeStruct((T, MIX), BF16),
        jax.ShapeDtypeStruct((T, MIX), BF16),
        jax.ShapeDtypeStruct((T, MIX), BF16),
        jax.ShapeDtypeStruct((T, MIX), BF16),
        jax.ShapeDtypeStruct((T, LANES), F32),
        jax.ShapeDtypeStruct((T // CHUNK, 8, CHUNK), F32),
    ]
    tile_spec = lambda r: pl.BlockSpec((None, r, tm), lambda i: (i, 0, 0))
    out_specs = [tile_spec(2 * MIX), row_spec(2 * MIX), tile_spec(MIX)] + [row_spec(MIX)] * 4 + [
        row_spec(LANES),
        pl.BlockSpec((tm // CHUNK, 8, CHUNK), lambda i: (i, 0, 0)),
    ]
    qt, k, vt, gq, gk, gv, gz, gb, rows = pl.pallas_call(
        functools.partial(_prep_kernel, tm=tm, tiles_per_seq=S // tm),
        grid=(n_tiles,),
        in_specs=prep_specs,
        out_specs=out_specs,
        out_shape=out_shapes,
        scratch_shapes=[pltpu.VMEM((tm + 8, 3 * MIX), F32)],
        compiler_params=pltpu.CompilerParams(dimension_semantics=("arbitrary",),
                                             vmem_limit_bytes=VMEM_LIMIT),
        name="prep",
    )(*prep_args)

    tq = tm
    nq = S // tq
    mla_o = pl.pallas_call(
        functools.partial(_attn_kernel, tq=tq),
        grid=(B, nq),
        in_specs=[
            pl.BlockSpec((None, 2 * MIX, tq), lambda b, i: (b * nq + i, 0, 0)),
            pl.BlockSpec((None, S, 2 * MIX), lambda b, i: (b, 0, 0)),
            pl.BlockSpec((None, nq, MIX, tq), lambda b, i: (b, 0, 0, 0)),
            _const_spec((N_HEADS, 1, HEAD)),
        ],
        out_specs=pl.BlockSpec((None, tq, MIX), lambda b, i: (b, i, 0)),
        out_shape=jax.ShapeDtypeStruct((B, S, MIX), BF16),
        scratch_shapes=[pltpu.VMEM((N_HEADS, 1, tq), F32), pltpu.VMEM((N_HEADS, 1, tq), F32),
                        pltpu.VMEM((N_HEADS, HEAD, tq), F32)],
        compiler_params=pltpu.CompilerParams(
            dimension_semantics=("arbitrary", "arbitrary"),
            vmem_limit_bytes=VMEM_LIMIT),
        name="attn",
    )(qt, k.reshape(B, S, 2 * MIX), vt.reshape(B, nq, MIX, tq),
      p["mla_out_norm_w"].reshape(N_HEADS, 1, HEAD))

    n_chunks = S // CHUNK
    seq_spec = lambda w: pl.BlockSpec((None, S, w), lambda b: (b, 0, 0))
    gdn_o = pl.pallas_call(
        functools.partial(_gdn_kernel, n_chunks=n_chunks, unroll=GDN_UNROLL),
        grid=(B,),
        in_specs=[seq_spec(MIX)] * 4 + [
            seq_spec(LANES),
            pl.BlockSpec((None, n_chunks, 8, CHUNK), lambda b: (b, 0, 0, 0)),
            _const_spec((1, HEAD)),
        ],
        out_specs=seq_spec(MIX),
        out_shape=jax.ShapeDtypeStruct((B, S, MIX), BF16),
        scratch_shapes=[
            pltpu.VMEM((N_HEADS, HEAD, HEAD), F32),
            pltpu.VMEM((n_chunks * N_HEADS, CHUNK, HEAD), F32),
            pltpu.VMEM((n_chunks * N_HEADS, 2 * CHUNK, HEAD), BF16),
            pltpu.VMEM((n_chunks * N_HEADS, CHUNK, HEAD), BF16),
            pltpu.VMEM((n_chunks * N_HEADS, CHUNK, CHUNK), BF16),
        ],
        compiler_params=pltpu.CompilerParams(dimension_semantics=("arbitrary",),
                                             vmem_limit_bytes=VMEM_LIMIT),
        name="gdn",
    )(gq.reshape(B, S, MIX), gk.reshape(B, S, MIX), gv.reshape(B, S, MIX), gz.reshape(B, S, MIX),
      gb.reshape(B, S, LANES), rows.reshape(B, n_chunks, 8, CHUNK),
      p["gdn_norm_w"].reshape(1, HEAD))

    tm2 = MLP_TM
    w_out = p["w_out"].astype(BF16)
    tok_spec = lambda w: pl.BlockSpec((tm2, w), lambda i: (i, 0))
    out = pl.pallas_call(
        _mlp_kernel,
        grid=(T // tm2,),
        in_specs=[
            tok_spec(D_MODEL), tok_spec(MIX), tok_spec(MIX),
            _const_spec((MIX, D_MODEL)), _const_spec((MIX, D_MODEL)),
            _const_spec((1, D_MODEL)),
            _const_spec((D_MODEL, D_FF)), _const_spec((D_FF, D_MODEL)),
        ],
        out_specs=tok_spec(D_MODEL),
        out_shape=jax.ShapeDtypeStruct((T, D_MODEL), F32),
        compiler_params=pltpu.CompilerParams(dimension_semantics=("arbitrary",),
                                             vmem_limit_bytes=VMEM_LIMIT),
        name="mlp",
    )(x2, mla_o.reshape(T, MIX), gdn_o.reshape(T, MIX), w_out[:MIX], w_out[MIX:],
      p["mlp_norm_w"].reshape(1, D_MODEL), p["w_up"].astype(BF16), p["w_down"].astype(BF16))
    return out.reshape(B, S, D_MODEL)


def kernel(x, positions, attn_norm_w, w_in, q_lat_norm_w, w_uq, kv_lat_norm_w, w_ukv, q_norm_w,
           k_norm_w, mla_out_norm_w, conv_w, a_log, dt_bias, gdn_norm_w, w_out, mlp_norm_w, w_up,
           w_down):
    B, S, _ = x.shape
    half = ROPE // 2
    inv_freq = ROPE_THETA ** (-jnp.arange(half, dtype=F32) / half)
    inv_freq = jnp.tile(inv_freq, LANES // half).reshape(1, LANES)
    pos2 = positions.reshape(B * S, 1)
    params = dict(attn_norm_w=attn_norm_w, w_in=w_in, q_lat_norm_w=q_lat_norm_w, w_uq=w_uq,
                  kv_lat_norm_w=kv_lat_norm_w, w_ukv=w_ukv, q_norm_w=q_norm_w, k_norm_w=k_norm_w,
                  mla_out_norm_w=mla_out_norm_w, conv_w=conv_w, a_log=a_log, dt_bias=dt_bias,
                  gdn_norm_w=gdn_norm_w, w_out=w_out, mlp_norm_w=mlp_norm_w, w_up=w_up,
                  w_down=w_down)
    h = x
    for l in range(attn_norm_w.shape[0]):
        h = _layer(h, pos2, inv_freq, {name: val[l] for name, val in params.items()})
    return h
```

```python
import functools

import jax
import jax.numpy as jnp
from jax import lax
from jax.experimental import pallas as pl
from jax.experimental.pallas import tpu as pltpu

F32 = jnp.float32
BF16 = jnp.bfloat16

D_MODEL = 1024
N_HEADS = 4
LORA = 256
NOPE = 128
ROPE = 64
HEAD = 128
QK_HEAD = NOPE + ROPE
ROPE_THETA = 10000.0
CONV_W = 4
CHUNK = 64
D_FF = 4 * D_MODEL
EPS = 1e-6
LANES = 128
MIX = N_HEADS * HEAD

PREP_TM = 256
GDN_UNROLL = 4
MLP_TM = 512
MLP_FC = 1024
VMEM_LIMIT = 56 * 1024 * 1024


def _mm(a, b):
    return jnp.dot(a.astype(BF16), b.astype(BF16), preferred_element_type=F32)


def _mm_nt(a, b):
    return lax.dot_general(a.astype(BF16), b.astype(BF16), (((1,), (1,)), ((), ())),
                           preferred_element_type=F32)


def _mm_tn(a, b):
    return lax.dot_general(a.astype(BF16), b.astype(BF16), (((0,), (0,)), ((), ())),
                           preferred_element_type=F32)


def _split3(x):
    hi = x.astype(BF16)
    r1 = x - hi.astype(F32)
    mid = r1.astype(BF16)
    lo = (r1 - mid.astype(F32)).astype(BF16)
    return hi, mid, lo


def _interleave(*gens):
    live = list(gens)
    while live:
        for g in list(live):
            try:
                next(g)
            except StopIteration:
                live.remove(g)


def _rms(x, w, n):
    return x * lax.rsqrt(jnp.sum(x * x, axis=-1, keepdims=True) * (1.0 / n) + EPS) * w


def _sigmoid(x):
    return 1.0 / (1.0 + jnp.exp(-x))


def _rope(x, cosf, sinf, lane):
    rot = jnp.where(lane < ROPE // 2, pltpu.roll(x, LANES - ROPE // 2, 1),
                    pltpu.roll(x, ROPE // 2, 1))
    return x * cosf + rot * sinf


def _prep_kernel(x_ref, posr_ref, anw_ref, wlat_ref, wkpe_ref, wab_ref, wg_ref, wz_ref,
                 qlnw_ref, kvlnw_ref, wuqnt_ref, wuqpt_ref, wukk_ref, wukvt_ref,
                 qnwn_ref, qnwp_ref, knwn_ref, knwp_ref, invft_ref, convw_ref,
                 alog_ref, dtb_ref,
                 qt_out, k_out, vt_out, gq_out, gk_out, gv_out, z_out, gb_out, rows_out,
                 tail, *, tm, tiles_per_seq):
    i = pl.program_id(0)
    half = ROPE // 2
    x = x_ref[...]
    xb = _rms(x, anw_ref[...], D_MODEL).astype(BF16)

    lat = jnp.dot(xb, wlat_ref[...], preferred_element_type=F32)
    g_all = jnp.dot(xb, wg_ref[...], preferred_element_type=F32)
    k_pe = jnp.dot(xb, wkpe_ref[...], preferred_element_type=F32)
    ab = jnp.dot(xb, wab_ref[...], preferred_element_type=F32)
    z_out[...] = jnp.dot(xb, wz_ref[...], preferred_element_type=F32).astype(BF16)

    ang_t = invft_ref[...] * posr_ref[...].astype(F32)
    cos_t = jnp.cos(ang_t)
    sin_t = jnp.sin(ang_t)
    lane = lax.broadcasted_iota(jnp.int32, (tm, LANES), 1)
    table = jnp.concatenate([cos_t, sin_t, jnp.zeros((LANES - ROPE, tm), F32)], axis=0).T
    cosf = jnp.where(lane < half, table, jnp.where(lane < ROPE, pltpu.roll(table, half, 1), 0.0))
    sinf = jnp.where(lane < half, -pltpu.roll(table, LANES - half, 1),
                     jnp.where(lane < ROPE, table, 0.0))

    qn = _rms(lat[:, :LORA], qlnw_ref[...], LORA)
    kvn = _rms(lat[:, LORA:], kvlnw_ref[...], LORA)
    qn_t = qn.T.astype(BF16)
    kvn_t = kvn.T.astype(BF16)
    kvn = kvn.astype(BF16)
    qt_nope = jnp.dot(wuqnt_ref[...], qn_t, preferred_element_type=F32)
    qt_pe = jnp.dot(wuqpt_ref[...], qn_t, preferred_element_type=F32)
    vt_out[...] = jnp.dot(wukvt_ref[...], kvn_t, preferred_element_type=F32).astype(BF16)
    k_nope = jnp.dot(kvn, wukk_ref[...], preferred_element_type=F32)
    k_pe = _rope(_rms(k_pe, knwp_ref[...], ROPE), cosf, sinf, lane).astype(BF16)

    scale = QK_HEAD ** -0.5
    qwn = qnwn_ref[...] * scale
    qwp = qnwp_ref[...] * scale
    for h in range(N_HEADS):
        lo = h * HEAD
        base = h * 2 * HEAD
        xn = qt_nope[lo:lo + NOPE, :]
        xn = xn * lax.rsqrt(jnp.sum(xn * xn, axis=0, keepdims=True) * (1.0 / NOPE) + EPS) * qwn
        xp = qt_pe[h * ROPE:(h + 1) * ROPE, :]
        xp = xp * lax.rsqrt(jnp.sum(xp * xp, axis=0, keepdims=True) * (1.0 / ROPE) + EPS) * qwp
        t1, t2 = xp[:half], xp[half:]
        qt_out[base:base + NOPE, :] = xn.astype(BF16)
        qt_out[base + NOPE:base + NOPE + half, :] = (t1 * cos_t - t2 * sin_t).astype(BF16)
        qt_out[base + NOPE + half:base + QK_HEAD, :] = (t2 * cos_t + t1 * sin_t).astype(BF16)
        qt_out[base + QK_HEAD:base + 2 * HEAD, :] = jnp.zeros((2 * HEAD - QK_HEAD, tm), BF16)
        k_out[:, base:base + HEAD] = _rms(k_nope[:, lo:lo + HEAD], knwn_ref[...], NOPE).astype(BF16)
        k_out[:, base + HEAD:base + 2 * HEAD] = k_pe

    @pl.when(i % tiles_per_seq == 0)
    def _():
        tail[...] = jnp.zeros(tail.shape, F32)

    prev = tail[...]
    row8 = lax.broadcasted_iota(jnp.int32, prev.shape, 0)
    conv = convw_ref[CONV_W - 1:CONV_W, :] * g_all
    for s in range(1, CONV_W):
        rolled = pltpu.roll(g_all, s, 0)
        first = jnp.where(row8 < s, pltpu.roll(prev, s, 0), rolled[:8])
        shifted = jnp.concatenate([first, rolled[8:]], axis=0)
        conv = conv + convw_ref[CONV_W - 1 - s:CONV_W - s, :] * shifted
    tail[...] = g_all[tm - 8:tm, :]
    act = conv * _sigmoid(conv)
    for h in range(N_HEADS):
        lo = h * HEAD
        gq = act[:, lo:lo + HEAD]
        gk = act[:, MIX + lo:MIX + lo + HEAD]
        gq = gq * lax.rsqrt(jnp.sum(gq * gq, axis=-1, keepdims=True) + EPS) * (HEAD ** -0.5)
        gk = gk * lax.rsqrt(jnp.sum(gk * gk, axis=-1, keepdims=True) + EPS)
        gq_out[:, lo:lo + HEAD] = gq.astype(BF16)
        gk_out[:, lo:lo + HEAD] = gk.astype(BF16)
    gv_out[...] = act[:, 2 * MIX:].astype(BF16)

    sp_in = ab + dtb_ref[...]
    softplus = jnp.maximum(sp_in, 0.0) + jnp.log(1.0 + jnp.exp(-jnp.abs(sp_in)))
    g = -jnp.exp(alog_ref[...]) * softplus
    beta = _sigmoid(ab)
    rr = lax.broadcasted_iota(jnp.int32, (tm, tm), 0)
    cc = lax.broadcasted_iota(jnp.int32, (tm, tm), 1)
    tri = jnp.where((cc <= rr) & ((cc // CHUNK) == (rr // CHUNK)), 1.0, 0.0).astype(BF16)
    g_hi, g_mid, g_lo = _split3(g)
    gcum = (jnp.dot(tri, g_hi, preferred_element_type=F32)
            + jnp.dot(tri, g_mid, preferred_element_type=F32)
            + jnp.dot(tri, g_lo, preferred_element_type=F32))
    gb = jnp.where(lane < N_HEADS, gcum, jnp.where(lane < 2 * N_HEADS, beta, 0.0))
    gb_out[...] = gb
    er = lax.broadcasted_iota(jnp.int32, (8, LANES), 0)
    ec = lax.broadcasted_iota(jnp.int32, (8, LANES), 1)
    eye8 = jnp.where(er == ec, 1.0, 0.0).astype(BF16)
    for c in range(tm // CHUNK):
        parts = _split3(gb[c * CHUNK:(c + 1) * CHUNK, :])
        rows = None
        for p in parts:
            t = lax.dot_general(eye8, p, (((1,), (1,)), ((), ())), preferred_element_type=F32)
            rows = t if rows is None else rows + t
        rows_out[c] = rows


def _attn_kernel(qt_ref, k_ref, vt_ref, w_ref, o_ref, m_s, l_s, acc_s, *, tq):
    i = pl.program_id(1)
    heads = range(N_HEADS)
    ones_rows = 16
    m_s[...] = jnp.full(m_s.shape, -jnp.inf, F32)
    l_s[...] = jnp.zeros(l_s.shape, F32)
    acc_s[...] = jnp.zeros(acc_s.shape, F32)

    def block(j, masked):
        r0 = pl.multiple_of(j * tq, tq)
        st = [jnp.dot(k_ref[pl.ds(r0, tq), 2 * HEAD * h:2 * HEAD * (h + 1)],
                      qt_ref[2 * HEAD * h:2 * HEAD * (h + 1), :],
                      preferred_element_type=F32) for h in heads]
        if masked:
            kk = lax.broadcasted_iota(jnp.int32, (tq, tq), 0)
            qq = lax.broadcasted_iota(jnp.int32, (tq, tq), 1)
            st = [jnp.where(kk <= qq, x, -jnp.inf) for x in st]
        m_old = [m_s[h] for h in heads]
        m_new = [jnp.maximum(m_old[h], jnp.max(st[h], axis=0, keepdims=True)) for h in heads]
        alpha = [jnp.exp(m_old[h] - m_new[h]) for h in heads]
        p = [jnp.exp(st[h] - m_new[h]).astype(BF16) for h in heads]
        ones = jnp.ones((ones_rows, tq), BF16)
        pv = [jnp.dot(jnp.concatenate([vt_ref[j, HEAD * h:HEAD * (h + 1), :], ones], axis=0), p[h],
                      preferred_element_type=F32) for h in heads]
        for h in heads:
            m_s[h] = m_new[h]
            l_s[h] = alpha[h] * l_s[h] + pv[h][HEAD:HEAD + 1]
            acc_s[h] = alpha[h] * acc_s[h] + pv[h][:HEAD]

    def body(j, carry):
        block(j, False)
        return carry

    lax.fori_loop(0, i, body, 0)
    block(i, True)
    for h in heads:
        o = (acc_s[h] / l_s[h]).T
        o_ref[:, HEAD * h:HEAD * (h + 1)] = _rms(o, w_ref[h], HEAD).astype(BF16)


def _gdn_kernel(q_ref, k_ref, v_ref, z_ref, gb_ref, rows_ref, gnw_ref, o_ref,
                s_ref, u_s, wq_s, kd_s, at_s, *, n_chunks, unroll):
    ii = lax.broadcasted_iota(jnp.int32, (CHUNK, CHUNK), 0)
    jj = lax.broadcasted_iota(jnp.int32, (CHUNK, CHUNK), 1)
    eye = jnp.where(ii == jj, 1.0, 0.0)
    heads = range(N_HEADS)

    def solve(t):
        ns = [t * unroll + c for c in range(unroll)]
        r0 = [pl.multiple_of(n * CHUNK, CHUNK) for n in ns]
        scal = [gb_ref[pl.ds(r, CHUNK), :] for r in r0]
        rows = [rows_ref[n] for n in ns]
        items = [(c, h) for c in range(unroll) for h in heads]
        cols = lambda h: slice(h * HEAD, (h + 1) * HEAD)
        q = [q_ref[pl.ds(r0[c], CHUNK), cols(h)] for c, h in items]
        k = [k_ref[pl.ds(r0[c], CHUNK), cols(h)] for c, h in items]
        v = [v_ref[pl.ds(r0[c], CHUNK), cols(h)] for c, h in items]
        g_col = [jnp.broadcast_to(scal[c][:, h:h + 1], (CHUNK, LANES)) for c, h in items]
        b_col = [jnp.broadcast_to(scal[c][:, N_HEADS + h:N_HEADS + h + 1], (CHUNK, CHUNK))
                 for c, h in items]
        g_row = [rows[c][h:h + 1, :] for c, h in items]
        b_row = [rows[c][N_HEADS + h:N_HEADS + h + 1, :] for c, h in items]
        n_it = range(len(items))
        decay = [jnp.exp(jnp.where(ii >= jj, g_col[x][:, :CHUNK] - g_row[x], -jnp.inf)) for x in n_it]
        qkk = [_mm_nt(jnp.concatenate([q[x], k[x]], axis=0), k[x]) for x in n_it]
        yield
        lmat = [jnp.where(ii > jj, b_col[x] * qkk[x][CHUNK:] * decay[x], 0.0) for x in n_it]
        inv = [eye - lmat[x] for x in n_it]
        pw = [_mm(lmat[x], lmat[x]) for x in n_it]
        yield
        for step in range(5):
            inv = [inv[x] + _mm(inv[x], pw[x]) for x in n_it]
            if step < 4:
                pw = [_mm(pw[x], pw[x]) for x in n_it]
            yield
        u = [_mm(inv[x] * b_row[x], v[x]) for x in n_it]
        w = [_mm(inv[x] * (b_row[x] * jnp.exp(g_row[x])), k[x]) for x in n_it]
        yield
        for x, (c, h) in enumerate(items):
            idx = ns[c] * N_HEADS + h
            g_last = g_col[x][CHUNK - 1:CHUNK, :]
            u_s[idx] = u[x]
            wq_s[idx, 0:CHUNK, :] = w[x].astype(BF16)
            wq_s[idx, CHUNK:2 * CHUNK, :] = (q[x].astype(F32) * jnp.exp(g_col[x])).astype(BF16)
            kd_s[idx] = (k[x].astype(F32) * jnp.exp(g_last - g_col[x])).astype(BF16)
            at_s[idx] = (qkk[x][:CHUNK] * decay[x]).astype(BF16)

    gnw = gnw_ref[...]

    def scan(t):
        for c in range(unroll):
            n = t * unroll + c
            r0 = pl.multiple_of(n * CHUNK, CHUNK)
            g_last = gb_ref[pl.ds(r0 + CHUNK - 1, 1), :]
            state = [s_ref[h] for h in heads]
            sb = [x.astype(BF16) for x in state]
            ws = [_mm(wq_s[n * N_HEADS + h], sb[h]) for h in heads]
            yield
            v_new = [(u_s[n * N_HEADS + h] - ws[h][:CHUNK]).astype(BF16) for h in heads]
            o = [ws[h][CHUNK:] + _mm(at_s[n * N_HEADS + h], v_new[h]) for h in heads]
            upd = [_mm_tn(kd_s[n * N_HEADS + h], v_new[h]) for h in heads]
            yield
            for h in heads:
                c_dec = jnp.exp(jnp.broadcast_to(g_last[:, h:h + 1], (HEAD, HEAD)))
                s_ref[h] = state[h] * c_dec + upd[h]
                zh = z_ref[pl.ds(r0, CHUNK), h * HEAD:(h + 1) * HEAD].astype(F32)
                gated = _rms(o[h], gnw, HEAD) * (zh * _sigmoid(zh))
                o_ref[pl.ds(r0, CHUNK), h * HEAD:(h + 1) * HEAD] = gated.astype(BF16)

    n_groups = n_chunks // unroll
    s_ref[...] = jnp.zeros_like(s_ref)
    _interleave(solve(0))

    def body(t, carry):
        _interleave(solve(t), scan(t - 1))
        return carry

    lax.fori_loop(1, n_groups, body, 0)
    _interleave(scan(n_groups - 1))


def _mlp_kernel(x_ref, mla_ref, gdn_ref, woa_ref, wob_ref, nw_ref, wup_ref, wdn_ref, o_ref):
    h = (x_ref[...]
         + jnp.dot(mla_ref[...], woa_ref[...], preferred_element_type=F32)
         + jnp.dot(gdn_ref[...], wob_ref[...], preferred_element_type=F32))
    hn = _rms(h, nw_ref[...], D_MODEL).astype(BF16)
    o_ref[...] = h
    for c in range(D_FF // MLP_FC):
        u = jnp.dot(hn, wup_ref[:, c * MLP_FC:(c + 1) * MLP_FC], preferred_element_type=F32)
        a = jnp.square(jnp.maximum(u, 0.0)).astype(BF16)
        o_ref[...] += jnp.dot(a, wdn_ref[c * MLP_FC:(c + 1) * MLP_FC, :], preferred_element_type=F32)


def _const_spec(shape):
    nd = len(shape)
    return pl.BlockSpec(shape, lambda *_: (0,) * nd, pipeline_mode=pl.Buffered(1))


def _pad_lanes(v, width=LANES):
    v = v.reshape(1, -1).astype(F32)
    return jnp.pad(v, ((0, 0), (0, width - v.shape[1])))


def _layer(h, pos, inv_freq, p):
    B, S, _ = h.shape
    T = B * S
    x2 = h.reshape(T, D_MODEL)

    w_in = p["w_in"]
    o_q, o_kv, o_pe = 0, LORA, 2 * LORA
    o_g = o_pe + ROPE
    o_z = o_g + 3 * MIX
    o_a = o_z + MIX
    w_lat = w_in[:, o_q:o_pe].astype(BF16)
    w_kpe = jnp.pad(w_in[:, o_pe:o_g], ((0, 0), (0, LANES - ROPE))).astype(BF16)
    w_g = w_in[:, o_g:o_z].astype(BF16)
    w_z = w_in[:, o_z:o_a].astype(BF16)
    w_ab = jnp.pad(w_in[:, o_a:], ((0, 0), (0, LANES - 2 * N_HEADS))).astype(BF16)
    w_uq = p["w_uq"].reshape(LORA, N_HEADS, QK_HEAD)
    w_uq_nt = w_uq[:, :, :NOPE].reshape(LORA, N_HEADS * NOPE).T.astype(BF16)
    w_uq_pt = w_uq[:, :, NOPE:].reshape(LORA, N_HEADS * ROPE).T.astype(BF16)
    w_ukv = p["w_ukv"].reshape(LORA, N_HEADS, NOPE + HEAD)
    w_uk_k = w_ukv[:, :, :NOPE].reshape(LORA, N_HEADS * NOPE).astype(BF16)
    w_uk_vt = w_ukv[:, :, NOPE:].reshape(LORA, N_HEADS * HEAD).T.astype(BF16)

    tm = PREP_TM
    n_tiles = T // tm
    row_spec = lambda w: pl.BlockSpec((tm, w), lambda i: (i, 0))
    col_bcast = lambda v: jnp.broadcast_to(v.astype(F32)[:, None], (v.shape[0], tm))
    prep_in = [
        (x2, row_spec(D_MODEL)),
        (pos.reshape(n_tiles, 1, tm), pl.BlockSpec((None, 1, tm), lambda i: (i, 0, 0))),
        (p["attn_norm_w"].reshape(1, D_MODEL), None),
        (w_lat, None), (w_kpe, None), (w_ab, None), (w_g, None), (w_z, None),
        (p["q_lat_norm_w"].reshape(1, LORA), None),
        (p["kv_lat_norm_w"].reshape(1, LORA), None),
        (w_uq_nt, None), (w_uq_pt, None), (w_uk_k, None), (w_uk_vt, None),
        (col_bcast(p["q_norm_w"][:NOPE]), None),
        (col_bcast(p["q_norm_w"][NOPE:]), None),
        (p["k_norm_w"][:NOPE].reshape(1, NOPE), None),
        (_pad_lanes(p["k_norm_w"][NOPE:]), None),
        (col_bcast(inv_freq), None),
        (p["conv_w"], None),
        (_pad_lanes(p["a_log"]), None),
        (_pad_lanes(p["dt_bias"]), None),
    ]
    prep_args = [a for a, _ in prep_in]
    prep_specs = [s if s is not None else _const_spec(a.shape) for a, s in prep_in]
    out_shapes = [
        jax.ShapeDtypeStruct((n_tiles, 2 * MIX, tm), BF16),
        jax.ShapeDtypeStruct((T, 2 * MIX), BF16),
        jax.ShapeDtypeStruct((n_tiles, MIX, tm), BF16),
        jax.ShapeDtypeStruct((T, MIX), BF16),
        jax.ShapeDtypeStruct((T, MIX), BF16),
        jax.ShapeDtypeStruct((T, MIX), BF16),
        jax.ShapeDtypeStruct((T, MIX), BF16),
        jax.ShapeDtypeStruct((T, LANES), F32),
        jax.ShapeDtypeStruct((T // CHUNK, 8, CHUNK), F32),
    ]
    tile_spec = lambda r: pl.BlockSpec((None, r, tm), lambda i: (i, 0, 0))
    out_specs = [tile_spec(2 * MIX), row_spec(2 * MIX), tile_spec(MIX)] + [row_spec(MIX)] * 4 + [
        row_spec(LANES),
        pl.BlockSpec((tm // CHUNK, 8, CHUNK), lambda i: (i, 0, 0)),
    ]
    qt, k, vt, gq, gk, gv, gz, gb, rows = pl.pallas_call(
        functools.partial(_prep_kernel, tm=tm, tiles_per_seq=S // tm),
        grid=(n_tiles,),
        in_specs=prep_specs,
        out_specs=out_specs,
        out_shape=out_shapes,
        scratch_shapes=[pltpu.VMEM((8, 3 * MIX), F32)],
        compiler_params=pltpu.CompilerParams(dimension_semantics=("arbitrary",),
                                             vmem_limit_bytes=VMEM_LIMIT),
        name="prep",
    )(*prep_args)

    tq = tm
    nq = S // tq
    mla_o = pl.pallas_call(
        functools.partial(_attn_kernel, tq=tq),
        grid=(B, nq),
        in_specs=[
            pl.BlockSpec((None, 2 * MIX, tq), lambda b, i: (b * nq + i, 0, 0)),
            pl.BlockSpec((None, S, 2 * MIX), lambda b, i: (b, 0, 0)),
            pl.BlockSpec((None, nq, MIX, tq), lambda b, i: (b, 0, 0, 0)),
            _const_spec((N_HEADS, 1, HEAD)),
        ],
        out_specs=pl.BlockSpec((None, tq, MIX), lambda b, i: (b, i, 0)),
        out_shape=jax.ShapeDtypeStruct((B, S, MIX), BF16),
        scratch_shapes=[pltpu.VMEM((N_HEADS, 1, tq), F32), pltpu.VMEM((N_HEADS, 1, tq), F32),
                        pltpu.VMEM((N_HEADS, HEAD, tq), F32)],
        compiler_params=pltpu.CompilerParams(
            dimension_semantics=("arbitrary", "arbitrary"),
            vmem_limit_bytes=VMEM_LIMIT),
        name="attn",
    )(qt, k.reshape(B, S, 2 * MIX), vt.reshape(B, nq, MIX, tq),
      p["mla_out_norm_w"].reshape(N_HEADS, 1, HEAD))

    n_chunks = S // CHUNK
    seq_spec = lambda w: pl.BlockSpec((None, S, w), lambda b: (b, 0, 0))
    gdn_o = pl.pallas_call(
        functools.partial(_gdn_kernel, n_chunks=n_chunks, unroll=GDN_UNROLL),
        grid=(B,),
        in_specs=[seq_spec(MIX)] * 4 + [
            seq_spec(LANES),
            pl.BlockSpec((None, n_chunks, 8, CHUNK), lambda b: (b, 0, 0, 0)),
            _const_spec((1, HEAD)),
        ],
        out_specs=seq_spec(MIX),
        out_shape=jax.ShapeDtypeStruct((B, S, MIX), BF16),
        scratch_shapes=[
            pltpu.VMEM((N_HEADS, HEAD, HEAD), F32),
            pltpu.VMEM((n_chunks * N_HEADS, CHUNK, HEAD), F32),
            pltpu.VMEM((n_chunks * N_HEADS, 2 * CHUNK, HEAD), BF16),
            pltpu.VMEM((n_chunks * N_HEADS, CHUNK, HEAD), BF16),
            pltpu.VMEM((n_chunks * N_HEADS, CHUNK, CHUNK), BF16),
        ],
        compiler_params=pltpu.CompilerParams(dimension_semantics=("arbitrary",),
                                             vmem_limit_bytes=VMEM_LIMIT),
        name="gdn",
    )(gq.reshape(B, S, MIX), gk.reshape(B, S, MIX), gv.reshape(B, S, MIX), gz.reshape(B, S, MIX),
      gb.reshape(B, S, LANES), rows.reshape(B, n_chunks, 8, CHUNK),
      p["gdn_norm_w"].reshape(1, HEAD))

    tm2 = MLP_TM
    w_out = p["w_out"].astype(BF16)
    tok_spec = lambda w: pl.BlockSpec((tm2, w), lambda i: (i, 0))
    out = pl.pallas_call(
        _mlp_kernel,
        grid=(T // tm2,),
        in_specs=[
            tok_spec(D_MODEL), tok_spec(MIX), tok_spec(MIX),
            _const_spec((MIX, D_MODEL)), _const_spec((MIX, D_MODEL)),
            _const_spec((1, D_MODEL)),
            _const_spec((D_MODEL, D_FF)), _const_spec((D_FF, D_MODEL)),
        ],
        out_specs=tok_spec(D_MODEL),
        out_shape=jax.ShapeDtypeStruct((T, D_MODEL), F32),
        compiler_params=pltpu.CompilerParams(dimension_semantics=("arbitrary",),
                                             vmem_limit_bytes=VMEM_LIMIT),
        name="mlp",
    )(x2, mla_o.reshape(T, MIX), gdn_o.reshape(T, MIX), w_out[:MIX], w_out[MIX:],
      p["mlp_norm_w"].reshape(1, D_MODEL), p["w_up"].astype(BF16), p["w_down"].astype(BF16))
    return out.reshape(B, S, D_MODEL)


def kernel(x, positions, attn_norm_w, w_in, q_lat_norm_w, w_uq, kv_lat_norm_w, w_ukv, q_norm_w,
           k_norm_w, mla_out_norm_w, conv_w, a_log, dt_bias, gdn_norm_w, w_out, mlp_norm_w, w_up,
           w_down):
    B, S, _ = x.shape
    half = ROPE // 2
    inv_freq = ROPE_THETA ** (-jnp.arange(half, dtype=F32) / half)
    params = dict(attn_norm_w=attn_norm_w, w_in=w_in, q_lat_norm_w=q_lat_norm_w, w_uq=w_uq,
                  kv_lat_norm_w=kv_lat_norm_w, w_ukv=w_ukv, q_norm_w=q_norm_w, k_norm_w=k_norm_w,
                  mla_out_norm_w=mla_out_norm_w, conv_w=conv_w, a_log=a_log, dt_bias=dt_bias,
                  gdn_norm_w=gdn_norm_w, w_out=w_out, mlp_norm_w=mlp_norm_w, w_up=w_up,
                  w_down=w_down)
    h = x
    for l in range(attn_norm_w.shape[0]):
        h = _layer(h, positions, inv_freq, {name: val[l] for name, val in params.items()})
    return h
```

```python
import functools

import jax
import jax.numpy as jnp
from jax import lax
from jax.experimental import pallas as pl
from jax.experimental.pallas import tpu as pltpu

F32 = jnp.float32
BF16 = jnp.bfloat16

D_MODEL = 1024
N_HEADS = 4
LORA = 256
NOPE = 128
ROPE = 64
HEAD = 128
QK_HEAD = NOPE + ROPE
ROPE_THETA = 10000.0
CONV_W = 4
CHUNK = 64
D_FF = 4 * D_MODEL
EPS = 1e-6
LANES = 128
MIX = N_HEADS * HEAD

PREP_TM = 256
ATTN_SUB = 2
GDN_UNROLL = 4
MLP_TM = 512
MLP_FC = 1024
VMEM_LIMIT = 56 * 1024 * 1024


def _mm(a, b):
    return jnp.dot(a.astype(BF16), b.astype(BF16), preferred_element_type=F32)


def _mm_nt(a, b):
    return lax.dot_general(a.astype(BF16), b.astype(BF16), (((1,), (1,)), ((), ())),
                           preferred_element_type=F32)


def _mm_tn(a, b):
    return lax.dot_general(a.astype(BF16), b.astype(BF16), (((0,), (0,)), ((), ())),
                           preferred_element_type=F32)


def _split3(x):
    hi = x.astype(BF16)
    r1 = x - hi.astype(F32)
    mid = r1.astype(BF16)
    lo = (r1 - mid.astype(F32)).astype(BF16)
    return hi, mid, lo


def _interleave(*gens):
    live = list(gens)
    while live:
        for g in list(live):
            try:
                next(g)
            except StopIteration:
                live.remove(g)


def _rms(x, w, n):
    return x * lax.rsqrt(jnp.sum(x * x, axis=-1, keepdims=True) * (1.0 / n) + EPS) * w


def _sigmoid(x):
    return 1.0 / (1.0 + jnp.exp(-x))


def _rope(x, cosf, sinf, lane):
    rot = jnp.where(lane < ROPE // 2, pltpu.roll(x, LANES - ROPE // 2, 1),
                    pltpu.roll(x, ROPE // 2, 1))
    return x * cosf + rot * sinf


def _prep_kernel(x_ref, posr_ref, anw_ref, wlat_ref, wkpe_ref, wab_ref, wg_ref, wz_ref,
                 qlnw_ref, kvlnw_ref, wuqnt_ref, wuqpt_ref, wukk_ref, wukvt_ref,
                 qnwn_ref, qnwp_ref, knwn_ref, knwp_ref, invft_ref, convw_ref,
                 alog_ref, dtb_ref,
                 qt_out, k_out, vt_out, gq_out, gk_out, gv_out, z_out, gb_out, rows_out,
                 tail, *, tm, tiles_per_seq):
    i = pl.program_id(0)
    half = ROPE // 2
    x = x_ref[...]
    xb = _rms(x, anw_ref[...], D_MODEL).astype(BF16)

    lat = jnp.dot(xb, wlat_ref[...], preferred_element_type=F32)
    g_all = jnp.dot(xb, wg_ref[...], preferred_element_type=F32)
    k_pe = jnp.dot(xb, wkpe_ref[...], preferred_element_type=F32)
    ab = jnp.dot(xb, wab_ref[...], preferred_element_type=F32)
    z_out[...] = jnp.dot(xb, wz_ref[...], preferred_element_type=F32).astype(BF16)

    ang_t = invft_ref[...] * posr_ref[...].astype(F32)
    cos_t = jnp.cos(ang_t)
    sin_t = jnp.sin(ang_t)
    lane = lax.broadcasted_iota(jnp.int32, (tm, LANES), 1)
    table = jnp.concatenate([cos_t, sin_t, jnp.zeros((LANES - ROPE, tm), F32)], axis=0).T
    cosf = jnp.where(lane < half, table, jnp.where(lane < ROPE, pltpu.roll(table, half, 1), 0.0))
    sinf = jnp.where(lane < half, -pltpu.roll(table, LANES - half, 1),
                     jnp.where(lane < ROPE, table, 0.0))

    qn = _rms(lat[:, :LORA], qlnw_ref[...], LORA)
    kvn = _rms(lat[:, LORA:], kvlnw_ref[...], LORA)
    qn_t = qn.T.astype(BF16)
    kvn_t = kvn.T.astype(BF16)
    kvn = kvn.astype(BF16)
    qt_nope = jnp.dot(wuqnt_ref[...], qn_t, preferred_element_type=F32)
    qt_pe = jnp.dot(wuqpt_ref[...], qn_t, preferred_element_type=F32)
    vt_out[...] = jnp.dot(wukvt_ref[...], kvn_t, preferred_element_type=F32).astype(BF16)
    k_nope = jnp.dot(kvn, wukk_ref[...], preferred_element_type=F32)
    k_pe = _rope(_rms(k_pe, knwp_ref[...], ROPE), cosf, sinf, lane).astype(BF16)

    scale = QK_HEAD ** -0.5
    qwn = qnwn_ref[...] * scale
    qwp = qnwp_ref[...] * scale
    for h in range(N_HEADS):
        lo = h * HEAD
        base = h * 2 * HEAD
        xn = qt_nope[lo:lo + NOPE, :]
        xn = xn * lax.rsqrt(jnp.sum(xn * xn, axis=0, keepdims=True) * (1.0 / NOPE) + EPS) * qwn
        xp = qt_pe[h * ROPE:(h + 1) * ROPE, :]
        xp = xp * lax.rsqrt(jnp.sum(xp * xp, axis=0, keepdims=True) * (1.0 / ROPE) + EPS) * qwp
        t1, t2 = xp[:half], xp[half:]
        qt_out[base:base + NOPE, :] = xn.astype(BF16)
        qt_out[base + NOPE:base + NOPE + half, :] = (t1 * cos_t - t2 * sin_t).astype(BF16)
        qt_out[base + NOPE + half:base + QK_HEAD, :] = (t2 * cos_t + t1 * sin_t).astype(BF16)
        qt_out[base + QK_HEAD:base + 2 * HEAD, :] = jnp.zeros((2 * HEAD - QK_HEAD, tm), BF16)
        k_out[:, base:base + HEAD] = _rms(k_nope[:, lo:lo + HEAD], knwn_ref[...], NOPE).astype(BF16)
        k_out[:, base + HEAD:base + 2 * HEAD] = k_pe

    @pl.when(i % tiles_per_seq == 0)
    def _():
        tail[...] = jnp.zeros(tail.shape, F32)

    prev = tail[...]
    row8 = lax.broadcasted_iota(jnp.int32, prev.shape, 0)
    conv = convw_ref[CONV_W - 1:CONV_W, :] * g_all
    for s in range(1, CONV_W):
        rolled = pltpu.roll(g_all, s, 0)
        first = jnp.where(row8 < s, pltpu.roll(prev, s, 0), rolled[:8])
        shifted = jnp.concatenate([first, rolled[8:]], axis=0)
        conv = conv + convw_ref[CONV_W - 1 - s:CONV_W - s, :] * shifted
    tail[...] = g_all[tm - 8:tm, :]
    act = conv * _sigmoid(conv)
    for h in range(N_HEADS):
        lo = h * HEAD
        gq = act[:, lo:lo + HEAD]
        gk = act[:, MIX + lo:MIX + lo + HEAD]
        gq = gq * lax.rsqrt(jnp.sum(gq * gq, axis=-1, keepdims=True) + EPS) * (HEAD ** -0.5)
        gk = gk * lax.rsqrt(jnp.sum(gk * gk, axis=-1, keepdims=True) + EPS)
        gq_out[:, lo:lo + HEAD] = gq.astype(BF16)
        gk_out[:, lo:lo + HEAD] = gk.astype(BF16)
    gv_out[...] = act[:, 2 * MIX:].astype(BF16)

    sp_in = ab + dtb_ref[...]
    softplus = jnp.maximum(sp_in, 0.0) + jnp.log(1.0 + jnp.exp(-jnp.abs(sp_in)))
    g = -jnp.exp(alog_ref[...]) * softplus
    beta = _sigmoid(ab)
    rr = lax.broadcasted_iota(jnp.int32, (tm, tm), 0)
    cc = lax.broadcasted_iota(jnp.int32, (tm, tm), 1)
    tri = jnp.where((cc <= rr) & ((cc // CHUNK) == (rr // CHUNK)), 1.0, 0.0).astype(BF16)
    g_hi, g_mid, g_lo = _split3(g)
    gcum = (jnp.dot(tri, g_hi, preferred_element_type=F32)
            + jnp.dot(tri, g_mid, preferred_element_type=F32)
            + jnp.dot(tri, g_lo, preferred_element_type=F32))
    gb = jnp.where(lane < N_HEADS, gcum, jnp.where(lane < 2 * N_HEADS, beta, 0.0))
    gb_out[...] = gb
    er = lax.broadcasted_iota(jnp.int32, (8, LANES), 0)
    ec = lax.broadcasted_iota(jnp.int32, (8, LANES), 1)
    eye8 = jnp.where(er == ec, 1.0, 0.0).astype(BF16)
    for c in range(tm // CHUNK):
        parts = _split3(gb[c * CHUNK:(c + 1) * CHUNK, :])
        rows = None
        for p in parts:
            t = lax.dot_general(eye8, p, (((1,), (1,)), ((), ())), preferred_element_type=F32)
            rows = t if rows is None else rows + t
        rows_out[c] = rows


def _attn_kernel(qt_ref, k_ref, vt_ref, w_ref, o_ref, m_s, l_s, acc_s, *, tq, n_sub):
    i = pl.program_id(1)
    heads = range(N_HEADS)
    ones_rows = 16
    m_s[...] = jnp.full(m_s.shape, -jnp.inf, F32)
    l_s[...] = jnp.zeros(l_s.shape, F32)
    acc_s[...] = jnp.zeros(acc_s.shape, F32)
    kk = lax.broadcasted_iota(jnp.int32, (tq, tq), 0)
    qq = lax.broadcasted_iota(jnp.int32, (tq, tq), 1)
    ones = jnp.ones((ones_rows, tq), BF16)

    def block(j, items):
        r0 = pl.multiple_of(j * tq, tq)
        n_it = range(len(items))
        st = [jnp.dot(k_ref[pl.ds(r0, tq), 2 * HEAD * h:2 * HEAD * (h + 1)],
                      qt_ref[a, 2 * HEAD * h:2 * HEAD * (h + 1), :],
                      preferred_element_type=F32) for a, h, _ in items]
        st = [jnp.where(kk <= qq, st[x], -jnp.inf) if items[x][2] else st[x] for x in n_it]
        slot = [a * N_HEADS + h for a, h, _ in items]
        m_old = [m_s[s] for s in slot]
        m_new = [jnp.maximum(m_old[x], jnp.max(st[x], axis=0, keepdims=True)) for x in n_it]
        alpha = [jnp.exp(m_old[x] - m_new[x]) for x in n_it]
        p = [jnp.exp(st[x] - m_new[x]).astype(BF16) for x in n_it]
        pv = [jnp.dot(jnp.concatenate([vt_ref[j, HEAD * h:HEAD * (h + 1), :], ones], axis=0), p[x],
                      preferred_element_type=F32) for x, (a, h, _) in enumerate(items)]
        for x, s in enumerate(slot):
            m_s[s] = m_new[x]
            l_s[s] = alpha[x] * l_s[s] + pv[x][HEAD:HEAD + 1]
            acc_s[s] = alpha[x] * acc_s[s] + pv[x][:HEAD]

    def body(j, carry):
        block(j, [(a, h, False) for a in range(n_sub) for h in heads])
        return carry

    first = i * n_sub
    lax.fori_loop(0, first, body, 0)
    for d in range(n_sub):
        block(first + d, [(a, h, a == d) for a in range(d, n_sub) for h in heads])
    for a in range(n_sub):
        for h in heads:
            s = a * N_HEADS + h
            o = (acc_s[s] / l_s[s]).T
            o_ref[a * tq:(a + 1) * tq, HEAD * h:HEAD * (h + 1)] = _rms(o, w_ref[h], HEAD).astype(BF16)


def _gdn_kernel(q_ref, k_ref, v_ref, z_ref, gb_ref, rows_ref, gnw_ref, o_ref,
                s_ref, u_s, wq_s, kd_s, at_s, *, n_chunks, unroll):
    ii = lax.broadcasted_iota(jnp.int32, (CHUNK, CHUNK), 0)
    jj = lax.broadcasted_iota(jnp.int32, (CHUNK, CHUNK), 1)
    eye = jnp.where(ii == jj, 1.0, 0.0)
    heads = range(N_HEADS)

    def solve(t):
        ns = [t * unroll + c for c in range(unroll)]
        r0 = [pl.multiple_of(n * CHUNK, CHUNK) for n in ns]
        scal = [gb_ref[pl.ds(r, CHUNK), :] for r in r0]
        rows = [rows_ref[n] for n in ns]
        items = [(c, h) for c in range(unroll) for h in heads]
        cols = lambda h: slice(h * HEAD, (h + 1) * HEAD)
        q = [q_ref[pl.ds(r0[c], CHUNK), cols(h)] for c, h in items]
        k = [k_ref[pl.ds(r0[c], CHUNK), cols(h)] for c, h in items]
        v = [v_ref[pl.ds(r0[c], CHUNK), cols(h)] for c, h in items]
        g_col = [jnp.broadcast_to(scal[c][:, h:h + 1], (CHUNK, LANES)) for c, h in items]
        b_col = [jnp.broadcast_to(scal[c][:, N_HEADS + h:N_HEADS + h + 1], (CHUNK, CHUNK))
                 for c, h in items]
        g_row = [rows[c][h:h + 1, :] for c, h in items]
        b_row = [rows[c][N_HEADS + h:N_HEADS + h + 1, :] for c, h in items]
        n_it = range(len(items))
        decay = [jnp.exp(jnp.where(ii >= jj, g_col[x][:, :CHUNK] - g_row[x], -jnp.inf)) for x in n_it]
        qkk = [_mm_nt(jnp.concatenate([q[x], k[x]], axis=0), k[x]) for x in n_it]
        yield
        lmat = [jnp.where(ii > jj, b_col[x] * qkk[x][CHUNK:] * decay[x], 0.0) for x in n_it]
        inv = [eye - lmat[x] for x in n_it]
        pw = [_mm(lmat[x], lmat[x]) for x in n_it]
        yield
        for step in range(5):
            inv = [inv[x] + _mm(inv[x], pw[x]) for x in n_it]
            if step < 4:
                pw = [_mm(pw[x], pw[x]) for x in n_it]
            yield
        u = [_mm(inv[x] * b_row[x], v[x]) for x in n_it]
        w = [_mm(inv[x] * (b_row[x] * jnp.exp(g_row[x])), k[x]) for x in n_it]
        yield
        for x, (c, h) in enumerate(items):
            idx = ns[c] * N_HEADS + h
            g_last = g_col[x][CHUNK - 1:CHUNK, :]
            u_s[idx] = u[x]
            wq_s[idx, 0:CHUNK, :] = w[x].astype(BF16)
            wq_s[idx, CHUNK:2 * CHUNK, :] = (q[x].astype(F32) * jnp.exp(g_col[x])).astype(BF16)
            kd_s[idx] = (k[x].astype(F32) * jnp.exp(g_last - g_col[x])).astype(BF16)
            at_s[idx] = (qkk[x][:CHUNK] * decay[x]).astype(BF16)

    gnw = gnw_ref[...]

    def scan(t):
        for c in range(unroll):
            n = t * unroll + c
            r0 = pl.multiple_of(n * CHUNK, CHUNK)
            g_last = gb_ref[pl.ds(r0 + CHUNK - 1, 1), :]
            state = [s_ref[h] for h in heads]
            sb = [x.astype(BF16) for x in state]
            ws = [_mm(wq_s[n * N_HEADS + h], sb[h]) for h in heads]
            yield
            v_new = [(u_s[n * N_HEADS + h] - ws[h][:CHUNK]).astype(BF16) for h in heads]
            o = [ws[h][CHUNK:] + _mm(at_s[n * N_HEADS + h], v_new[h]) for h in heads]
            upd = [_mm_tn(kd_s[n * N_HEADS + h], v_new[h]) for h in heads]
            yield
            for h in heads:
                c_dec = jnp.exp(jnp.broadcast_to(g_last[:, h:h + 1], (HEAD, HEAD)))
                s_ref[h] = state[h] * c_dec + upd[h]
                zh = z_ref[pl.ds(r0, CHUNK), h * HEAD:(h + 1) * HEAD].astype(F32)
                gated = _rms(o[h], gnw, HEAD) * (zh * _sigmoid(zh))
                o_ref[pl.ds(r0, CHUNK), h * HEAD:(h + 1) * HEAD] = gated.astype(BF16)

    n_groups = n_chunks // unroll
    s_ref[...] = jnp.zeros_like(s_ref)
    _interleave(solve(0))

    def body(t, carry):
        _interleave(solve(t), scan(t - 1))
        return carry

    lax.fori_loop(1, n_groups, body, 0)
    _interleave(scan(n_groups - 1))


def _mlp_kernel(x_ref, mla_ref, gdn_ref, woa_ref, wob_ref, nw_ref, wup_ref, wdn_ref, o_ref):
    h = (x_ref[...]
         + jnp.dot(mla_ref[...], woa_ref[...], preferred_element_type=F32)
         + jnp.dot(gdn_ref[...], wob_ref[...], preferred_element_type=F32))
    hn = _rms(h, nw_ref[...], D_MODEL).astype(BF16)
    o_ref[...] = h
    for c in range(D_FF // MLP_FC):
        u = jnp.dot(hn, wup_ref[:, c * MLP_FC:(c + 1) * MLP_FC], preferred_element_type=F32)
        a = jnp.square(jnp.maximum(u, 0.0)).astype(BF16)
        o_ref[...] += jnp.dot(a, wdn_ref[c * MLP_FC:(c + 1) * MLP_FC, :], preferred_element_type=F32)


def _const_spec(shape):
    nd = len(shape)
    return pl.BlockSpec(shape, lambda *_: (0,) * nd, pipeline_mode=pl.Buffered(1))


def _pad_lanes(v, width=LANES):
    v = v.reshape(1, -1).astype(F32)
    return jnp.pad(v, ((0, 0), (0, width - v.shape[1])))


def _layer(h, pos, inv_freq, p):
    B, S, _ = h.shape
    T = B * S
    x2 = h.reshape(T, D_MODEL)

    w_in = p["w_in"]
    o_q, o_kv, o_pe = 0, LORA, 2 * LORA
    o_g = o_pe + ROPE
    o_z = o_g + 3 * MIX
    o_a = o_z + MIX
    w_lat = w_in[:, o_q:o_pe].astype(BF16)
    w_kpe = jnp.pad(w_in[:, o_pe:o_g], ((0, 0), (0, LANES - ROPE))).astype(BF16)
    w_g = w_in[:, o_g:o_z].astype(BF16)
    w_z = w_in[:, o_z:o_a].astype(BF16)
    w_ab = jnp.pad(w_in[:, o_a:], ((0, 0), (0, LANES - 2 * N_HEADS))).astype(BF16)
    w_uq = p["w_uq"].reshape(LORA, N_HEADS, QK_HEAD)
    w_uq_nt = w_uq[:, :, :NOPE].reshape(LORA, N_HEADS * NOPE).T.astype(BF16)
    w_uq_pt = w_uq[:, :, NOPE:].reshape(LORA, N_HEADS * ROPE).T.astype(BF16)
    w_ukv = p["w_ukv"].reshape(LORA, N_HEADS, NOPE + HEAD)
    w_uk_k = w_ukv[:, :, :NOPE].reshape(LORA, N_HEADS * NOPE).astype(BF16)
    w_uk_vt = w_ukv[:, :, NOPE:].reshape(LORA, N_HEADS * HEAD).T.astype(BF16)

    tm = PREP_TM
    n_tiles = T // tm
    row_spec = lambda w: pl.BlockSpec((tm, w), lambda i: (i, 0))
    col_bcast = lambda v: jnp.broadcast_to(v.astype(F32)[:, None], (v.shape[0], tm))
    prep_in = [
        (x2, row_spec(D_MODEL)),
        (pos.reshape(n_tiles, 1, tm), pl.BlockSpec((None, 1, tm), lambda i: (i, 0, 0))),
        (p["attn_norm_w"].reshape(1, D_MODEL), None),
        (w_lat, None), (w_kpe, None), (w_ab, None), (w_g, None), (w_z, None),
        (p["q_lat_norm_w"].reshape(1, LORA), None),
        (p["kv_lat_norm_w"].reshape(1, LORA), None),
        (w_uq_nt, None), (w_uq_pt, None), (w_uk_k, None), (w_uk_vt, None),
        (col_bcast(p["q_norm_w"][:NOPE]), None),
        (col_bcast(p["q_norm_w"][NOPE:]), None),
        (p["k_norm_w"][:NOPE].reshape(1, NOPE), None),
        (_pad_lanes(p["k_norm_w"][NOPE:]), None),
        (col_bcast(inv_freq), None),
        (p["conv_w"], None),
        (_pad_lanes(p["a_log"]), None),
        (_pad_lanes(p["dt_bias"]), None),
    ]
    prep_args = [a for a, _ in prep_in]
    prep_specs = [s if s is not None else _const_spec(a.shape) for a, s in prep_in]
    out_shapes = [
        jax.ShapeDtypeStruct((n_tiles, 2 * MIX, tm), BF16),
        jax.ShapeDtypeStruct((T, 2 * MIX), BF16),
        jax.ShapeDtypeStruct((n_tiles, MIX, tm), BF16),
        jax.ShapeDtypeStruct((T, MIX), BF16),
        jax.ShapeDtypeStruct((T, MIX), BF16),
        jax.ShapeDtypeStruct((T, MIX), BF16),
        jax.ShapeDtypeStruct((T, MIX), BF16),
        jax.ShapeDtypeStruct((T, LANES), F32),
        jax.ShapeDtypeStruct((T // CHUNK, 8, CHUNK), F32),
    ]
    tile_spec = lambda r: pl.BlockSpec((None, r, tm), lambda i: (i, 0, 0))
    out_specs = [tile_spec(2 * MIX), row_spec(2 * MIX), tile_spec(MIX)] + [row_spec(MIX)] * 4 + [
        row_spec(LANES),
        pl.BlockSpec((tm // CHUNK, 8, CHUNK), lambda i: (i, 0, 0)),
    ]
    qt, k, vt, gq, gk, gv, gz, gb, rows = pl.pallas_call(
        functools.partial(_prep_kernel, tm=tm, tiles_per_seq=S // tm),
        grid=(n_tiles,),
        in_specs=prep_specs,
        out_specs=out_specs,
        out_shape=out_shapes,
        scratch_shapes=[pltpu.VMEM((8, 3 * MIX), F32)],
        compiler_params=pltpu.CompilerParams(dimension_semantics=("arbitrary",),
                                             vmem_limit_bytes=VMEM_LIMIT),
        name="prep",
    )(*prep_args)

    tq = tm
    nq = S // tq
    n_sub = ATTN_SUB
    n_steps = nq // n_sub
    mla_o = pl.pallas_call(
        functools.partial(_attn_kernel, tq=tq, n_sub=n_sub),
        grid=(B, n_steps),
        in_specs=[
            pl.BlockSpec((n_sub, 2 * MIX, tq), lambda b, i: (b * n_steps + i, 0, 0)),
            pl.BlockSpec((None, S, 2 * MIX), lambda b, i: (b, 0, 0)),
            pl.BlockSpec((None, nq, MIX, tq), lambda b, i: (b, 0, 0, 0)),
            _const_spec((N_HEADS, 1, HEAD)),
        ],
        out_specs=pl.BlockSpec((None, n_sub * tq, MIX), lambda b, i: (b, i, 0)),
        out_shape=jax.ShapeDtypeStruct((B, S, MIX), BF16),
        scratch_shapes=[pltpu.VMEM((n_sub * N_HEADS, 1, tq), F32),
                        pltpu.VMEM((n_sub * N_HEADS, 1, tq), F32),
                        pltpu.VMEM((n_sub * N_HEADS, HEAD, tq), F32)],
        compiler_params=pltpu.CompilerParams(
            dimension_semantics=("arbitrary", "arbitrary"),
            vmem_limit_bytes=VMEM_LIMIT),
        name="attn",
    )(qt, k.reshape(B, S, 2 * MIX), vt.reshape(B, nq, MIX, tq),
      p["mla_out_norm_w"].reshape(N_HEADS, 1, HEAD))

    n_chunks = S // CHUNK
    seq_spec = lambda w: pl.BlockSpec((None, S, w), lambda b: (b, 0, 0))
    gdn_o = pl.pallas_call(
        functools.partial(_gdn_kernel, n_chunks=n_chunks, unroll=GDN_UNROLL),
        grid=(B,),
        in_specs=[seq_spec(MIX)] * 4 + [
            seq_spec(LANES),
            pl.BlockSpec((None, n_chunks, 8, CHUNK), lambda b: (b, 0, 0, 0)),
            _const_spec((1, HEAD)),
        ],
        out_specs=seq_spec(MIX),
        out_shape=jax.ShapeDtypeStruct((B, S, MIX), BF16),
        scratch_shapes=[
            pltpu.VMEM((N_HEADS, HEAD, HEAD), F32),
            pltpu.VMEM((n_chunks * N_HEADS, CHUNK, HEAD), F32),
            pltpu.VMEM((n_chunks * N_HEADS, 2 * CHUNK, HEAD), BF16),
            pltpu.VMEM((n_chunks * N_HEADS, CHUNK, HEAD), BF16),
            pltpu.VMEM((n_chunks * N_HEADS, CHUNK, CHUNK), BF16),
        ],
        compiler_params=pltpu.CompilerParams(dimension_semantics=("arbitrary",),
                                             vmem_limit_bytes=VMEM_LIMIT),
        name="gdn",
    )(gq.reshape(B, S, MIX), gk.reshape(B, S, MIX), gv.reshape(B, S, MIX), gz.reshape(B, S, MIX),
      gb.reshape(B, S, LANES), rows.reshape(B, n_chunks, 8, CHUNK),
      p["gdn_norm_w"].reshape(1, HEAD))

    tm2 = MLP_TM
    w_out = p["w_out"].astype(BF16)
    tok_spec = lambda w: pl.BlockSpec((tm2, w), lambda i: (i, 0))
    out = pl.pallas_call(
        _mlp_kernel,
        grid=(T // tm2,),
        in_specs=[
            tok_spec(D_MODEL), tok_spec(MIX), tok_spec(MIX),
            _const_spec((MIX, D_MODEL)), _const_spec((MIX, D_MODEL)),
            _const_spec((1, D_MODEL)),
            _const_spec((D_MODEL, D_FF)), _const_spec((D_FF, D_MODEL)),
        ],
        out_specs=tok_spec(D_MODEL),
        out_shape=jax.ShapeDtypeStruct((T, D_MODEL), F32),
        compiler_params=pltpu.CompilerParams(dimension_semantics=("arbitrary",),
                                             vmem_limit_bytes=VMEM_LIMIT),
        name="mlp",
    )(x2, mla_o.reshape(T, MIX), gdn_o.reshape(T, MIX), w_out[:MIX], w_out[MIX:],
      p["mlp_norm_w"].reshape(1, D_MODEL), p["w_up"].astype(BF16), p["w_down"].astype(BF16))
    return out.reshape(B, S, D_MODEL)


def kernel(x, positions, attn_norm_w, w_in, q_lat_norm_w, w_uq, kv_lat_norm_w, w_ukv, q_norm_w,
           k_norm_w, mla_out_norm_w, conv_w, a_log, dt_bias, gdn_norm_w, w_out, mlp_norm_w, w_up,
           w_down):
    B, S, _ = x.shape
    half = ROPE // 2
    inv_freq = ROPE_THETA ** (-jnp.arange(half, dtype=F32) / half)
    params = dict(attn_norm_w=attn_norm_w, w_in=w_in, q_lat_norm_w=q_lat_norm_w, w_uq=w_uq,
                  kv_lat_norm_w=kv_lat_norm_w, w_ukv=w_ukv, q_norm_w=q_norm_w, k_norm_w=k_norm_w,
                  mla_out_norm_w=mla_out_norm_w, conv_w=conv_w, a_log=a_log, dt_bias=dt_bias,
                  gdn_norm_w=gdn_norm_w, w_out=w_out, mlp_norm_w=mlp_norm_w, w_up=w_up,
                  w_down=w_down)
    h = x
    for l in range(attn_norm_w.shape[0]):
        h = _layer(h, positions, inv_freq, {name: val[l] for name, val in params.items()})
    return h
```

```python
import functools

import jax
import jax.numpy as jnp
from jax import lax
from jax.experimental import pallas as pl
from jax.experimental.pallas import tpu as pltpu

F32 = jnp.float32
BF16 = jnp.bfloat16

D_MODEL = 1024
N_HEADS = 4
LORA = 256
NOPE = 128
ROPE = 64
HEAD = 128
QK_HEAD = NOPE + ROPE
ROPE_THETA = 10000.0
CONV_W = 4
CHUNK = 64
D_FF = 4 * D_MODEL
EPS = 1e-6
LANES = 128
MIX = N_HEADS * HEAD

PREP_TM = 256
ATTN_SUB = 2
GDN_UNROLL = 4
MLP_TM = 512
MLP_FC = 1024
VMEM_LIMIT = 56 * 1024 * 1024


def _mm(a, b):
    return jnp.dot(a.astype(BF16), b.astype(BF16), preferred_element_type=F32)


def _mm_nt(a, b):
    return lax.dot_general(a.astype(BF16), b.astype(BF16), (((1,), (1,)), ((), ())),
                           preferred_element_type=F32)


def _mm_tn(a, b):
    return lax.dot_general(a.astype(BF16), b.astype(BF16), (((0,), (0,)), ((), ())),
                           preferred_element_type=F32)


def _split3(x):
    hi = x.astype(BF16)
    r1 = x - hi.astype(F32)
    mid = r1.astype(BF16)
    lo = (r1 - mid.astype(F32)).astype(BF16)
    return hi, mid, lo


def _interleave(*gens):
    live = list(gens)
    while live:
        for g in list(live):
            try:
                next(g)
            except StopIteration:
                live.remove(g)


def _rms(x, w, n):
    return x * lax.rsqrt(jnp.sum(x * x, axis=-1, keepdims=True) * (1.0 / n) + EPS) * w


def _sigmoid(x):
    return 1.0 / (1.0 + jnp.exp(-x))


def _rope(x, cosf, sinf, lane):
    rot = jnp.where(lane < ROPE // 2, pltpu.roll(x, LANES - ROPE // 2, 1),
                    pltpu.roll(x, ROPE // 2, 1))
    return x * cosf + rot * sinf


def _prep_kernel(x_ref, posr_ref, anw_ref, wlat_ref, wkpe_ref, wab_ref, wg_ref, wz_ref,
                 qlnw_ref, kvlnw_ref, wuqnt_ref, wuqpt_ref, wukk_ref, wukvt_ref,
                 qnwn_ref, qnwp_ref, knwn_ref, knwp_ref, invft_ref, convw_ref,
                 alog_ref, dtb_ref,
                 qt_out, k_out, vt_out, gq_out, gk_out, gv_out, z_out, gb_out, rows_out,
                 tail, *, tm, tiles_per_seq):
    i = pl.program_id(0)
    half = ROPE // 2
    x = x_ref[...]
    xb = _rms(x, anw_ref[...], D_MODEL).astype(BF16)

    lat = jnp.dot(xb, wlat_ref[...], preferred_element_type=F32)
    g_all = jnp.dot(xb, wg_ref[...], preferred_element_type=F32)
    k_pe = jnp.dot(xb, wkpe_ref[...], preferred_element_type=F32)
    ab = jnp.dot(xb, wab_ref[...], preferred_element_type=F32)
    z_out[...] = jnp.dot(xb, wz_ref[...], preferred_element_type=F32).astype(BF16)

    ang_t = invft_ref[...] * posr_ref[...].astype(F32)
    cos_t = jnp.cos(ang_t)
    sin_t = jnp.sin(ang_t)
    lane = lax.broadcasted_iota(jnp.int32, (tm, LANES), 1)
    table = jnp.concatenate([cos_t, sin_t, jnp.zeros((LANES - ROPE, tm), F32)], axis=0).T
    cosf = jnp.where(lane < half, table, jnp.where(lane < ROPE, pltpu.roll(table, half, 1), 0.0))
    sinf = jnp.where(lane < half, -pltpu.roll(table, LANES - half, 1),
                     jnp.where(lane < ROPE, table, 0.0))

    qn = _rms(lat[:, :LORA], qlnw_ref[...], LORA)
    kvn = _rms(lat[:, LORA:], kvlnw_ref[...], LORA)
    qn_t = qn.T.astype(BF16)
    kvn_t = kvn.T.astype(BF16)
    kvn = kvn.astype(BF16)
    qt_nope = jnp.dot(wuqnt_ref[...], qn_t, preferred_element_type=F32)
    qt_pe = jnp.dot(wuqpt_ref[...], qn_t, preferred_element_type=F32)
    vt_out[...] = jnp.dot(wukvt_ref[...], kvn_t, preferred_element_type=F32).astype(BF16)
    k_nope = jnp.dot(kvn, wukk_ref[...], preferred_element_type=F32)
    k_pe = _rope(_rms(k_pe, knwp_ref[...], ROPE), cosf, sinf, lane).astype(BF16)

    scale = QK_HEAD ** -0.5
    qwn = qnwn_ref[...] * scale
    qwp = qnwp_ref[...] * scale
    for h in range(N_HEADS):
        lo = h * HEAD
        base = h * 2 * HEAD
        xn = qt_nope[lo:lo + NOPE, :]
        xn = xn * lax.rsqrt(jnp.sum(xn * xn, axis=0, keepdims=True) * (1.0 / NOPE) + EPS) * qwn
        xp = qt_pe[h * ROPE:(h + 1) * ROPE, :]
        xp = xp * lax.rsqrt(jnp.sum(xp * xp, axis=0, keepdims=True) * (1.0 / ROPE) + EPS) * qwp
        t1, t2 = xp[:half], xp[half:]
        qt_out[base:base + NOPE, :] = xn.astype(BF16)
        qt_out[base + NOPE:base + NOPE + half, :] = (t1 * cos_t - t2 * sin_t).astype(BF16)
        qt_out[base + NOPE + half:base + QK_HEAD, :] = (t2 * cos_t + t1 * sin_t).astype(BF16)
        qt_out[base + QK_HEAD:base + 2 * HEAD, :] = jnp.zeros((2 * HEAD - QK_HEAD, tm), BF16)
        k_out[:, base:base + HEAD] = _rms(k_nope[:, lo:lo + HEAD], knwn_ref[...], NOPE).astype(BF16)
        k_out[:, base + HEAD:base + 2 * HEAD] = k_pe

    @pl.when(i % tiles_per_seq == 0)
    def _():
        tail[...] = jnp.zeros(tail.shape, F32)

    prev = tail[...]
    row8 = lax.broadcasted_iota(jnp.int32, prev.shape, 0)
    conv = convw_ref[CONV_W - 1:CONV_W, :] * g_all
    for s in range(1, CONV_W):
        rolled = pltpu.roll(g_all, s, 0)
        first = jnp.where(row8 < s, pltpu.roll(prev, s, 0), rolled[:8])
        shifted = jnp.concatenate([first, rolled[8:]], axis=0)
        conv = conv + convw_ref[CONV_W - 1 - s:CONV_W - s, :] * shifted
    tail[...] = g_all[tm - 8:tm, :]
    act = conv * _sigmoid(conv)
    for h in range(N_HEADS):
        lo = h * HEAD
        gq = act[:, lo:lo + HEAD]
        gk = act[:, MIX + lo:MIX + lo + HEAD]
        gq = gq * lax.rsqrt(jnp.sum(gq * gq, axis=-1, keepdims=True) + EPS) * (HEAD ** -0.5)
        gk = gk * lax.rsqrt(jnp.sum(gk * gk, axis=-1, keepdims=True) + EPS)
        gq_out[:, lo:lo + HEAD] = gq.astype(BF16)
        gk_out[:, lo:lo + HEAD] = gk.astype(BF16)
    gv_out[...] = act[:, 2 * MIX:].astype(BF16)

    sp_in = ab + dtb_ref[...]
    softplus = jnp.maximum(sp_in, 0.0) + jnp.log(1.0 + jnp.exp(-jnp.abs(sp_in)))
    g = -jnp.exp(alog_ref[...]) * softplus
    beta = _sigmoid(ab)
    rr = lax.broadcasted_iota(jnp.int32, (tm, tm), 0)
    cc = lax.broadcasted_iota(jnp.int32, (tm, tm), 1)
    tri = jnp.where((cc <= rr) & ((cc // CHUNK) == (rr // CHUNK)), 1.0, 0.0).astype(BF16)
    g_hi, g_mid, g_lo = _split3(g)
    gcum = (jnp.dot(tri, g_hi, preferred_element_type=F32)
            + jnp.dot(tri, g_mid, preferred_element_type=F32)
            + jnp.dot(tri, g_lo, preferred_element_type=F32))
    gb = jnp.where(lane < N_HEADS, gcum, jnp.where(lane < 2 * N_HEADS, beta, 0.0))
    gb_out[...] = gb
    er = lax.broadcasted_iota(jnp.int32, (8, LANES), 0)
    ec = lax.broadcasted_iota(jnp.int32, (8, LANES), 1)
    eye8 = jnp.where(er == ec, 1.0, 0.0).astype(BF16)
    for c in range(tm // CHUNK):
        parts = _split3(gb[c * CHUNK:(c + 1) * CHUNK, :])
        rows = None
        for p in parts:
            t = lax.dot_general(eye8, p, (((1,), (1,)), ((), ())), preferred_element_type=F32)
            rows = t if rows is None else rows + t
        rows_out[c] = rows


def _attn_kernel(qt_ref, k_ref, vt_ref, w_ref, o_ref, m_s, l_s, acc_s, st_s, *, tq, n_sub):
    i = pl.program_id(1)
    heads = range(N_HEADS)
    ones_rows = 16
    m_s[...] = jnp.full(m_s.shape, -jnp.inf, F32)
    l_s[...] = jnp.zeros(l_s.shape, F32)
    acc_s[...] = jnp.zeros(acc_s.shape, F32)
    kk = lax.broadcasted_iota(jnp.int32, (tq, tq), 0)
    qq = lax.broadcasted_iota(jnp.int32, (tq, tq), 1)
    ones = jnp.ones((ones_rows, tq), BF16)

    def scores(j, items, buf):
        r0 = pl.multiple_of(j * tq, tq)
        for a, h, _ in items:
            st_s[buf, a * N_HEADS + h] = jnp.dot(
                k_ref[pl.ds(r0, tq), 2 * HEAD * h:2 * HEAD * (h + 1)],
                qt_ref[a, 2 * HEAD * h:2 * HEAD * (h + 1), :],
                preferred_element_type=F32)

    def update(j, items, buf):
        n_it = range(len(items))
        slot = [a * N_HEADS + h for a, h, _ in items]
        st = [st_s[buf, s] for s in slot]
        st = [jnp.where(kk <= qq, st[x], -jnp.inf) if items[x][2] else st[x] for x in n_it]
        m_old = [m_s[s] for s in slot]
        m_new = [jnp.maximum(m_old[x], jnp.max(st[x], axis=0, keepdims=True)) for x in n_it]
        alpha = [jnp.exp(m_old[x] - m_new[x]) for x in n_it]
        p = [jnp.exp(st[x] - m_new[x]).astype(BF16) for x in n_it]
        pv = [jnp.dot(jnp.concatenate([vt_ref[j, HEAD * h:HEAD * (h + 1), :], ones], axis=0), p[x],
                      preferred_element_type=F32) for x, (a, h, _) in enumerate(items)]
        for x, s in enumerate(slot):
            m_s[s] = m_new[x]
            l_s[s] = alpha[x] * l_s[s] + pv[x][HEAD:HEAD + 1]
            acc_s[s] = alpha[x] * acc_s[s] + pv[x][:HEAD]

    full = [(a, h, False) for a in range(n_sub) for h in heads]
    diag = lambda d: [(a, h, a == d) for a in range(d, n_sub) for h in heads]
    first = i * n_sub

    def pair(t, carry):
        scores(2 * t + 1, full, 1)
        update(2 * t, full, 0)
        scores(2 * t + 2, full, 0)
        update(2 * t + 1, full, 1)
        return carry

    scores(0, full, 0)
    lax.fori_loop(0, first // 2, pair, 0)
    for d in range(n_sub):
        if d + 1 < n_sub:
            scores(first + d + 1, diag(d + 1), (d + 1) % 2)
        update(first + d, diag(d), d % 2)
    for a in range(n_sub):
        for h in heads:
            s = a * N_HEADS + h
            o = (acc_s[s] / l_s[s]).T
            o_ref[a * tq:(a + 1) * tq, HEAD * h:HEAD * (h + 1)] = _rms(o, w_ref[h], HEAD).astype(BF16)


def _gdn_kernel(q_ref, k_ref, v_ref, z_ref, gb_ref, rows_ref, gnw_ref, o_ref,
                s_ref, u_s, wq_s, kd_s, at_s, *, n_chunks, unroll):
    ii = lax.broadcasted_iota(jnp.int32, (CHUNK, CHUNK), 0)
    jj = lax.broadcasted_iota(jnp.int32, (CHUNK, CHUNK), 1)
    eye = jnp.where(ii == jj, 1.0, 0.0)
    heads = range(N_HEADS)

    def solve(t):
        ns = [t * unroll + c for c in range(unroll)]
        r0 = [pl.multiple_of(n * CHUNK, CHUNK) for n in ns]
        scal = [gb_ref[pl.ds(r, CHUNK), :] for r in r0]
        rows = [rows_ref[n] for n in ns]
        items = [(c, h) for c in range(unroll) for h in heads]
        cols = lambda h: slice(h * HEAD, (h + 1) * HEAD)
        q = [q_ref[pl.ds(r0[c], CHUNK), cols(h)] for c, h in items]
        k = [k_ref[pl.ds(r0[c], CHUNK), cols(h)] for c, h in items]
        v = [v_ref[pl.ds(r0[c], CHUNK), cols(h)] for c, h in items]
        g_col = [jnp.broadcast_to(scal[c][:, h:h + 1], (CHUNK, LANES)) for c, h in items]
        b_col = [jnp.broadcast_to(scal[c][:, N_HEADS + h:N_HEADS + h + 1], (CHUNK, CHUNK))
                 for c, h in items]
        g_row = [rows[c][h:h + 1, :] for c, h in items]
        b_row = [rows[c][N_HEADS + h:N_HEADS + h + 1, :] for c, h in items]
        n_it = range(len(items))
        decay = [jnp.exp(jnp.where(ii >= jj, g_col[x][:, :CHUNK] - g_row[x], -jnp.inf)) for x in n_it]
        qkk = [_mm_nt(jnp.concatenate([q[x], k[x]], axis=0), k[x]) for x in n_it]
        yield
        lmat = [jnp.where(ii > jj, b_col[x] * qkk[x][CHUNK:] * decay[x], 0.0) for x in n_it]
        inv = [eye - lmat[x] for x in n_it]
        pw = [_mm(lmat[x], lmat[x]) for x in n_it]
        yield
        for step in range(5):
            inv = [inv[x] + _mm(inv[x], pw[x]) for x in n_it]
            if step < 4:
                pw = [_mm(pw[x], pw[x]) for x in n_it]
            yield
        u = [_mm(inv[x] * b_row[x], v[x]) for x in n_it]
        w = [_mm(inv[x] * (b_row[x] * jnp.exp(g_row[x])), k[x]) for x in n_it]
        yield
        for x, (c, h) in enumerate(items):
            idx = ns[c] * N_HEADS + h
            g_last = g_col[x][CHUNK - 1:CHUNK, :]
            u_s[idx] = u[x]
            wq_s[idx, 0:CHUNK, :] = w[x].astype(BF16)
            wq_s[idx, CHUNK:2 * CHUNK, :] = (q[x].astype(F32) * jnp.exp(g_col[x])).astype(BF16)
            kd_s[idx] = (k[x].astype(F32) * jnp.exp(g_last - g_col[x])).astype(BF16)
            at_s[idx] = (qkk[x][:CHUNK] * decay[x]).astype(BF16)

    gnw = gnw_ref[...]

    def scan(t):
        for c in range(unroll):
            n = t * unroll + c
            r0 = pl.multiple_of(n * CHUNK, CHUNK)
            g_last = gb_ref[pl.ds(r0 + CHUNK - 1, 1), :]
            state = [s_ref[h] for h in heads]
            sb = [x.astype(BF16) for x in state]
            ws = [_mm(wq_s[n * N_HEADS + h], sb[h]) for h in heads]
            yield
            v_new = [(u_s[n * N_HEADS + h] - ws[h][:CHUNK]).astype(BF16) for h in heads]
            o = [ws[h][CHUNK:] + _mm(at_s[n * N_HEADS + h], v_new[h]) for h in heads]
            upd = [_mm_tn(kd_s[n * N_HEADS + h], v_new[h]) for h in heads]
            yield
            for h in heads:
                c_dec = jnp.exp(jnp.broadcast_to(g_last[:, h:h + 1], (HEAD, HEAD)))
                s_ref[h] = state[h] * c_dec + upd[h]
                zh = z_ref[pl.ds(r0, CHUNK), h * HEAD:(h + 1) * HEAD].astype(F32)
                gated = _rms(o[h], gnw, HEAD) * (zh * _sigmoid(zh))
                o_ref[pl.ds(r0, CHUNK), h * HEAD:(h + 1) * HEAD] = gated.astype(BF16)

    n_groups = n_chunks // unroll
    s_ref[...] = jnp.zeros_like(s_ref)
    _interleave(solve(0))

    def body(t, carry):
        _interleave(solve(t), scan(t - 1))
        return carry

    lax.fori_loop(1, n_groups, body, 0)
    _interleave(scan(n_groups - 1))


def _mlp_kernel(x_ref, mla_ref, gdn_ref, woa_ref, wob_ref, nw_ref, wup_ref, wdn_ref, o_ref):
    h = (x_ref[...]
         + jnp.dot(mla_ref[...], woa_ref[...], preferred_element_type=F32)
         + jnp.dot(gdn_ref[...], wob_ref[...], preferred_element_type=F32))
    hn = _rms(h, nw_ref[...], D_MODEL).astype(BF16)
    o_ref[...] = h
    for c in range(D_FF // MLP_FC):
        u = jnp.dot(hn, wup_ref[:, c * MLP_FC:(c + 1) * MLP_FC], preferred_element_type=F32)
        a = jnp.square(jnp.maximum(u, 0.0)).astype(BF16)
        o_ref[...] += jnp.dot(a, wdn_ref[c * MLP_FC:(c + 1) * MLP_FC, :], preferred_element_type=F32)


def _const_spec(shape):
    nd = len(shape)
    return pl.BlockSpec(shape, lambda *_: (0,) * nd, pipeline_mode=pl.Buffered(1))


def _pad_lanes(v, width=LANES):
    v = v.reshape(1, -1).astype(F32)
    return jnp.pad(v, ((0, 0), (0, width - v.shape[1])))


def _layer(h, pos, inv_freq, p):
    B, S, _ = h.shape
    T = B * S
    x2 = h.reshape(T, D_MODEL)

    w_in = p["w_in"]
    o_q, o_kv, o_pe = 0, LORA, 2 * LORA
    o_g = o_pe + ROPE
    o_z = o_g + 3 * MIX
    o_a = o_z + MIX
    w_lat = w_in[:, o_q:o_pe].astype(BF16)
    w_kpe = jnp.pad(w_in[:, o_pe:o_g], ((0, 0), (0, LANES - ROPE))).astype(BF16)
    w_g = w_in[:, o_g:o_z].astype(BF16)
    w_z = w_in[:, o_z:o_a].astype(BF16)
    w_ab = jnp.pad(w_in[:, o_a:], ((0, 0), (0, LANES - 2 * N_HEADS))).astype(BF16)
    w_uq = p["w_uq"].reshape(LORA, N_HEADS, QK_HEAD)
    w_uq_nt = w_uq[:, :, :NOPE].reshape(LORA, N_HEADS * NOPE).T.astype(BF16)
    w_uq_pt = w_uq[:, :, NOPE:].reshape(LORA, N_HEADS * ROPE).T.astype(BF16)
    w_ukv = p["w_ukv"].reshape(LORA, N_HEADS, NOPE + HEAD)
    w_uk_k = w_ukv[:, :, :NOPE].reshape(LORA, N_HEADS * NOPE).astype(BF16)
    w_uk_vt = w_ukv[:, :, NOPE:].reshape(LORA, N_HEADS * HEAD).T.astype(BF16)

    tm = PREP_TM
    n_tiles = T // tm
    row_spec = lambda w: pl.BlockSpec((tm, w), lambda i: (i, 0))
    col_bcast = lambda v: jnp.broadcast_to(v.astype(F32)[:, None], (v.shape[0], tm))
    prep_in = [
        (x2, row_spec(D_MODEL)),
        (pos.reshape(n_tiles, 1, tm), pl.BlockSpec((None, 1, tm), lambda i: (i, 0, 0))),
        (p["attn_norm_w"].reshape(1, D_MODEL), None),
        (w_lat, None), (w_kpe, None), (w_ab, None), (w_g, None), (w_z, None),
        (p["q_lat_norm_w"].reshape(1, LORA), None),
        (p["kv_lat_norm_w"].reshape(1, LORA), None),
        (w_uq_nt, None), (w_uq_pt, None), (w_uk_k, None), (w_uk_vt, None),
        (col_bcast(p["q_norm_w"][:NOPE]), None),
        (col_bcast(p["q_norm_w"][NOPE:]), None),
        (p["k_norm_w"][:NOPE].reshape(1, NOPE), None),
        (_pad_lanes(p["k_norm_w"][NOPE:]), None),
        (col_bcast(inv_freq), None),
        (p["conv_w"], None),
        (_pad_lanes(p["a_log"]), None),
        (_pad_lanes(p["dt_bias"]), None),
    ]
    prep_args = [a for a, _ in prep_in]
    prep_specs = [s if s is not None else _const_spec(a.shape) for a, s in prep_in]
    out_shapes = [
        jax.ShapeDtypeStruct((n_tiles, 2 * MIX, tm), BF16),
        jax.ShapeDtypeStruct((T, 2 * MIX), BF16),
        jax.ShapeDtypeStruct((n_tiles, MIX, tm), BF16),
        jax.ShapeDtypeStruct((T, MIX), BF16),
        jax.ShapeDtypeStruct((T, MIX), BF16),
        jax.ShapeDtypeStruct((T, MIX), BF16),
        jax.ShapeDtypeStruct((T, MIX), BF16),
        jax.ShapeDtypeStruct((T, LANES), F32),
        jax.ShapeDtypeStruct((T // CHUNK, 8, CHUNK), F32),
    ]
    tile_spec = lambda r: pl.BlockSpec((None, r, tm), lambda i: (i, 0, 0))
    out_specs = [tile_spec(2 * MIX), row_spec(2 * MIX), tile_spec(MIX)] + [row_spec(MIX)] * 4 + [
        row_spec(LANES),
        pl.BlockSpec((tm // CHUNK, 8, CHUNK), lambda i: (i, 0, 0)),
    ]
    qt, k, vt, gq, gk, gv, gz, gb, rows = pl.pallas_call(
        functools.partial(_prep_kernel, tm=tm, tiles_per_seq=S // tm),
        grid=(n_tiles,),
        in_specs=prep_specs,
        out_specs=out_specs,
        out_shape=out_shapes,
        scratch_shapes=[pltpu.VMEM((8, 3 * MIX), F32)],
        compiler_params=pltpu.CompilerParams(dimension_semantics=("arbitrary",),
                                             vmem_limit_bytes=VMEM_LIMIT),
        name="prep",
    )(*prep_args)

    tq = tm
    nq = S // tq
    n_sub = ATTN_SUB
    assert n_sub % 2 == 0 and nq % n_sub == 0
    n_steps = nq // n_sub
    mla_o = pl.pallas_call(
        functools.partial(_attn_kernel, tq=tq, n_sub=n_sub),
        grid=(B, n_steps),
        in_specs=[
            pl.BlockSpec((n_sub, 2 * MIX, tq), lambda b, i: (b * n_steps + i, 0, 0)),
            pl.BlockSpec((None, S, 2 * MIX), lambda b, i: (b, 0, 0)),
            pl.BlockSpec((None, nq, MIX, tq), lambda b, i: (b, 0, 0, 0)),
            _const_spec((N_HEADS, 1, HEAD)),
        ],
        out_specs=pl.BlockSpec((None, n_sub * tq, MIX), lambda b, i: (b, i, 0)),
        out_shape=jax.ShapeDtypeStruct((B, S, MIX), BF16),
        scratch_shapes=[pltpu.VMEM((n_sub * N_HEADS, 1, tq), F32),
                        pltpu.VMEM((n_sub * N_HEADS, 1, tq), F32),
                        pltpu.VMEM((n_sub * N_HEADS, HEAD, tq), F32),
                        pltpu.VMEM((2, n_sub * N_HEADS, tq, tq), F32)],
        compiler_params=pltpu.CompilerParams(
            dimension_semantics=("arbitrary", "arbitrary"),
            vmem_limit_bytes=VMEM_LIMIT),
        name="attn",
    )(qt, k.reshape(B, S, 2 * MIX), vt.reshape(B, nq, MIX, tq),
      p["mla_out_norm_w"].reshape(N_HEADS, 1, HEAD))

    n_chunks = S // CHUNK
    seq_spec = lambda w: pl.BlockSpec((None, S, w), lambda b: (b, 0, 0))
    gdn_o = pl.pallas_call(
        functools.partial(_gdn_kernel, n_chunks=n_chunks, unroll=GDN_UNROLL),
        grid=(B,),
        in_specs=[seq_spec(MIX)] * 4 + [
            seq_spec(LANES),
            pl.BlockSpec((None, n_chunks, 8, CHUNK), lambda b: (b, 0, 0, 0)),
            _const_spec((1, HEAD)),
        ],
        out_specs=seq_spec(MIX),
        out_shape=jax.ShapeDtypeStruct((B, S, MIX), BF16),
        scratch_shapes=[
            pltpu.VMEM((N_HEADS, HEAD, HEAD), F32),
            pltpu.VMEM((n_chunks * N_HEADS, CHUNK, HEAD), F32),
            pltpu.VMEM((n_chunks * N_HEADS, 2 * CHUNK, HEAD), BF16),
            pltpu.VMEM((n_chunks * N_HEADS, CHUNK, HEAD), BF16),
            pltpu.VMEM((n_chunks * N_HEADS, CHUNK, CHUNK), BF16),
        ],
        compiler_params=pltpu.CompilerParams(dimension_semantics=("arbitrary",),
                                             vmem_limit_bytes=VMEM_LIMIT),
        name="gdn",
    )(gq.reshape(B, S, MIX), gk.reshape(B, S, MIX), gv.reshape(B, S, MIX), gz.reshape(B, S, MIX),
      gb.reshape(B, S, LANES), rows.reshape(B, n_chunks, 8, CHUNK),
      p["gdn_norm_w"].reshape(1, HEAD))

    tm2 = MLP_TM
    w_out = p["w_out"].astype(BF16)
    tok_spec = lambda w: pl.BlockSpec((tm2, w), lambda i: (i, 0))
    out = pl.pallas_call(
        _mlp_kernel,
        grid=(T // tm2,),
        in_specs=[
            tok_spec(D_MODEL), tok_spec(MIX), tok_spec(MIX),
            _const_spec((MIX, D_MODEL)), _const_spec((MIX, D_MODEL)),
            _const_spec((1, D_MODEL)),
            _const_spec((D_MODEL, D_FF)), _const_spec((D_FF, D_MODEL)),
        ],
        out_specs=tok_spec(D_MODEL),
        out_shape=jax.ShapeDtypeStruct((T, D_MODEL), F32),
        compiler_params=pltpu.CompilerParams(dimension_semantics=("arbitrary",),
                                             vmem_limit_bytes=VMEM_LIMIT),
        name="mlp",
    )(x2, mla_o.reshape(T, MIX), gdn_o.reshape(T, MIX), w_out[:MIX], w_out[MIX:],
      p["mlp_norm_w"].reshape(1, D_MODEL), p["w_up"].astype(BF16), p["w_down"].astype(BF16))
    return out.reshape(B, S, D_MODEL)


def kernel(x, positions, attn_norm_w, w_in, q_lat_norm_w, w_uq, kv_lat_norm_w, w_ukv, q_norm_w,
           k_norm_w, mla_out_norm_w, conv_w, a_log, dt_bias, gdn_norm_w, w_out, mlp_norm_w, w_up,
           w_down):
    B, S, _ = x.shape
    half = ROPE // 2
    inv_freq = ROPE_THETA ** (-jnp.arange(half, dtype=F32) / half)
    params = dict(attn_norm_w=attn_norm_w, w_in=w_in, q_lat_norm_w=q_lat_norm_w, w_uq=w_uq,
                  kv_lat_norm_w=kv_lat_norm_w, w_ukv=w_ukv, q_norm_w=q_norm_w, k_norm_w=k_norm_w,
                  mla_out_norm_w=mla_out_norm_w, conv_w=conv_w, a_log=a_log, dt_bias=dt_bias,
                  gdn_norm_w=gdn_norm_w, w_out=w_out, mlp_norm_w=mlp_norm_w, w_up=w_up,
                  w_down=w_down)
    h = x
    for l in range(attn_norm_w.shape[0]):
        h = _layer(h, positions, inv_freq, {name: val[l] for name, val in params.items()})
    return h
```

```python
import functools

import jax
import jax.numpy as jnp
from jax import lax
from jax.experimental import pallas as pl
from jax.experimental.pallas import tpu as pltpu

F32 = jnp.float32
BF16 = jnp.bfloat16

D_MODEL = 1024
N_HEADS = 4
LORA = 256
NOPE = 128
ROPE = 64
HEAD = 128
QK_HEAD = NOPE + ROPE
ROPE_THETA = 10000.0
CONV_W = 4
CHUNK = 64
D_FF = 4 * D_MODEL
EPS = 1e-6
LANES = 128
MIX = N_HEADS * HEAD

PREP_TM = 256
ATTN_SUB = 2
GDN_UNROLL = 4
MLP_TM = 512
MLP_FC = 1024
VMEM_LIMIT = 56 * 1024 * 1024


def _mm(a, b):
    return jnp.dot(a.astype(BF16), b.astype(BF16), preferred_element_type=F32)


def _mm_nt(a, b):
    return lax.dot_general(a.astype(BF16), b.astype(BF16), (((1,), (1,)), ((), ())),
                           preferred_element_type=F32)


def _mm_tn(a, b):
    return lax.dot_general(a.astype(BF16), b.astype(BF16), (((0,), (0,)), ((), ())),
                           preferred_element_type=F32)


def _split3(x):
    hi = x.astype(BF16)
    r1 = x - hi.astype(F32)
    mid = r1.astype(BF16)
    lo = (r1 - mid.astype(F32)).astype(BF16)
    return hi, mid, lo


def _interleave(*gens):
    live = list(gens)
    while live:
        for g in list(live):
            try:
                next(g)
            except StopIteration:
                live.remove(g)


def _rms(x, w, n):
    return x * lax.rsqrt(jnp.sum(x * x, axis=-1, keepdims=True) * (1.0 / n) + EPS) * w


def _sigmoid(x):
    return 1.0 / (1.0 + jnp.exp(-x))


def _rope(x, cosf, sinf, lane):
    rot = jnp.where(lane < ROPE // 2, pltpu.roll(x, LANES - ROPE // 2, 1),
                    pltpu.roll(x, ROPE // 2, 1))
    return x * cosf + rot * sinf


def _prep_kernel(x0_ref, xa_ref, xb_ref, posr_ref, anw_ref, wlat_ref, wkpe_ref, wab_ref, wg_ref, wz_ref,
                 qlnw_ref, kvlnw_ref, wuqnt_ref, wuqpt_ref, wukk_ref, wukvt_ref,
                 qnwn_ref, qnwp_ref, knwn_ref, knwp_ref, invft_ref, convw_ref,
                 alog_ref, dtb_ref,
                 qt_out, k_out, vt_out, gq_out, gk_out, gv_out, z_out, gb_out, rows_out,
                 lat_s, g_s, kpe_s, ab_s, z_s, tail, *, tm, tiles_per_seq):
    i = pl.program_id(0)
    half = ROPE // 2

    def front(x_ref, slot):
        xn = _rms(x_ref[...], anw_ref[...], D_MODEL).astype(BF16)
        yield
        lat_s[slot] = jnp.dot(xn, wlat_ref[...], preferred_element_type=F32)
        yield
        g_s[slot] = jnp.dot(xn, wg_ref[...], preferred_element_type=F32)
        yield
        kpe_s[slot] = jnp.dot(xn, wkpe_ref[...], preferred_element_type=F32)
        ab_s[slot] = jnp.dot(xn, wab_ref[...], preferred_element_type=F32)
        z_s[slot] = jnp.dot(xn, wz_ref[...], preferred_element_type=F32).astype(BF16)

    def back(slot, a):
        rows = slice(a * tm, (a + 1) * tm)
        z_out[rows, :] = z_s[slot]
        ang_t = invft_ref[...] * posr_ref[a].astype(F32)
        cos_t = jnp.cos(ang_t)
        sin_t = jnp.sin(ang_t)
        lane = lax.broadcasted_iota(jnp.int32, (tm, LANES), 1)
        table = jnp.concatenate([cos_t, sin_t, jnp.zeros((LANES - ROPE, tm), F32)], axis=0).T
        cosf = jnp.where(lane < half, table, jnp.where(lane < ROPE, pltpu.roll(table, half, 1), 0.0))
        sinf = jnp.where(lane < half, -pltpu.roll(table, LANES - half, 1),
                         jnp.where(lane < ROPE, table, 0.0))
        yield

        qn = _rms(lat_s[slot, :, :LORA], qlnw_ref[...], LORA)
        kvn = _rms(lat_s[slot, :, LORA:], kvlnw_ref[...], LORA)
        qn_t = qn.T.astype(BF16)
        kvn_t = kvn.T.astype(BF16)
        kvn = kvn.astype(BF16)
        qt_nope = jnp.dot(wuqnt_ref[...], qn_t, preferred_element_type=F32)
        qt_pe = jnp.dot(wuqpt_ref[...], qn_t, preferred_element_type=F32)
        vt_out[a] = jnp.dot(wukvt_ref[...], kvn_t, preferred_element_type=F32).astype(BF16)
        k_nope = jnp.dot(kvn, wukk_ref[...], preferred_element_type=F32)
        k_pe = _rope(_rms(kpe_s[slot], knwp_ref[...], ROPE), cosf, sinf, lane).astype(BF16)
        yield

        scale = QK_HEAD ** -0.5
        qwn = qnwn_ref[...] * scale
        qwp = qnwp_ref[...] * scale
        for h in range(N_HEADS):
            lo = h * HEAD
            base = h * 2 * HEAD
            xn = qt_nope[lo:lo + NOPE, :]
            xn = xn * lax.rsqrt(jnp.sum(xn * xn, axis=0, keepdims=True) * (1.0 / NOPE) + EPS) * qwn
            xp = qt_pe[h * ROPE:(h + 1) * ROPE, :]
            xp = xp * lax.rsqrt(jnp.sum(xp * xp, axis=0, keepdims=True) * (1.0 / ROPE) + EPS) * qwp
            t1, t2 = xp[:half], xp[half:]
            qt_out[a, base:base + NOPE, :] = xn.astype(BF16)
            qt_out[a, base + NOPE:base + NOPE + half, :] = (t1 * cos_t - t2 * sin_t).astype(BF16)
            qt_out[a, base + NOPE + half:base + QK_HEAD, :] = (t2 * cos_t + t1 * sin_t).astype(BF16)
            qt_out[a, base + QK_HEAD:base + 2 * HEAD, :] = jnp.zeros((2 * HEAD - QK_HEAD, tm), BF16)
            k_out[rows, base:base + HEAD] = _rms(k_nope[:, lo:lo + HEAD], knwn_ref[...], NOPE).astype(BF16)
            k_out[rows, base + HEAD:base + 2 * HEAD] = k_pe
        yield

        g_all = g_s[slot]
        seq_start = (2 * i + a) % tiles_per_seq == 0
        prev = jnp.where(seq_start, 0.0, tail[...])
        row8 = lax.broadcasted_iota(jnp.int32, prev.shape, 0)
        conv = convw_ref[CONV_W - 1:CONV_W, :] * g_all
        for s in range(1, CONV_W):
            rolled = pltpu.roll(g_all, s, 0)
            first = jnp.where(row8 < s, pltpu.roll(prev, s, 0), rolled[:8])
            shifted = jnp.concatenate([first, rolled[8:]], axis=0)
            conv = conv + convw_ref[CONV_W - 1 - s:CONV_W - s, :] * shifted
        tail[...] = g_all[tm - 8:tm, :]
        yield
        act = conv * _sigmoid(conv)
        for h in range(N_HEADS):
            lo = h * HEAD
            gq = act[:, lo:lo + HEAD]
            gk = act[:, MIX + lo:MIX + lo + HEAD]
            gq = gq * lax.rsqrt(jnp.sum(gq * gq, axis=-1, keepdims=True) + EPS) * (HEAD ** -0.5)
            gk = gk * lax.rsqrt(jnp.sum(gk * gk, axis=-1, keepdims=True) + EPS)
            gq_out[rows, lo:lo + HEAD] = gq.astype(BF16)
            gk_out[rows, lo:lo + HEAD] = gk.astype(BF16)
        gv_out[rows, :] = act[:, 2 * MIX:].astype(BF16)
        yield

        ab = ab_s[slot]
        sp_in = ab + dtb_ref[...]
        softplus = jnp.maximum(sp_in, 0.0) + jnp.log(1.0 + jnp.exp(-jnp.abs(sp_in)))
        g = -jnp.exp(alog_ref[...]) * softplus
        beta = _sigmoid(ab)
        rr = lax.broadcasted_iota(jnp.int32, (tm, tm), 0)
        cc = lax.broadcasted_iota(jnp.int32, (tm, tm), 1)
        tri = jnp.where((cc <= rr) & ((cc // CHUNK) == (rr // CHUNK)), 1.0, 0.0).astype(BF16)
        g_hi, g_mid, g_lo = _split3(g)
        gcum = (jnp.dot(tri, g_hi, preferred_element_type=F32)
                + jnp.dot(tri, g_mid, preferred_element_type=F32)
                + jnp.dot(tri, g_lo, preferred_element_type=F32))
        gb = jnp.where(lane < N_HEADS, gcum, jnp.where(lane < 2 * N_HEADS, beta, 0.0))
        gb_out[rows, :] = gb
        er = lax.broadcasted_iota(jnp.int32, (8, LANES), 0)
        ec = lax.broadcasted_iota(jnp.int32, (8, LANES), 1)
        eye8 = jnp.where(er == ec, 1.0, 0.0).astype(BF16)
        for c in range(tm // CHUNK):
            parts = _split3(gb[c * CHUNK:(c + 1) * CHUNK, :])
            acc = None
            for p in parts:
                t = lax.dot_general(eye8, p, (((1,), (1,)), ((), ())), preferred_element_type=F32)
                acc = t if acc is None else acc + t
            rows_out[a * (tm // CHUNK) + c] = acc

    @pl.when(i == 0)
    def _():
        tail[...] = jnp.zeros(tail.shape, F32)
        _interleave(front(x0_ref, 0))

    _interleave(front(xa_ref, 1), back(0, 0))
    _interleave(front(xb_ref, 0), back(1, 1))


def _attn_kernel(qt_ref, k_ref, vt_ref, w_ref, o_ref, m_s, l_s, acc_s, st_s, *, tq, n_sub):
    i = pl.program_id(1)
    heads = range(N_HEADS)
    ones_rows = 16
    m_s[...] = jnp.full(m_s.shape, -jnp.inf, F32)
    l_s[...] = jnp.zeros(l_s.shape, F32)
    acc_s[...] = jnp.zeros(acc_s.shape, F32)
    kk = lax.broadcasted_iota(jnp.int32, (tq, tq), 0)
    qq = lax.broadcasted_iota(jnp.int32, (tq, tq), 1)
    ones = jnp.ones((ones_rows, tq), BF16)

    def scores(j, items, buf):
        r0 = pl.multiple_of(j * tq, tq)
        for a, h, _ in items:
            st_s[buf, a * N_HEADS + h] = jnp.dot(
                k_ref[pl.ds(r0, tq), 2 * HEAD * h:2 * HEAD * (h + 1)],
                qt_ref[a, 2 * HEAD * h:2 * HEAD * (h + 1), :],
                preferred_element_type=F32)

    def update(j, items, buf):
        n_it = range(len(items))
        slot = [a * N_HEADS + h for a, h, _ in items]
        st = [st_s[buf, s] for s in slot]
        st = [jnp.where(kk <= qq, st[x], -jnp.inf) if items[x][2] else st[x] for x in n_it]
        m_old = [m_s[s] for s in slot]
        m_new = [jnp.maximum(m_old[x], jnp.max(st[x], axis=0, keepdims=True)) for x in n_it]
        alpha = [jnp.exp(m_old[x] - m_new[x]) for x in n_it]
        p = [jnp.exp(st[x] - m_new[x]).astype(BF16) for x in n_it]
        pv = [jnp.dot(jnp.concatenate([vt_ref[j, HEAD * h:HEAD * (h + 1), :], ones], axis=0), p[x],
                      preferred_element_type=F32) for x, (a, h, _) in enumerate(items)]
        for x, s in enumerate(slot):
            m_s[s] = m_new[x]
            l_s[s] = alpha[x] * l_s[s] + pv[x][HEAD:HEAD + 1]
            acc_s[s] = alpha[x] * acc_s[s] + pv[x][:HEAD]

    full = [(a, h, False) for a in range(n_sub) for h in heads]
    diag = lambda d: [(a, h, a == d) for a in range(d, n_sub) for h in heads]
    first = i * n_sub

    def pair(t, carry):
        scores(2 * t + 1, full, 1)
        update(2 * t, full, 0)
        scores(2 * t + 2, full, 0)
        update(2 * t + 1, full, 1)
        return carry

    scores(0, full, 0)
    lax.fori_loop(0, first // 2, pair, 0)
    for d in range(n_sub):
        if d + 1 < n_sub:
            scores(first + d + 1, diag(d + 1), (d + 1) % 2)
        update(first + d, diag(d), d % 2)
    for a in range(n_sub):
        for h in heads:
            s = a * N_HEADS + h
            o = (acc_s[s] / l_s[s]).T
            o_ref[a * tq:(a + 1) * tq, HEAD * h:HEAD * (h + 1)] = _rms(o, w_ref[h], HEAD).astype(BF16)


def _gdn_kernel(q_ref, k_ref, v_ref, z_ref, gb_ref, rows_ref, gnw_ref, o_ref,
                s_ref, u_s, wq_s, kd_s, at_s, *, n_chunks, unroll):
    ii = lax.broadcasted_iota(jnp.int32, (CHUNK, CHUNK), 0)
    jj = lax.broadcasted_iota(jnp.int32, (CHUNK, CHUNK), 1)
    eye = jnp.where(ii == jj, 1.0, 0.0)
    heads = range(N_HEADS)

    def solve(t):
        ns = [t * unroll + c for c in range(unroll)]
        r0 = [pl.multiple_of(n * CHUNK, CHUNK) for n in ns]
        scal = [gb_ref[pl.ds(r, CHUNK), :] for r in r0]
        rows = [rows_ref[n] for n in ns]
        items = [(c, h) for c in range(unroll) for h in heads]
        cols = lambda h: slice(h * HEAD, (h + 1) * HEAD)
        q = [q_ref[pl.ds(r0[c], CHUNK), cols(h)] for c, h in items]
        k = [k_ref[pl.ds(r0[c], CHUNK), cols(h)] for c, h in items]
        v = [v_ref[pl.ds(r0[c], CHUNK), cols(h)] for c, h in items]
        g_col = [jnp.broadcast_to(scal[c][:, h:h + 1], (CHUNK, LANES)) for c, h in items]
        b_col = [jnp.broadcast_to(scal[c][:, N_HEADS + h:N_HEADS + h + 1], (CHUNK, CHUNK))
                 for c, h in items]
        g_row = [rows[c][h:h + 1, :] for c, h in items]
        b_row = [rows[c][N_HEADS + h:N_HEADS + h + 1, :] for c, h in items]
        n_it = range(len(items))
        decay = [jnp.exp(jnp.where(ii >= jj, g_col[x][:, :CHUNK] - g_row[x], -jnp.inf)) for x in n_it]
        qkk = [_mm_nt(jnp.concatenate([q[x], k[x]], axis=0), k[x]) for x in n_it]
        yield
        lmat = [jnp.where(ii > jj, b_col[x] * qkk[x][CHUNK:] * decay[x], 0.0) for x in n_it]
        inv = [eye - lmat[x] for x in n_it]
        pw = [_mm(lmat[x], lmat[x]) for x in n_it]
        yield
        for step in range(5):
            inv = [inv[x] + _mm(inv[x], pw[x]) for x in n_it]
            if step < 4:
                pw = [_mm(pw[x], pw[x]) for x in n_it]
            yield
        u = [_mm(inv[x] * b_row[x], v[x]) for x in n_it]
        w = [_mm(inv[x] * (b_row[x] * jnp.exp(g_row[x])), k[x]) for x in n_it]
        yield
        for x, (c, h) in enumerate(items):
            idx = ns[c] * N_HEADS + h
            g_last = g_col[x][CHUNK - 1:CHUNK, :]
            u_s[idx] = u[x]
            wq_s[idx, 0:CHUNK, :] = w[x].astype(BF16)
            wq_s[idx, CHUNK:2 * CHUNK, :] = (q[x].astype(F32) * jnp.exp(g_col[x])).astype(BF16)
            kd_s[idx] = (k[x].astype(F32) * jnp.exp(g_last - g_col[x])).astype(BF16)
            at_s[idx] = (qkk[x][:CHUNK] * decay[x]).astype(BF16)

    gnw = gnw_ref[...]

    def scan(t):
        for c in range(unroll):
            n = t * unroll + c
            r0 = pl.multiple_of(n * CHUNK, CHUNK)
            g_last = gb_ref[pl.ds(r0 + CHUNK - 1, 1), :]
            state = [s_ref[h] for h in heads]
            sb = [x.astype(BF16) for x in state]
            ws = [_mm(wq_s[n * N_HEADS + h], sb[h]) for h in heads]
            yield
            v_new = [(u_s[n * N_HEADS + h] - ws[h][:CHUNK]).astype(BF16) for h in heads]
            o = [ws[h][CHUNK:] + _mm(at_s[n * N_HEADS + h], v_new[h]) for h in heads]
            upd = [_mm_tn(kd_s[n * N_HEADS + h], v_new[h]) for h in heads]
            yield
            for h in heads:
                c_dec = jnp.exp(jnp.broadcast_to(g_last[:, h:h + 1], (HEAD, HEAD)))
                s_ref[h] = state[h] * c_dec + upd[h]
                zh = z_ref[pl.ds(r0, CHUNK), h * HEAD:(h + 1) * HEAD].astype(F32)
                gated = _rms(o[h], gnw, HEAD) * (zh * _sigmoid(zh))
                o_ref[pl.ds(r0, CHUNK), h * HEAD:(h + 1) * HEAD] = gated.astype(BF16)

    n_groups = n_chunks // unroll
    s_ref[...] = jnp.zeros_like(s_ref)
    _interleave(solve(0))

    def body(t, carry):
        _interleave(solve(t), scan(t - 1))
        return carry

    lax.fori_loop(1, n_groups, body, 0)
    _interleave(scan(n_groups - 1))


def _mlp_kernel(x_ref, mla_ref, gdn_ref, woa_ref, wob_ref, nw_ref, wup_ref, wdn_ref, o_ref):
    h = (x_ref[...]
         + jnp.dot(mla_ref[...], woa_ref[...], preferred_element_type=F32)
         + jnp.dot(gdn_ref[...], wob_ref[...], preferred_element_type=F32))
    hn = _rms(h, nw_ref[...], D_MODEL).astype(BF16)
    o_ref[...] = h
    for c in range(D_FF // MLP_FC):
        u = jnp.dot(hn, wup_ref[:, c * MLP_FC:(c + 1) * MLP_FC], preferred_element_type=F32)
        a = jnp.square(jnp.maximum(u, 0.0)).astype(BF16)
        o_ref[...] += jnp.dot(a, wdn_ref[c * MLP_FC:(c + 1) * MLP_FC, :], preferred_element_type=F32)


def _const_spec(shape):
    nd = len(shape)
    return pl.BlockSpec(shape, lambda *_: (0,) * nd, pipeline_mode=pl.Buffered(1))


def _pad_lanes(v, width=LANES):
    v = v.reshape(1, -1).astype(F32)
    return jnp.pad(v, ((0, 0), (0, width - v.shape[1])))


def _layer(h, pos, inv_freq, p):
    B, S, _ = h.shape
    T = B * S
    x2 = h.reshape(T, D_MODEL)

    w_in = p["w_in"]
    o_q, o_kv, o_pe = 0, LORA, 2 * LORA
    o_g = o_pe + ROPE
    o_z = o_g + 3 * MIX
    o_a = o_z + MIX
    w_lat = w_in[:, o_q:o_pe].astype(BF16)
    w_kpe = jnp.pad(w_in[:, o_pe:o_g], ((0, 0), (0, LANES - ROPE))).astype(BF16)
    w_g = w_in[:, o_g:o_z].astype(BF16)
    w_z = w_in[:, o_z:o_a].astype(BF16)
    w_ab = jnp.pad(w_in[:, o_a:], ((0, 0), (0, LANES - 2 * N_HEADS))).astype(BF16)
    w_uq = p["w_uq"].reshape(LORA, N_HEADS, QK_HEAD)
    w_uq_nt = w_uq[:, :, :NOPE].reshape(LORA, N_HEADS * NOPE).T.astype(BF16)
    w_uq_pt = w_uq[:, :, NOPE:].reshape(LORA, N_HEADS * ROPE).T.astype(BF16)
    w_ukv = p["w_ukv"].reshape(LORA, N_HEADS, NOPE + HEAD)
    w_uk_k = w_ukv[:, :, :NOPE].reshape(LORA, N_HEADS * NOPE).astype(BF16)
    w_uk_vt = w_ukv[:, :, NOPE:].reshape(LORA, N_HEADS * HEAD).T.astype(BF16)

    tm = PREP_TM
    n_tiles = T // tm
    assert n_tiles % 2 == 0
    row_spec = lambda w: pl.BlockSpec((2 * tm, w), lambda i: (i, 0))
    col_bcast = lambda v: jnp.broadcast_to(v.astype(F32)[:, None], (v.shape[0], tm))
    x_tile = lambda index: pl.BlockSpec((tm, D_MODEL), index)
    prep_in = [
        (x2, x_tile(lambda i: (0, 0))),
        (x2, x_tile(lambda i: (2 * i + 1, 0))),
        (x2, x_tile(lambda i: (jnp.minimum(2 * i + 2, n_tiles - 1), 0))),
        (pos.reshape(n_tiles, 1, tm), pl.BlockSpec((2, 1, tm), lambda i: (i, 0, 0))),
        (p["attn_norm_w"].reshape(1, D_MODEL), None),
        (w_lat, None), (w_kpe, None), (w_ab, None), (w_g, None), (w_z, None),
        (p["q_lat_norm_w"].reshape(1, LORA), None),
        (p["kv_lat_norm_w"].reshape(1, LORA), None),
        (w_uq_nt, None), (w_uq_pt, None), (w_uk_k, None), (w_uk_vt, None),
        (col_bcast(p["q_norm_w"][:NOPE]), None),
        (col_bcast(p["q_norm_w"][NOPE:]), None),
        (p["k_norm_w"][:NOPE].reshape(1, NOPE), None),
        (_pad_lanes(p["k_norm_w"][NOPE:]), None),
        (col_bcast(inv_freq), None),
        (p["conv_w"], None),
        (_pad_lanes(p["a_log"]), None),
        (_pad_lanes(p["dt_bias"]), None),
    ]
    prep_args = [a for a, _ in prep_in]
    prep_specs = [s if s is not None else _const_spec(a.shape) for a, s in prep_in]
    out_shapes = [
        jax.ShapeDtypeStruct((n_tiles, 2 * MIX, tm), BF16),
        jax.ShapeDtypeStruct((T, 2 * MIX), BF16),
        jax.ShapeDtypeStruct((n_tiles, MIX, tm), BF16),
        jax.ShapeDtypeStruct((T, MIX), BF16),
        jax.ShapeDtypeStruct((T, MIX), BF16),
        jax.ShapeDtypeStruct((T, MIX), BF16),
        jax.ShapeDtypeStruct((T, MIX), BF16),
        jax.ShapeDtypeStruct((T, LANES), F32),
        jax.ShapeDtypeStruct((T // CHUNK, 8, CHUNK), F32),
    ]
    tile_spec = lambda r: pl.BlockSpec((2, r, tm), lambda i: (i, 0, 0))
    out_specs = [tile_spec(2 * MIX), row_spec(2 * MIX), tile_spec(MIX)] + [row_spec(MIX)] * 4 + [
        row_spec(LANES),
        pl.BlockSpec((2 * tm // CHUNK, 8, CHUNK), lambda i: (i, 0, 0)),
    ]
    qt, k, vt, gq, gk, gv, gz, gb, rows = pl.pallas_call(
        functools.partial(_prep_kernel, tm=tm, tiles_per_seq=S // tm),
        grid=(n_tiles // 2,),
        in_specs=prep_specs,
        out_specs=out_specs,
        out_shape=out_shapes,
        scratch_shapes=[
            pltpu.VMEM((2, tm, 2 * LORA), F32),
            pltpu.VMEM((2, tm, 3 * MIX), F32),
            pltpu.VMEM((2, tm, LANES), F32),
            pltpu.VMEM((2, tm, LANES), F32),
            pltpu.VMEM((2, tm, MIX), BF16),
            pltpu.VMEM((8, 3 * MIX), F32),
        ],
        compiler_params=pltpu.CompilerParams(dimension_semantics=("arbitrary",),
                                             vmem_limit_bytes=VMEM_LIMIT),
        name="prep",
    )(*prep_args)

    tq = tm
    nq = S // tq
    n_sub = ATTN_SUB
    assert n_sub % 2 == 0 and nq % n_sub == 0
    n_steps = nq // n_sub
    mla_o = pl.pallas_call(
        functools.partial(_attn_kernel, tq=tq, n_sub=n_sub),
        grid=(B, n_steps),
        in_specs=[
            pl.BlockSpec((n_sub, 2 * MIX, tq), lambda b, i: (b * n_steps + i, 0, 0)),
            pl.BlockSpec((None, S, 2 * MIX), lambda b, i: (b, 0, 0)),
            pl.BlockSpec((None, nq, MIX, tq), lambda b, i: (b, 0, 0, 0)),
            _const_spec((N_HEADS, 1, HEAD)),
        ],
        out_specs=pl.BlockSpec((None, n_sub * tq, MIX), lambda b, i: (b, i, 0)),
        out_shape=jax.ShapeDtypeStruct((B, S, MIX), BF16),
        scratch_shapes=[pltpu.VMEM((n_sub * N_HEADS, 1, tq), F32),
                        pltpu.VMEM((n_sub * N_HEADS, 1, tq), F32),
                        pltpu.VMEM((n_sub * N_HEADS, HEAD, tq), F32),
                        pltpu.VMEM((2, n_sub * N_HEADS, tq, tq), F32)],
        compiler_params=pltpu.CompilerParams(
            dimension_semantics=("arbitrary", "arbitrary"),
            vmem_limit_bytes=VMEM_LIMIT),
        name="attn",
    )(qt, k.reshape(B, S, 2 * MIX), vt.reshape(B, nq, MIX, tq),
      p["mla_out_norm_w"].reshape(N_HEADS, 1, HEAD))

    n_chunks = S // CHUNK
    seq_spec = lambda w: pl.BlockSpec((None, S, w), lambda b: (b, 0, 0))
    gdn_o = pl.pallas_call(
        functools.partial(_gdn_kernel, n_chunks=n_chunks, unroll=GDN_UNROLL),
        grid=(B,),
        in_specs=[seq_spec(MIX)] * 4 + [
            seq_spec(LANES),
            pl.BlockSpec((None, n_chunks, 8, CHUNK), lambda b: (b, 0, 0, 0)),
            _const_spec((1, HEAD)),
        ],
        out_specs=seq_spec(MIX),
        out_shape=jax.ShapeDtypeStruct((B, S, MIX), BF16),
        scratch_shapes=[
            pltpu.VMEM((N_HEADS, HEAD, HEAD), F32),
            pltpu.VMEM((n_chunks * N_HEADS, CHUNK, HEAD), F32),
            pltpu.VMEM((n_chunks * N_HEADS, 2 * CHUNK, HEAD), BF16),
            pltpu.VMEM((n_chunks * N_HEADS, CHUNK, HEAD), BF16),
            pltpu.VMEM((n_chunks * N_HEADS, CHUNK, CHUNK), BF16),
        ],
        compiler_params=pltpu.CompilerParams(dimension_semantics=("arbitrary",),
                                             vmem_limit_bytes=VMEM_LIMIT),
        name="gdn",
    )(gq.reshape(B, S, MIX), gk.reshape(B, S, MIX), gv.reshape(B, S, MIX), gz.reshape(B, S, MIX),
      gb.reshape(B, S, LANES), rows.reshape(B, n_chunks, 8, CHUNK),
      p["gdn_norm_w"].reshape(1, HEAD))

    tm2 = MLP_TM
    w_out = p["w_out"].astype(BF16)
    tok_spec = lambda w: pl.BlockSpec((tm2, w), lambda i: (i, 0))
    out = pl.pallas_call(
        _mlp_kernel,
        grid=(T // tm2,),
        in_specs=[
            tok_spec(D_MODEL), tok_spec(MIX), tok_spec(MIX),
            _const_spec((MIX, D_MODEL)), _const_spec((MIX, D_MODEL)),
            _const_spec((1, D_MODEL)),
            _const_spec((D_MODEL, D_FF)), _const_spec((D_FF, D_MODEL)),
        ],
        out_specs=tok_spec(D_MODEL),
        out_shape=jax.ShapeDtypeStruct((T, D_MODEL), F32),
        compiler_params=pltpu.CompilerParams(dimension_semantics=("arbitrary",),
                                             vmem_limit_bytes=VMEM_LIMIT),
        name="mlp",
    )(x2, mla_o.reshape(T, MIX), gdn_o.reshape(T, MIX), w_out[:MIX], w_out[MIX:],
      p["mlp_norm_w"].reshape(1, D_MODEL), p["w_up"].astype(BF16), p["w_down"].astype(BF16))
    return out.reshape(B, S, D_MODEL)


def kernel(x, positions, attn_norm_w, w_in, q_lat_norm_w, w_uq, kv_lat_norm_w, w_ukv, q_norm_w,
           k_norm_w, mla_out_norm_w, conv_w, a_log, dt_bias, gdn_norm_w, w_out, mlp_norm_w, w_up,
           w_down):
    B, S, _ = x.shape
    half = ROPE // 2
    inv_freq = ROPE_THETA ** (-jnp.arange(half, dtype=F32) / half)
    params = dict(attn_norm_w=attn_norm_w, w_in=w_in, q_lat_norm_w=q_lat_norm_w, w_uq=w_uq,
                  kv_lat_norm_w=kv_lat_norm_w, w_ukv=w_ukv, q_norm_w=q_norm_w, k_norm_w=k_norm_w,
                  mla_out_norm_w=mla_out_norm_w, conv_w=conv_w, a_log=a_log, dt_bias=dt_bias,
                  gdn_norm_w=gdn_norm_w, w_out=w_out, mlp_norm_w=mlp_norm_w, w_up=w_up,
                  w_down=w_down)
    h = x
    for l in range(attn_norm_w.shape[0]):
        h = _layer(h, positions, inv_freq, {name: val[l] for name, val in params.items()})
    return h
```

```python
import functools

import jax
import jax.numpy as jnp
from jax import lax
from jax.experimental import pallas as pl
from jax.experimental.pallas import tpu as pltpu

F32 = jnp.float32
BF16 = jnp.bfloat16

D_MODEL = 1024
N_HEADS = 4
LORA = 256
NOPE = 128
ROPE = 64
HEAD = 128
QK_HEAD = NOPE + ROPE
ROPE_THETA = 10000.0
CONV_W = 4
CHUNK = 64
D_FF = 4 * D_MODEL
EPS = 1e-6
LOG2E = 1.4426950408889634
LANES = 128
MIX = N_HEADS * HEAD

PREP_TM = 256
ATTN_SUB = 2
GDN_UNROLL = 4
MLP_TM = 512
MLP_FC = 1024
VMEM_LIMIT = 56 * 1024 * 1024


def _mm(a, b):
    return jnp.dot(a.astype(BF16), b.astype(BF16), preferred_element_type=F32)


def _mm_nt(a, b):
    return lax.dot_general(a.astype(BF16), b.astype(BF16), (((1,), (1,)), ((), ())),
                           preferred_element_type=F32)


def _mm_tn(a, b):
    return lax.dot_general(a.astype(BF16), b.astype(BF16), (((0,), (0,)), ((), ())),
                           preferred_element_type=F32)


def _split3(x):
    hi = x.astype(BF16)
    r1 = x - hi.astype(F32)
    mid = r1.astype(BF16)
    lo = (r1 - mid.astype(F32)).astype(BF16)
    return hi, mid, lo


def _interleave(*gens):
    live = list(gens)
    while live:
        for g in list(live):
            try:
                next(g)
            except StopIteration:
                live.remove(g)


def _rms(x, w, n):
    return x * lax.rsqrt(jnp.sum(x * x, axis=-1, keepdims=True) * (1.0 / n) + EPS) * w


def _sigmoid(x):
    return 1.0 / (1.0 + jnp.exp(-x))


def _rope(x, cosf, sinf, lane):
    rot = jnp.where(lane < ROPE // 2, pltpu.roll(x, LANES - ROPE // 2, 1),
                    pltpu.roll(x, ROPE // 2, 1))
    return x * cosf + rot * sinf


def _prep_kernel(x0_ref, xa_ref, xb_ref, posr_ref, anw_ref, wlat_ref, wkpe_ref, wab_ref, wg_ref, wz_ref,
                 qlnw_ref, kvlnw_ref, wuqnt_ref, wuqpt_ref, wukk_ref, wukvt_ref,
                 qnwn_ref, qnwp_ref, knwn_ref, knwp_ref, invft_ref, convw_ref,
                 alog_ref, dtb_ref,
                 qt_out, k_out, vt_out, gq_out, gk_out, gv_out, z_out, gb_out, rows_out,
                 lat_s, g_s, kpe_s, ab_s, z_s, tail, tri_s, *, tm, tiles_per_seq):
    i = pl.program_id(0)
    half = ROPE // 2

    def front(x_ref, slot):
        xn = _rms(x_ref[...], anw_ref[...], D_MODEL).astype(BF16)
        yield
        lat_s[slot] = jnp.dot(xn, wlat_ref[...], preferred_element_type=F32)
        yield
        g_s[slot] = jnp.dot(xn, wg_ref[...], preferred_element_type=F32)
        yield
        kpe_s[slot] = jnp.dot(xn, wkpe_ref[...], preferred_element_type=F32)
        ab_s[slot] = jnp.dot(xn, wab_ref[...], preferred_element_type=F32)
        z_s[slot] = jnp.dot(xn, wz_ref[...], preferred_element_type=F32).astype(BF16)

    def back(slot, a):
        rows = slice(a * tm, (a + 1) * tm)
        z_out[rows, :] = z_s[slot]
        ang_t = invft_ref[...] * posr_ref[a].astype(F32)
        cos_t = jnp.cos(ang_t)
        sin_t = jnp.sin(ang_t)
        lane = lax.broadcasted_iota(jnp.int32, (tm, LANES), 1)
        table = jnp.concatenate([cos_t, sin_t, jnp.zeros((LANES - ROPE, tm), F32)], axis=0).T
        cosf = jnp.where(lane < half, table, jnp.where(lane < ROPE, pltpu.roll(table, half, 1), 0.0))
        sinf = jnp.where(lane < half, -pltpu.roll(table, LANES - half, 1),
                         jnp.where(lane < ROPE, table, 0.0))
        yield

        qn = _rms(lat_s[slot, :, :LORA], qlnw_ref[...], LORA)
        kvn = _rms(lat_s[slot, :, LORA:], kvlnw_ref[...], LORA)
        qn_t = qn.T.astype(BF16)
        kvn_t = kvn.T.astype(BF16)
        kvn = kvn.astype(BF16)
        qt_nope = jnp.dot(wuqnt_ref[...], qn_t, preferred_element_type=F32)
        qt_pe = jnp.dot(wuqpt_ref[...], qn_t, preferred_element_type=F32)
        vt_out[a] = jnp.dot(wukvt_ref[...], kvn_t, preferred_element_type=F32).astype(BF16)
        k_nope = jnp.dot(kvn, wukk_ref[...], preferred_element_type=F32)
        k_pe = _rope(_rms(kpe_s[slot], knwp_ref[...], ROPE), cosf, sinf, lane).astype(BF16)
        yield

        scale = QK_HEAD ** -0.5 * LOG2E
        qwn = qnwn_ref[...] * scale
        qwp = qnwp_ref[...] * scale
        for h in range(N_HEADS):
            lo = h * HEAD
            base = h * 2 * HEAD
            xn = qt_nope[lo:lo + NOPE, :]
            xn = xn * lax.rsqrt(jnp.sum(xn * xn, axis=0, keepdims=True) * (1.0 / NOPE) + EPS) * qwn
            xp = qt_pe[h * ROPE:(h + 1) * ROPE, :]
            xp = xp * lax.rsqrt(jnp.sum(xp * xp, axis=0, keepdims=True) * (1.0 / ROPE) + EPS) * qwp
            t1, t2 = xp[:half], xp[half:]
            qt_out[a, base:base + NOPE, :] = xn.astype(BF16)
            qt_out[a, base + NOPE:base + NOPE + half, :] = (t1 * cos_t - t2 * sin_t).astype(BF16)
            qt_out[a, base + NOPE + half:base + QK_HEAD, :] = (t2 * cos_t + t1 * sin_t).astype(BF16)
            qt_out[a, base + QK_HEAD:base + 2 * HEAD, :] = jnp.zeros((2 * HEAD - QK_HEAD, tm), BF16)
            k_out[rows, base:base + HEAD] = _rms(k_nope[:, lo:lo + HEAD], knwn_ref[...], NOPE).astype(BF16)
            k_out[rows, base + HEAD:base + 2 * HEAD] = k_pe
        yield

        g_all = g_s[slot]
        seq_start = (2 * i + a) % tiles_per_seq == 0
        prev = jnp.where(seq_start, 0.0, tail[...])
        row8 = lax.broadcasted_iota(jnp.int32, prev.shape, 0)
        conv = convw_ref[CONV_W - 1:CONV_W, :] * g_all
        for s in range(1, CONV_W):
            rolled = pltpu.roll(g_all, s, 0)
            first = jnp.where(row8 < s, pltpu.roll(prev, s, 0), rolled[:8])
            shifted = jnp.concatenate([first, rolled[8:]], axis=0)
            conv = conv + convw_ref[CONV_W - 1 - s:CONV_W - s, :] * shifted
        tail[...] = g_all[tm - 8:tm, :]
        yield
        half_conv = 0.5 * conv
        act = half_conv + half_conv * jnp.tanh(half_conv)
        for h in range(N_HEADS):
            lo = h * HEAD
            gq = act[:, lo:lo + HEAD]
            gk = act[:, MIX + lo:MIX + lo + HEAD]
            gq = gq * lax.rsqrt(jnp.sum(gq * gq, axis=-1, keepdims=True) + EPS) * (HEAD ** -0.5)
            gk = gk * lax.rsqrt(jnp.sum(gk * gk, axis=-1, keepdims=True) + EPS)
            gq_out[rows, lo:lo + HEAD] = gq.astype(BF16)
            gk_out[rows, lo:lo + HEAD] = gk.astype(BF16)
        gv_out[rows, :] = act[:, 2 * MIX:].astype(BF16)
        yield

        ab = ab_s[slot]
        sp_in = ab + dtb_ref[...]
        softplus = jnp.maximum(sp_in, 0.0) + jnp.log(1.0 + jnp.exp(-jnp.abs(sp_in)))
        g = -jnp.exp(alog_ref[...]) * softplus
        beta = _sigmoid(ab)
        tri = tri_s[...]
        g_hi, g_mid, g_lo = _split3(g)
        gcum = (jnp.dot(tri, g_hi, preferred_element_type=F32)
                + jnp.dot(tri, g_mid, preferred_element_type=F32)
                + jnp.dot(tri, g_lo, preferred_element_type=F32))
        gb = jnp.where(lane < N_HEADS, gcum, jnp.where(lane < 2 * N_HEADS, beta, 0.0))
        gb_out[rows, :] = gb
        er = lax.broadcasted_iota(jnp.int32, (8, LANES), 0)
        ec = lax.broadcasted_iota(jnp.int32, (8, LANES), 1)
        eye8 = jnp.where(er == ec, 1.0, 0.0).astype(BF16)
        for c in range(tm // CHUNK):
            parts = _split3(gb[c * CHUNK:(c + 1) * CHUNK, :])
            acc = None
            for p in parts:
                t = lax.dot_general(eye8, p, (((1,), (1,)), ((), ())), preferred_element_type=F32)
                acc = t if acc is None else acc + t
            rows_out[a * (tm // CHUNK) + c] = acc

    @pl.when(i == 0)
    def _():
        tail[...] = jnp.zeros(tail.shape, F32)
        rr = lax.broadcasted_iota(jnp.int32, (tm, tm), 0)
        cc = lax.broadcasted_iota(jnp.int32, (tm, tm), 1)
        tri_s[...] = jnp.where((cc <= rr) & ((cc // CHUNK) == (rr // CHUNK)), 1.0, 0.0).astype(BF16)
        _interleave(front(x0_ref, 0))

    _interleave(front(xa_ref, 1), back(0, 0))
    _interleave(front(xb_ref, 0), back(1, 1))


def _attn_kernel(qt_ref, k_ref, vt_ref, w_ref, o_ref, m_s, l_s, acc_s, st_s, *, tq, n_sub):
    i = pl.program_id(1)
    heads = range(N_HEADS)
    ones_rows = 16
    m_s[...] = jnp.full(m_s.shape, -jnp.inf, F32)
    l_s[...] = jnp.zeros(l_s.shape, F32)
    acc_s[...] = jnp.zeros(acc_s.shape, F32)
    kk = lax.broadcasted_iota(jnp.int32, (tq, tq), 0)
    qq = lax.broadcasted_iota(jnp.int32, (tq, tq), 1)
    ones = jnp.ones((ones_rows, tq), BF16)

    def scores(j, items, buf):
        r0 = pl.multiple_of(j * tq, tq)
        for a, h, _ in items:
            st_s[buf, a * N_HEADS + h] = jnp.dot(
                k_ref[pl.ds(r0, tq), 2 * HEAD * h:2 * HEAD * (h + 1)],
                qt_ref[a, 2 * HEAD * h:2 * HEAD * (h + 1), :],
                preferred_element_type=F32)

    def update(j, items, buf):
        n_it = range(len(items))
        slot = [a * N_HEADS + h for a, h, _ in items]
        st = [st_s[buf, s] for s in slot]
        st = [jnp.where(kk <= qq, st[x], -jnp.inf) if items[x][2] else st[x] for x in n_it]
        m_old = [m_s[s] for s in slot]
        m_new = [jnp.maximum(m_old[x], jnp.max(st[x], axis=0, keepdims=True)) for x in n_it]
        alpha = [jnp.exp2(m_old[x] - m_new[x]) for x in n_it]
        p = [jnp.exp2(st[x] - m_new[x]).astype(BF16) for x in n_it]
        pv = [jnp.dot(jnp.concatenate([vt_ref[j, HEAD * h:HEAD * (h + 1), :], ones], axis=0), p[x],
                      preferred_element_type=F32) for x, (a, h, _) in enumerate(items)]
        for x, s in enumerate(slot):
            m_s[s] = m_new[x]
            l_s[s] = alpha[x] * l_s[s] + pv[x][HEAD:HEAD + 1]
            acc_s[s] = alpha[x] * acc_s[s] + pv[x][:HEAD]

    full = [(a, h, False) for a in range(n_sub) for h in heads]
    diag = lambda d: [(a, h, a == d) for a in range(d, n_sub) for h in heads]
    first = i * n_sub

    def pair(t, carry):
        scores(2 * t + 1, full, 1)
        update(2 * t, full, 0)
        scores(2 * t + 2, full, 0)
        update(2 * t + 1, full, 1)
        return carry

    scores(0, full, 0)
    lax.fori_loop(0, first // 2, pair, 0)
    for d in range(n_sub):
        if d + 1 < n_sub:
            scores(first + d + 1, diag(d + 1), (d + 1) % 2)
        update(first + d, diag(d), d % 2)
    for a in range(n_sub):
        for h in heads:
            s = a * N_HEADS + h
            o = (acc_s[s] / l_s[s]).T
            o_ref[a * tq:(a + 1) * tq, HEAD * h:HEAD * (h + 1)] = _rms(o, w_ref[h], HEAD).astype(BF16)


def _gdn_kernel(q_ref, k_ref, v_ref, z_ref, gb_ref, rows_ref, gnw_ref, o_ref,
                s_ref, u_s, wq_s, kd_s, at_s, *, n_chunks, unroll):
    ii = lax.broadcasted_iota(jnp.int32, (CHUNK, CHUNK), 0)
    jj = lax.broadcasted_iota(jnp.int32, (CHUNK, CHUNK), 1)
    eye = jnp.where(ii == jj, 1.0, 0.0)
    heads = range(N_HEADS)

    def solve(t):
        ns = [t * unroll + c for c in range(unroll)]
        r0 = [pl.multiple_of(n * CHUNK, CHUNK) for n in ns]
        scal = [gb_ref[pl.ds(r, CHUNK), :] for r in r0]
        rows = [rows_ref[n] for n in ns]
        items = [(c, h) for c in range(unroll) for h in heads]
        cols = lambda h: slice(h * HEAD, (h + 1) * HEAD)
        q = [q_ref[pl.ds(r0[c], CHUNK), cols(h)] for c, h in items]
        k = [k_ref[pl.ds(r0[c], CHUNK), cols(h)] for c, h in items]
        v = [v_ref[pl.ds(r0[c], CHUNK), cols(h)] for c, h in items]
        g_col = [jnp.broadcast_to(scal[c][:, h:h + 1], (CHUNK, LANES)) for c, h in items]
        b_col = [jnp.broadcast_to(scal[c][:, N_HEADS + h:N_HEADS + h + 1], (CHUNK, LANES))
                 for c, h in items]
        g_row = [rows[c][h:h + 1, :] for c, h in items]
        n_it = range(len(items))
        decay = [jnp.exp(jnp.where(ii >= jj, g_col[x][:, :CHUNK] - g_row[x], -jnp.inf)) for x in n_it]
        qkk = [_mm_nt(jnp.concatenate([q[x], k[x]], axis=0), k[x]) for x in n_it]
        yield
        lmat = [jnp.where(ii > jj, b_col[x][:, :CHUNK] * qkk[x][CHUNK:] * decay[x], 0.0) for x in n_it]
        inv = [eye - lmat[x] for x in n_it]
        pw = [_mm(lmat[x], lmat[x]) for x in n_it]
        yield
        for step in range(4):
            both = [_mm(jnp.concatenate([inv[x], pw[x]], axis=0), pw[x]) for x in n_it]
            inv = [inv[x] + both[x][:CHUNK] for x in n_it]
            pw = [both[x][CHUNK:] for x in n_it]
            yield
        inv = [inv[x] + _mm(inv[x], pw[x]) for x in n_it]
        yield
        e_col = [jnp.exp(g_col[x]) for x in n_it]
        rhs = [jnp.concatenate([v[x].astype(F32) * b_col[x], k[x].astype(F32) * (b_col[x] * e_col[x])],
                               axis=1) for x in n_it]
        uw = [_mm(inv[x], rhs[x]) for x in n_it]
        yield
        for x, (c, h) in enumerate(items):
            idx = ns[c] * N_HEADS + h
            g_last = g_col[x][CHUNK - 1:CHUNK, :]
            u_s[idx] = uw[x][:, :HEAD]
            wq_s[idx, 0:CHUNK, :] = uw[x][:, HEAD:].astype(BF16)
            wq_s[idx, CHUNK:2 * CHUNK, :] = (q[x].astype(F32) * e_col[x]).astype(BF16)
            kd_s[idx] = (k[x].astype(F32) * jnp.exp(g_last - g_col[x])).astype(BF16)
            at_s[idx] = (qkk[x][:CHUNK] * decay[x]).astype(BF16)

    gnw = gnw_ref[...]

    def scan(t):
        for c in range(unroll):
            n = t * unroll + c
            r0 = pl.multiple_of(n * CHUNK, CHUNK)
            g_last = gb_ref[pl.ds(r0 + CHUNK - 1, 1), :]
            state = [s_ref[h] for h in heads]
            sb = [x.astype(BF16) for x in state]
            ws = [_mm(wq_s[n * N_HEADS + h], sb[h]) for h in heads]
            yield
            v_new = [(u_s[n * N_HEADS + h] - ws[h][:CHUNK]).astype(BF16) for h in heads]
            o = [ws[h][CHUNK:] + _mm(at_s[n * N_HEADS + h], v_new[h]) for h in heads]
            upd = [_mm_tn(kd_s[n * N_HEADS + h], v_new[h]) for h in heads]
            yield
            for h in heads:
                c_dec = jnp.exp(jnp.broadcast_to(g_last[:, h:h + 1], (HEAD, HEAD)))
                s_ref[h] = state[h] * c_dec + upd[h]
                zh = z_ref[pl.ds(r0, CHUNK), h * HEAD:(h + 1) * HEAD].astype(F32)
                gated = _rms(o[h], gnw, HEAD) * (zh * _sigmoid(zh))
                o_ref[pl.ds(r0, CHUNK), h * HEAD:(h + 1) * HEAD] = gated.astype(BF16)

    n_groups = n_chunks // unroll
    s_ref[...] = jnp.zeros_like(s_ref)
    _interleave(solve(0))

    def body(t, carry):
        _interleave(solve(t), scan(t - 1))
        return carry

    lax.fori_loop(1, n_groups, body, 0)
    _interleave(scan(n_groups - 1))


def _mlp_kernel(x_ref, mla_ref, gdn_ref, woa_ref, wob_ref, nw_ref, wup_ref, wdn_ref, o_ref):
    h = (x_ref[...]
         + jnp.dot(mla_ref[...], woa_ref[...], preferred_element_type=F32)
         + jnp.dot(gdn_ref[...], wob_ref[...], preferred_element_type=F32))
    hn = _rms(h, nw_ref[...], D_MODEL).astype(BF16)
    o_ref[...] = h
    for c in range(D_FF // MLP_FC):
        u = jnp.dot(hn, wup_ref[:, c * MLP_FC:(c + 1) * MLP_FC], preferred_element_type=F32)
        a = jnp.square(jnp.maximum(u, 0.0)).astype(BF16)
        o_ref[...] += jnp.dot(a, wdn_ref[c * MLP_FC:(c + 1) * MLP_FC, :], preferred_element_type=F32)


def _const_spec(shape):
    nd = len(shape)
    return pl.BlockSpec(shape, lambda *_: (0,) * nd, pipeline_mode=pl.Buffered(1))


def _pad_lanes(v, width=LANES):
    v = v.reshape(1, -1).astype(F32)
    return jnp.pad(v, ((0, 0), (0, width - v.shape[1])))


def _layer(h, pos, inv_freq, p):
    B, S, _ = h.shape
    T = B * S
    x2 = h.reshape(T, D_MODEL)

    w_in = p["w_in"]
    o_q, o_kv, o_pe = 0, LORA, 2 * LORA
    o_g = o_pe + ROPE
    o_z = o_g + 3 * MIX
    o_a = o_z + MIX
    w_lat = w_in[:, o_q:o_pe].astype(BF16)
    w_kpe = jnp.pad(w_in[:, o_pe:o_g], ((0, 0), (0, LANES - ROPE))).astype(BF16)
    w_g = w_in[:, o_g:o_z].astype(BF16)
    w_z = w_in[:, o_z:o_a].astype(BF16)
    w_ab = jnp.pad(w_in[:, o_a:], ((0, 0), (0, LANES - 2 * N_HEADS))).astype(BF16)
    w_uq = p["w_uq"].reshape(LORA, N_HEADS, QK_HEAD)
    w_uq_nt = w_uq[:, :, :NOPE].reshape(LORA, N_HEADS * NOPE).T.astype(BF16)
    w_uq_pt = w_uq[:, :, NOPE:].reshape(LORA, N_HEADS * ROPE).T.astype(BF16)
    w_ukv = p["w_ukv"].reshape(LORA, N_HEADS, NOPE + HEAD)
    w_uk_k = w_ukv[:, :, :NOPE].reshape(LORA, N_HEADS * NOPE).astype(BF16)
    w_uk_vt = w_ukv[:, :, NOPE:].reshape(LORA, N_HEADS * HEAD).T.astype(BF16)

    tm = PREP_TM
    n_tiles = T // tm
    assert n_tiles % 2 == 0
    row_spec = lambda w: pl.BlockSpec((2 * tm, w), lambda i: (i, 0))
    col_bcast = lambda v: jnp.broadcast_to(v.astype(F32)[:, None], (v.shape[0], tm))
    x_tile = lambda index: pl.BlockSpec((tm, D_MODEL), index)
    prep_in = [
        (x2, x_tile(lambda i: (0, 0))),
        (x2, x_tile(lambda i: (2 * i + 1, 0))),
        (x2, x_tile(lambda i: (jnp.minimum(2 * i + 2, n_tiles - 1), 0))),
        (pos.reshape(n_tiles, 1, tm), pl.BlockSpec((2, 1, tm), lambda i: (i, 0, 0))),
        (p["attn_norm_w"].reshape(1, D_MODEL), None),
        (w_lat, None), (w_kpe, None), (w_ab, None), (w_g, None), (w_z, None),
        (p["q_lat_norm_w"].reshape(1, LORA), None),
        (p["kv_lat_norm_w"].reshape(1, LORA), None),
        (w_uq_nt, None), (w_uq_pt, None), (w_uk_k, None), (w_uk_vt, None),
        (col_bcast(p["q_norm_w"][:NOPE]), None),
        (col_bcast(p["q_norm_w"][NOPE:]), None),
        (p["k_norm_w"][:NOPE].reshape(1, NOPE), None),
        (_pad_lanes(p["k_norm_w"][NOPE:]), None),
        (col_bcast(inv_freq), None),
        (p["conv_w"], None),
        (_pad_lanes(p["a_log"]), None),
        (_pad_lanes(p["dt_bias"]), None),
    ]
    prep_args = [a for a, _ in prep_in]
    prep_specs = [s if s is not None else _const_spec(a.shape) for a, s in prep_in]
    out_shapes = [
        jax.ShapeDtypeStruct((n_tiles, 2 * MIX, tm), BF16),
        jax.ShapeDtypeStruct((T, 2 * MIX), BF16),
        jax.ShapeDtypeStruct((n_tiles, MIX, tm), BF16),
        jax.ShapeDtypeStruct((T, MIX), BF16),
        jax.ShapeDtypeStruct((T, MIX), BF16),
        jax.ShapeDtypeStruct((T, MIX), BF16),
        jax.ShapeDtypeStruct((T, MIX), BF16),
        jax.ShapeDtypeStruct((T, LANES), F32),
        jax.ShapeDtypeStruct((T // CHUNK, 8, CHUNK), F32),
    ]
    tile_spec = lambda r: pl.BlockSpec((2, r, tm), lambda i: (i, 0, 0))
    out_specs = [tile_spec(2 * MIX), row_spec(2 * MIX), tile_spec(MIX)] + [row_spec(MIX)] * 4 + [
        row_spec(LANES),
        pl.BlockSpec((2 * tm // CHUNK, 8, CHUNK), lambda i: (i, 0, 0)),
    ]
    qt, k, vt, gq, gk, gv, gz, gb, rows = pl.pallas_call(
        functools.partial(_prep_kernel, tm=tm, tiles_per_seq=S // tm),
        grid=(n_tiles // 2,),
        in_specs=prep_specs,
        out_specs=out_specs,
        out_shape=out_shapes,
        scratch_shapes=[
            pltpu.VMEM((2, tm, 2 * LORA), F32),
            pltpu.VMEM((2, tm, 3 * MIX), F32),
            pltpu.VMEM((2, tm, LANES), F32),
            pltpu.VMEM((2, tm, LANES), F32),
            pltpu.VMEM((2, tm, MIX), BF16),
            pltpu.VMEM((8, 3 * MIX), F32),
            pltpu.VMEM((tm, tm), BF16),
        ],
        compiler_params=pltpu.CompilerParams(dimension_semantics=("arbitrary",),
                                             vmem_limit_bytes=VMEM_LIMIT),
        name="prep",
    )(*prep_args)

    tq = tm
    nq = S // tq
    n_sub = ATTN_SUB
    assert n_sub % 2 == 0 and nq % n_sub == 0
    n_steps = nq // n_sub
    mla_o = pl.pallas_call(
        functools.partial(_attn_kernel, tq=tq, n_sub=n_sub),
        grid=(B, n_steps),
        in_specs=[
            pl.BlockSpec((n_sub, 2 * MIX, tq), lambda b, i: (b * n_steps + i, 0, 0)),
            pl.BlockSpec((None, S, 2 * MIX), lambda b, i: (b, 0, 0)),
            pl.BlockSpec((None, nq, MIX, tq), lambda b, i: (b, 0, 0, 0)),
            _const_spec((N_HEADS, 1, HEAD)),
        ],
        out_specs=pl.BlockSpec((None, n_sub * tq, MIX), lambda b, i: (b, i, 0)),
        out_shape=jax.ShapeDtypeStruct((B, S, MIX), BF16),
        scratch_shapes=[pltpu.VMEM((n_sub * N_HEADS, 1, tq), F32),
                        pltpu.VMEM((n_sub * N_HEADS, 1, tq), F32),
                        pltpu.VMEM((n_sub * N_HEADS, HEAD, tq), F32),
                        pltpu.VMEM((2, n_sub * N_HEADS, tq, tq), F32)],
        compiler_params=pltpu.CompilerParams(
            dimension_semantics=("arbitrary", "arbitrary"),
            vmem_limit_bytes=VMEM_LIMIT),
        name="attn",
    )(qt, k.reshape(B, S, 2 * MIX), vt.reshape(B, nq, MIX, tq),
      p["mla_out_norm_w"].reshape(N_HEADS, 1, HEAD))

    n_chunks = S // CHUNK
    seq_spec = lambda w: pl.BlockSpec((None, S, w), lambda b: (b, 0, 0))
    gdn_o = pl.pallas_call(
        functools.partial(_gdn_kernel, n_chunks=n_chunks, unroll=GDN_UNROLL),
        grid=(B,),
        in_specs=[seq_spec(MIX)] * 4 + [
            seq_spec(LANES),
            pl.BlockSpec((None, n_chunks, 8, CHUNK), lambda b: (b, 0, 0, 0)),
            _const_spec((1, HEAD)),
        ],
        out_specs=seq_spec(MIX),
        out_shape=jax.ShapeDtypeStruct((B, S, MIX), BF16),
        scratch_shapes=[
            pltpu.VMEM((N_HEADS, HEAD, HEAD), F32),
            pltpu.VMEM((n_chunks * N_HEADS, CHUNK, HEAD), F32),
            pltpu.VMEM((n_chunks * N_HEADS, 2 * CHUNK, HEAD), BF16),
            pltpu.VMEM((n_chunks * N_HEADS, CHUNK, HEAD), BF16),
            pltpu.VMEM((n_chunks * N_HEADS, CHUNK, CHUNK), BF16),
        ],
        compiler_params=pltpu.CompilerParams(dimension_semantics=("arbitrary",),
                                             vmem_limit_bytes=VMEM_LIMIT),
        name="gdn",
    )(gq.reshape(B, S, MIX), gk.reshape(B, S, MIX), gv.reshape(B, S, MIX), gz.reshape(B, S, MIX),
      gb.reshape(B, S, LANES), rows.reshape(B, n_chunks, 8, CHUNK),
      p["gdn_norm_w"].reshape(1, HEAD))

    tm2 = MLP_TM
    w_out = p["w_out"].astype(BF16)
    tok_spec = lambda w: pl.BlockSpec((tm2, w), lambda i: (i, 0))
    out = pl.pallas_call(
        _mlp_kernel,
        grid=(T // tm2,),
        in_specs=[
            tok_spec(D_MODEL), tok_spec(MIX), tok_spec(MIX),
            _const_spec((MIX, D_MODEL)), _const_spec((MIX, D_MODEL)),
            _const_spec((1, D_MODEL)),
            _const_spec((D_MODEL, D_FF)), _const_spec((D_FF, D_MODEL)),
        ],
        out_specs=tok_spec(D_MODEL),
        out_shape=jax.ShapeDtypeStruct((T, D_MODEL), F32),
        compiler_params=pltpu.CompilerParams(dimension_semantics=("arbitrary",),
                                             vmem_limit_bytes=VMEM_LIMIT),
        name="mlp",
    )(x2, mla_o.reshape(T, MIX), gdn_o.reshape(T, MIX), w_out[:MIX], w_out[MIX:],
      p["mlp_norm_w"].reshape(1, D_MODEL), p["w_up"].astype(BF16), p["w_down"].astype(BF16))
    return out.reshape(B, S, D_MODEL)


def kernel(x, positions, attn_norm_w, w_in, q_lat_norm_w, w_uq, kv_lat_norm_w, w_ukv, q_norm_w,
           k_norm_w, mla_out_norm_w, conv_w, a_log, dt_bias, gdn_norm_w, w_out, mlp_norm_w, w_up,
           w_down):
    B, S, _ = x.shape
    half = ROPE // 2
    inv_freq = ROPE_THETA ** (-jnp.arange(half, dtype=F32) / half)
    params = dict(attn_norm_w=attn_norm_w, w_in=w_in, q_lat_norm_w=q_lat_norm_w, w_uq=w_uq,
                  kv_lat_norm_w=kv_lat_norm_w, w_ukv=w_ukv, q_norm_w=q_norm_w, k_norm_w=k_norm_w,
                  mla_out_norm_w=mla_out_norm_w, conv_w=conv_w, a_log=a_log, dt_bias=dt_bias,
                  gdn_norm_w=gdn_norm_w, w_out=w_out, mlp_norm_w=mlp_norm_w, w_up=w_up,
                  w_down=w_down)
    h = x
    for l in range(attn_norm_w.shape[0]):
        h = _layer(h, positions, inv_freq, {name: val[l] for name, val in params.items()})
    return h
```

```python
import functools

import jax
import jax.numpy as jnp
from jax import lax
from jax.experimental import pallas as pl
from jax.experimental.pallas import tpu as pltpu

F32 = jnp.float32
BF16 = jnp.bfloat16

D_MODEL = 1024
N_HEADS = 4
LORA = 256
NOPE = 128
ROPE = 64
HEAD = 128
QK_HEAD = NOPE + ROPE
ROPE_THETA = 10000.0
CONV_W = 4
CHUNK = 64
D_FF = 4 * D_MODEL
EPS = 1e-6
LOG2E = 1.4426950408889634
LANES = 128
MIX = N_HEADS * HEAD

PREP_TM = 256
GDN_UNROLL = 4
MLP_TM = 512
MLP_FC = 1024
VMEM_LIMIT = 56 * 1024 * 1024


def _mm(a, b):
    return jnp.dot(a.astype(BF16), b.astype(BF16), preferred_element_type=F32)


def _mm_nt(a, b):
    return lax.dot_general(a.astype(BF16), b.astype(BF16), (((1,), (1,)), ((), ())),
                           preferred_element_type=F32)


def _mm_tn(a, b):
    return lax.dot_general(a.astype(BF16), b.astype(BF16), (((0,), (0,)), ((), ())),
                           preferred_element_type=F32)


def _split3(x):
    hi = x.astype(BF16)
    r1 = x - hi.astype(F32)
    mid = r1.astype(BF16)
    lo = (r1 - mid.astype(F32)).astype(BF16)
    return hi, mid, lo


def _interleave(*gens):
    live = list(gens)
    while live:
        for g in list(live):
            try:
                next(g)
            except StopIteration:
                live.remove(g)


def _rms(x, w, n):
    return x * lax.rsqrt(jnp.sum(x * x, axis=-1, keepdims=True) * (1.0 / n) + EPS) * w


def _sigmoid(x):
    return 1.0 / (1.0 + jnp.exp(-x))


def _rope(x, cosf, sinf, lane):
    rot = jnp.where(lane < ROPE // 2, pltpu.roll(x, LANES - ROPE // 2, 1),
                    pltpu.roll(x, ROPE // 2, 1))
    return x * cosf + rot * sinf


def _prep_kernel(x0_ref, xa_ref, xb_ref, posr_ref, anw_ref, wlat_ref, wkpe_ref, wab_ref, wg_ref, wz_ref,
                 qlnw_ref, kvlnw_ref, wuqnt_ref, wuqpt_ref, wukk_ref, wukvt_ref,
                 qnwn_ref, qnwp_ref, knwn_ref, knwp_ref, invft_ref, convw_ref,
                 alog_ref, dtb_ref,
                 qt_out, k_out, vt_out, gq_out, gk_out, gv_out, z_out, gb_out, rows_out,
                 lat_s, g_s, kpe_s, ab_s, z_s, tail, tri_s, *, tm, tiles_per_seq):
    i = pl.program_id(0)
    half = ROPE // 2

    def front(x_ref, slot):
        xn = _rms(x_ref[...], anw_ref[...], D_MODEL).astype(BF16)
        yield
        lat_s[slot] = jnp.dot(xn, wlat_ref[...], preferred_element_type=F32)
        yield
        g_s[slot] = jnp.dot(xn, wg_ref[...], preferred_element_type=F32)
        yield
        kpe_s[slot] = jnp.dot(xn, wkpe_ref[...], preferred_element_type=F32)
        ab_s[slot] = jnp.dot(xn, wab_ref[...], preferred_element_type=F32)
        z_s[slot] = jnp.dot(xn, wz_ref[...], preferred_element_type=F32).astype(BF16)

    def back(slot, a):
        rows = slice(a * tm, (a + 1) * tm)
        z_out[rows, :] = z_s[slot]
        ang_t = invft_ref[...] * posr_ref[a].astype(F32)
        cos_t = jnp.cos(ang_t)
        sin_t = jnp.sin(ang_t)
        lane = lax.broadcasted_iota(jnp.int32, (tm, LANES), 1)
        table = jnp.concatenate([cos_t, sin_t, jnp.zeros((LANES - ROPE, tm), F32)], axis=0).T
        cosf = jnp.where(lane < half, table, jnp.where(lane < ROPE, pltpu.roll(table, half, 1), 0.0))
        sinf = jnp.where(lane < half, -pltpu.roll(table, LANES - half, 1),
                         jnp.where(lane < ROPE, table, 0.0))
        yield

        qn = _rms(lat_s[slot, :, :LORA], qlnw_ref[...], LORA)
        kvn = _rms(lat_s[slot, :, LORA:], kvlnw_ref[...], LORA)
        qn_t = qn.T.astype(BF16)
        kvn_t = kvn.T.astype(BF16)
        kvn = kvn.astype(BF16)
        qt_nope = jnp.dot(wuqnt_ref[...], qn_t, preferred_element_type=F32)
        qt_pe = jnp.dot(wuqpt_ref[...], qn_t, preferred_element_type=F32)
        vt_out[a] = jnp.dot(wukvt_ref[...], kvn_t, preferred_element_type=F32).astype(BF16)
        k_nope = jnp.dot(kvn, wukk_ref[...], preferred_element_type=F32)
        k_pe = _rope(_rms(kpe_s[slot], knwp_ref[...], ROPE), cosf, sinf, lane).astype(BF16)
        yield

        scale = QK_HEAD ** -0.5 * LOG2E
        qwn = qnwn_ref[...] * scale
        qwp = qnwp_ref[...] * scale
        for h in range(N_HEADS):
            lo = h * HEAD
            base = h * 2 * HEAD
            xn = qt_nope[lo:lo + NOPE, :]
            xn = xn * lax.rsqrt(jnp.sum(xn * xn, axis=0, keepdims=True) * (1.0 / NOPE) + EPS) * qwn
            xp = qt_pe[h * ROPE:(h + 1) * ROPE, :]
            xp = xp * lax.rsqrt(jnp.sum(xp * xp, axis=0, keepdims=True) * (1.0 / ROPE) + EPS) * qwp
            t1, t2 = xp[:half], xp[half:]
            qt_out[a, base:base + NOPE, :] = xn.astype(BF16)
            qt_out[a, base + NOPE:base + NOPE + half, :] = (t1 * cos_t - t2 * sin_t).astype(BF16)
            qt_out[a, base + NOPE + half:base + QK_HEAD, :] = (t2 * cos_t + t1 * sin_t).astype(BF16)
            qt_out[a, base + QK_HEAD:base + 2 * HEAD, :] = jnp.zeros((2 * HEAD - QK_HEAD, tm), BF16)
            k_out[rows, base:base + HEAD] = _rms(k_nope[:, lo:lo + HEAD], knwn_ref[...], NOPE).astype(BF16)
            k_out[rows, base + HEAD:base + 2 * HEAD] = k_pe
        yield

        g_all = g_s[slot]
        seq_start = (2 * i + a) % tiles_per_seq == 0
        prev = jnp.where(seq_start, 0.0, tail[...])
        row8 = lax.broadcasted_iota(jnp.int32, prev.shape, 0)
        conv = convw_ref[CONV_W - 1:CONV_W, :] * g_all
        for s in range(1, CONV_W):
            rolled = pltpu.roll(g_all, s, 0)
            first = jnp.where(row8 < s, pltpu.roll(prev, s, 0), rolled[:8])
            shifted = jnp.concatenate([first, rolled[8:]], axis=0)
            conv = conv + convw_ref[CONV_W - 1 - s:CONV_W - s, :] * shifted
        tail[...] = g_all[tm - 8:tm, :]
        yield
        half_conv = 0.5 * conv
        act = half_conv + half_conv * jnp.tanh(half_conv)
        for h in range(N_HEADS):
            lo = h * HEAD
            gq = act[:, lo:lo + HEAD]
            gk = act[:, MIX + lo:MIX + lo + HEAD]
            gq = gq * lax.rsqrt(jnp.sum(gq * gq, axis=-1, keepdims=True) + EPS) * (HEAD ** -0.5)
            gk = gk * lax.rsqrt(jnp.sum(gk * gk, axis=-1, keepdims=True) + EPS)
            gq_out[rows, lo:lo + HEAD] = gq.astype(BF16)
            gk_out[rows, lo:lo + HEAD] = gk.astype(BF16)
        gv_out[rows, :] = act[:, 2 * MIX:].astype(BF16)
        yield

        ab = ab_s[slot]
        sp_in = ab + dtb_ref[...]
        softplus = jnp.maximum(sp_in, 0.0) + jnp.log(1.0 + jnp.exp(-jnp.abs(sp_in)))
        g = -jnp.exp(alog_ref[...]) * softplus
        beta = _sigmoid(ab)
        tri = tri_s[...]
        g_hi, g_mid, g_lo = _split3(g)
        gcum = (jnp.dot(tri, g_hi, preferred_element_type=F32)
                + jnp.dot(tri, g_mid, preferred_element_type=F32)
                + jnp.dot(tri, g_lo, preferred_element_type=F32))
        gb = jnp.where(lane < N_HEADS, gcum, jnp.where(lane < 2 * N_HEADS, beta, 0.0))
        gb_out[rows, :] = gb
        er = lax.broadcasted_iota(jnp.int32, (8, LANES), 0)
        ec = lax.broadcasted_iota(jnp.int32, (8, LANES), 1)
        eye8 = jnp.where(er == ec, 1.0, 0.0).astype(BF16)
        for c in range(tm // CHUNK):
            parts = _split3(gb[c * CHUNK:(c + 1) * CHUNK, :])
            acc = None
            for p in parts:
                t = lax.dot_general(eye8, p, (((1,), (1,)), ((), ())), preferred_element_type=F32)
                acc = t if acc is None else acc + t
            rows_out[a * (tm // CHUNK) + c] = acc

    @pl.when(i == 0)
    def _():
        tail[...] = jnp.zeros(tail.shape, F32)
        rr = lax.broadcasted_iota(jnp.int32, (tm, tm), 0)
        cc = lax.broadcasted_iota(jnp.int32, (tm, tm), 1)
        tri_s[...] = jnp.where((cc <= rr) & ((cc // CHUNK) == (rr // CHUNK)), 1.0, 0.0).astype(BF16)
        _interleave(front(x0_ref, 0))

    _interleave(front(xa_ref, 1), back(0, 0))
    _interleave(front(xb_ref, 0), back(1, 1))


def _attend(pairs, qt_ref, k_ref, vt_ref, m_s, l_s, acc_s, st_s, *, tq, nq):
    heads = range(N_HEADS)
    steps = nq + 1
    kk = lax.broadcasted_iota(jnp.int32, (tq, tq), 0)
    qq = lax.broadcasted_iota(jnp.int32, (tq, tq), 1)
    ones = jnp.ones((16, tq), BF16)
    hk = lambda h: slice(2 * HEAD * h, 2 * HEAD * (h + 1))
    hv = lambda h: slice(HEAD * h, HEAD * (h + 1))

    def blocks(s):
        return [(jnp.where(s <= p, p, nq - 1 - p), jnp.where(s <= p, p - s, s - p - 1)) for p in pairs]

    def scores(s, c, h, q, kv):
        r0 = pl.multiple_of(kv * tq, tq)
        st_s[s % 2, c * N_HEADS + h] = jnp.dot(k_ref[pl.ds(r0, tq), hk(h)], qt_ref[q, hk(h), :],
                                               preferred_element_type=F32)

    for c, (q, kv) in enumerate(blocks(0)):
        for h in heads:
            scores(0, c, h, q, kv)
    yield
    for s in range(steps):
        nxt = blocks(s + 1) if s + 1 < steps else None
        for c, (q, kv) in enumerate(blocks(s)):
            for h in heads:
                if nxt is not None:
                    scores(s + 1, c, h, *nxt[c])
                i = q * N_HEADS + h
                st = st_s[s % 2, c * N_HEADS + h]
                if s in (0, steps - 1):
                    st = jnp.where(kk <= qq, st, -jnp.inf)
                m_old = m_s[i]
                m_new = jnp.maximum(m_old, jnp.max(st, axis=0, keepdims=True))
                alpha = jnp.exp2(m_old - m_new)
                p_t = jnp.exp2(st - m_new).astype(BF16)
                pv = jnp.dot(jnp.concatenate([vt_ref[kv, hv(h), :], ones], axis=0), p_t,
                             preferred_element_type=F32)
                m_s[i] = m_new
                l_s[i] = alpha * l_s[i] + pv[HEAD:HEAD + 1]
                acc_s[i] = alpha * acc_s[i] + pv[:HEAD]
            yield


def _attn_kernel(qt_ref, k_ref, vt_ref, w_ref, o_ref, m_s, l_s, acc_s, st_s, *, tq, nq):
    m_s[...] = jnp.full(m_s.shape, -jnp.inf, F32)
    l_s[...] = jnp.zeros(l_s.shape, F32)
    acc_s[...] = jnp.zeros(acc_s.shape, F32)
    n_pairs = nq // 2

    def body(t, carry):
        _interleave(_attend([t, n_pairs - 1 - t], qt_ref, k_ref, vt_ref, m_s, l_s, acc_s, st_s,
                            tq=tq, nq=nq))
        return carry

    lax.fori_loop(0, n_pairs // 2, body, 0)
    rr = lax.broadcasted_iota(jnp.int32, (tq, tq), 0)
    cc = lax.broadcasted_iota(jnp.int32, (tq, tq), 1)
    eye = jnp.where(rr == cc, 1.0, 0.0).astype(BF16)
    for q in range(nq):
        for h in range(N_HEADS):
            i = q * N_HEADS + h
            o_t = acc_s[i] / l_s[i]
            o_t = o_t * lax.rsqrt(jnp.sum(o_t * o_t, axis=0, keepdims=True) * (1.0 / HEAD) + EPS) * w_ref[h]
            o = lax.dot_general(eye, o_t.astype(BF16), (((1,), (1,)), ((), ())),
                                preferred_element_type=F32)
            o_ref[q * tq:(q + 1) * tq, HEAD * h:HEAD * (h + 1)] = o.astype(BF16)


def _gdn_kernel(q_ref, k_ref, v_ref, z_ref, gb_ref, rows_ref, gnw_ref, o_ref,
                s_ref, u_s, wq_s, kd_s, at_s, *, n_chunks, unroll):
    ii = lax.broadcasted_iota(jnp.int32, (CHUNK, CHUNK), 0)
    jj = lax.broadcasted_iota(jnp.int32, (CHUNK, CHUNK), 1)
    eye = jnp.where(ii == jj, 1.0, 0.0)
    heads = range(N_HEADS)

    def solve(t):
        ns = [t * unroll + c for c in range(unroll)]
        r0 = [pl.multiple_of(n * CHUNK, CHUNK) for n in ns]
        scal = [gb_ref[pl.ds(r, CHUNK), :] for r in r0]
        rows = [rows_ref[n] for n in ns]
        items = [(c, h) for c in range(unroll) for h in heads]
        cols = lambda h: slice(h * HEAD, (h + 1) * HEAD)
        q = [q_ref[pl.ds(r0[c], CHUNK), cols(h)] for c, h in items]
        k = [k_ref[pl.ds(r0[c], CHUNK), cols(h)] for c, h in items]
        v = [v_ref[pl.ds(r0[c], CHUNK), cols(h)] for c, h in items]
        g_col = [jnp.broadcast_to(scal[c][:, h:h + 1], (CHUNK, LANES)) for c, h in items]
        b_col = [jnp.broadcast_to(scal[c][:, N_HEADS + h:N_HEADS + h + 1], (CHUNK, LANES))
                 for c, h in items]
        g_row = [rows[c][h:h + 1, :] for c, h in items]
        n_it = range(len(items))
        decay = [jnp.exp(jnp.where(ii >= jj, g_col[x][:, :CHUNK] - g_row[x], -jnp.inf)) for x in n_it]
        qkk = [_mm_nt(jnp.concatenate([q[x], k[x]], axis=0), k[x]) for x in n_it]
        yield
        lmat = [jnp.where(ii > jj, b_col[x][:, :CHUNK] * qkk[x][CHUNK:] * decay[x], 0.0) for x in n_it]
        inv = [eye - lmat[x] for x in n_it]
        pw = [_mm(lmat[x], lmat[x]) for x in n_it]
        yield
        for step in range(4):
            both = [_mm(jnp.concatenate([inv[x], pw[x]], axis=0), pw[x]) for x in n_it]
            inv = [inv[x] + both[x][:CHUNK] for x in n_it]
            pw = [both[x][CHUNK:] for x in n_it]
            yield
        inv = [inv[x] + _mm(inv[x], pw[x]) for x in n_it]
        yield
        e_col = [jnp.exp(g_col[x]) for x in n_it]
        rhs = [jnp.concatenate([v[x].astype(F32) * b_col[x], k[x].astype(F32) * (b_col[x] * e_col[x])],
                               axis=1) for x in n_it]
        uw = [_mm(inv[x], rhs[x]) for x in n_it]
        yield
        for x, (c, h) in enumerate(items):
            idx = ns[c] * N_HEADS + h
            g_last = g_col[x][CHUNK - 1:CHUNK, :]
            u_s[idx] = uw[x][:, :HEAD]
            wq_s[idx, 0:CHUNK, :] = uw[x][:, HEAD:].astype(BF16)
            wq_s[idx, CHUNK:2 * CHUNK, :] = (q[x].astype(F32) * e_col[x]).astype(BF16)
            kd_s[idx] = (k[x].astype(F32) * jnp.exp(g_last - g_col[x])).astype(BF16)
            at_s[idx] = (qkk[x][:CHUNK] * decay[x]).astype(BF16)

    gnw = gnw_ref[...]

    def scan(t):
        for c in range(unroll):
            n = t * unroll + c
            r0 = pl.multiple_of(n * CHUNK, CHUNK)
            g_last = gb_ref[pl.ds(r0 + CHUNK - 1, 1), :]
            state = [s_ref[h] for h in heads]
            sb = [x.astype(BF16) for x in state]
            ws = [_mm(wq_s[n * N_HEADS + h], sb[h]) for h in heads]
            yield
            v_new = [(u_s[n * N_HEADS + h] - ws[h][:CHUNK]).astype(BF16) for h in heads]
            o = [ws[h][CHUNK:] + _mm(at_s[n * N_HEADS + h], v_new[h]) for h in heads]
            upd = [_mm_tn(kd_s[n * N_HEADS + h], v_new[h]) for h in heads]
            yield
            for h in heads:
                c_dec = jnp.exp(jnp.broadcast_to(g_last[:, h:h + 1], (HEAD, HEAD)))
                s_ref[h] = state[h] * c_dec + upd[h]
                zh = z_ref[pl.ds(r0, CHUNK), h * HEAD:(h + 1) * HEAD].astype(F32)
                gated = _rms(o[h], gnw, HEAD) * (zh * _sigmoid(zh))
                o_ref[pl.ds(r0, CHUNK), h * HEAD:(h + 1) * HEAD] = gated.astype(BF16)

    n_groups = n_chunks // unroll
    s_ref[...] = jnp.zeros_like(s_ref)
    _interleave(solve(0))

    def body(t, carry):
        _interleave(solve(t), scan(t - 1))
        return carry

    lax.fori_loop(1, n_groups, body, 0)
    _interleave(scan(n_groups - 1))


def _mlp_kernel(x_ref, mla_ref, gdn_ref, woa_ref, wob_ref, nw_ref, wup_ref, wdn_ref, o_ref):
    h = (x_ref[...]
         + jnp.dot(mla_ref[...], woa_ref[...], preferred_element_type=F32)
         + jnp.dot(gdn_ref[...], wob_ref[...], preferred_element_type=F32))
    hn = _rms(h, nw_ref[...], D_MODEL).astype(BF16)
    o_ref[...] = h
    for c in range(D_FF // MLP_FC):
        u = jnp.dot(hn, wup_ref[:, c * MLP_FC:(c + 1) * MLP_FC], preferred_element_type=F32)
        a = jnp.square(jnp.maximum(u, 0.0)).astype(BF16)
        o_ref[...] += jnp.dot(a, wdn_ref[c * MLP_FC:(c + 1) * MLP_FC, :], preferred_element_type=F32)


def _const_spec(shape):
    nd = len(shape)
    return pl.BlockSpec(shape, lambda *_: (0,) * nd, pipeline_mode=pl.Buffered(1))


def _pad_lanes(v, width=LANES):
    v = v.reshape(1, -1).astype(F32)
    return jnp.pad(v, ((0, 0), (0, width - v.shape[1])))


def _layer(h, pos, inv_freq, p):
    B, S, _ = h.shape
    T = B * S
    x2 = h.reshape(T, D_MODEL)

    w_in = p["w_in"]
    o_q, o_kv, o_pe = 0, LORA, 2 * LORA
    o_g = o_pe + ROPE
    o_z = o_g + 3 * MIX
    o_a = o_z + MIX
    w_lat = w_in[:, o_q:o_pe].astype(BF16)
    w_kpe = jnp.pad(w_in[:, o_pe:o_g], ((0, 0), (0, LANES - ROPE))).astype(BF16)
    w_g = w_in[:, o_g:o_z].astype(BF16)
    w_z = w_in[:, o_z:o_a].astype(BF16)
    w_ab = jnp.pad(w_in[:, o_a:], ((0, 0), (0, LANES - 2 * N_HEADS))).astype(BF16)
    w_uq = p["w_uq"].reshape(LORA, N_HEADS, QK_HEAD)
    w_uq_nt = w_uq[:, :, :NOPE].reshape(LORA, N_HEADS * NOPE).T.astype(BF16)
    w_uq_pt = w_uq[:, :, NOPE:].reshape(LORA, N_HEADS * ROPE).T.astype(BF16)
    w_ukv = p["w_ukv"].reshape(LORA, N_HEADS, NOPE + HEAD)
    w_uk_k = w_ukv[:, :, :NOPE].reshape(LORA, N_HEADS * NOPE).astype(BF16)
    w_uk_vt = w_ukv[:, :, NOPE:].reshape(LORA, N_HEADS * HEAD).T.astype(BF16)

    tm = PREP_TM
    n_tiles = T // tm
    assert n_tiles % 2 == 0
    row_spec = lambda w: pl.BlockSpec((2 * tm, w), lambda i: (i, 0))
    col_bcast = lambda v: jnp.broadcast_to(v.astype(F32)[:, None], (v.shape[0], tm))
    x_tile = lambda index: pl.BlockSpec((tm, D_MODEL), index)
    prep_in = [
        (x2, x_tile(lambda i: (0, 0))),
        (x2, x_tile(lambda i: (2 * i + 1, 0))),
        (x2, x_tile(lambda i: (jnp.minimum(2 * i + 2, n_tiles - 1), 0))),
        (pos.reshape(n_tiles, 1, tm), pl.BlockSpec((2, 1, tm), lambda i: (i, 0, 0))),
        (p["attn_norm_w"].reshape(1, D_MODEL), None),
        (w_lat, None), (w_kpe, None), (w_ab, None), (w_g, None), (w_z, None),
        (p["q_lat_norm_w"].reshape(1, LORA), None),
        (p["kv_lat_norm_w"].reshape(1, LORA), None),
        (w_uq_nt, None), (w_uq_pt, None), (w_uk_k, None), (w_uk_vt, None),
        (col_bcast(p["q_norm_w"][:NOPE]), None),
        (col_bcast(p["q_norm_w"][NOPE:]), None),
        (p["k_norm_w"][:NOPE].reshape(1, NOPE), None),
        (_pad_lanes(p["k_norm_w"][NOPE:]), None),
        (col_bcast(inv_freq), None),
        (p["conv_w"], None),
        (_pad_lanes(p["a_log"]), None),
        (_pad_lanes(p["dt_bias"]), None),
    ]
    prep_args = [a for a, _ in prep_in]
    prep_specs = [s if s is not None else _const_spec(a.shape) for a, s in prep_in]
    out_shapes = [
        jax.ShapeDtypeStruct((n_tiles, 2 * MIX, tm), BF16),
        jax.ShapeDtypeStruct((T, 2 * MIX), BF16),
        jax.ShapeDtypeStruct((n_tiles, MIX, tm), BF16),
        jax.ShapeDtypeStruct((T, MIX), BF16),
        jax.ShapeDtypeStruct((T, MIX), BF16),
        jax.ShapeDtypeStruct((T, MIX), BF16),
        jax.ShapeDtypeStruct((T, MIX), BF16),
        jax.ShapeDtypeStruct((T, LANES), F32),
        jax.ShapeDtypeStruct((T // CHUNK, 8, CHUNK), F32),
    ]
    tile_spec = lambda r: pl.BlockSpec((2, r, tm), lambda i: (i, 0, 0))
    out_specs = [tile_spec(2 * MIX), row_spec(2 * MIX), tile_spec(MIX)] + [row_spec(MIX)] * 4 + [
        row_spec(LANES),
        pl.BlockSpec((2 * tm // CHUNK, 8, CHUNK), lambda i: (i, 0, 0)),
    ]
    qt, k, vt, gq, gk, gv, gz, gb, rows = pl.pallas_call(
        functools.partial(_prep_kernel, tm=tm, tiles_per_seq=S // tm),
        grid=(n_tiles // 2,),
        in_specs=prep_specs,
        out_specs=out_specs,
        out_shape=out_shapes,
        scratch_shapes=[
            pltpu.VMEM((2, tm, 2 * LORA), F32),
            pltpu.VMEM((2, tm, 3 * MIX), F32),
            pltpu.VMEM((2, tm, LANES), F32),
            pltpu.VMEM((2, tm, LANES), F32),
            pltpu.VMEM((2, tm, MIX), BF16),
            pltpu.VMEM((8, 3 * MIX), F32),
            pltpu.VMEM((tm, tm), BF16),
        ],
        compiler_params=pltpu.CompilerParams(dimension_semantics=("arbitrary",),
                                             vmem_limit_bytes=VMEM_LIMIT),
        name="prep",
    )(*prep_args)

    tq = tm
    nq = S // tq
    assert nq % 4 == 0
    mla_o = pl.pallas_call(
        functools.partial(_attn_kernel, tq=tq, nq=nq),
        grid=(B,),
        in_specs=[
            pl.BlockSpec((nq, 2 * MIX, tq), lambda b: (b, 0, 0)),
            pl.BlockSpec((None, S, 2 * MIX), lambda b: (b, 0, 0)),
            pl.BlockSpec((None, nq, MIX, tq), lambda b: (b, 0, 0, 0)),
            _const_spec((N_HEADS, HEAD, tq)),
        ],
        out_specs=pl.BlockSpec((None, S, MIX), lambda b: (b, 0, 0)),
        out_shape=jax.ShapeDtypeStruct((B, S, MIX), BF16),
        scratch_shapes=[pltpu.VMEM((nq * N_HEADS, 1, tq), F32),
                        pltpu.VMEM((nq * N_HEADS, 1, tq), F32),
                        pltpu.VMEM((nq * N_HEADS, HEAD, tq), F32),
                        pltpu.VMEM((2, 2 * N_HEADS, tq, tq), F32)],
        compiler_params=pltpu.CompilerParams(
            dimension_semantics=("arbitrary",),
            vmem_limit_bytes=VMEM_LIMIT),
        name="attn",
    )(qt, k.reshape(B, S, 2 * MIX), vt.reshape(B, nq, MIX, tq),
      jnp.broadcast_to(p["mla_out_norm_w"].astype(F32)[:, :, None], (N_HEADS, HEAD, tq)))

    n_chunks = S // CHUNK
    seq_spec = lambda w: pl.BlockSpec((None, S, w), lambda b: (b, 0, 0))
    gdn_o = pl.pallas_call(
        functools.partial(_gdn_kernel, n_chunks=n_chunks, unroll=GDN_UNROLL),
        grid=(B,),
        in_specs=[seq_spec(MIX)] * 4 + [
            seq_spec(LANES),
            pl.BlockSpec((None, n_chunks, 8, CHUNK), lambda b: (b, 0, 0, 0)),
            _const_spec((1, HEAD)),
        ],
        out_specs=seq_spec(MIX),
        out_shape=jax.ShapeDtypeStruct((B, S, MIX), BF16),
        scratch_shapes=[
            pltpu.VMEM((N_HEADS, HEAD, HEAD), F32),
            pltpu.VMEM((n_chunks * N_HEADS, CHUNK, HEAD), F32),
            pltpu.VMEM((n_chunks * N_HEADS, 2 * CHUNK, HEAD), BF16),
            pltpu.VMEM((n_chunks * N_HEADS, CHUNK, HEAD), BF16),
            pltpu.VMEM((n_chunks * N_HEADS, CHUNK, CHUNK), BF16),
        ],
        compiler_params=pltpu.CompilerParams(dimension_semantics=("arbitrary",),
                                             vmem_limit_bytes=VMEM_LIMIT),
        name="gdn",
    )(gq.reshape(B, S, MIX), gk.reshape(B, S, MIX), gv.reshape(B, S, MIX), gz.reshape(B, S, MIX),
      gb.reshape(B, S, LANES), rows.reshape(B, n_chunks, 8, CHUNK),
      p["gdn_norm_w"].reshape(1, HEAD))

    tm2 = MLP_TM
    w_out = p["w_out"].astype(BF16)
    tok_spec = lambda w: pl.BlockSpec((tm2, w), lambda i: (i, 0))
    out = pl.pallas_call(
        _mlp_kernel,
        grid=(T // tm2,),
        in_specs=[
            tok_spec(D_MODEL), tok_spec(MIX), tok_spec(MIX),
            _const_spec((MIX, D_MODEL)), _const_spec((MIX, D_MODEL)),
            _const_spec((1, D_MODEL)),
            _const_spec((D_MODEL, D_FF)), _const_spec((D_FF, D_MODEL)),
        ],
        out_specs=tok_spec(D_MODEL),
        out_shape=jax.ShapeDtypeStruct((T, D_MODEL), F32),
        compiler_params=pltpu.CompilerParams(dimension_semantics=("arbitrary",),
                                             vmem_limit_bytes=VMEM_LIMIT),
        name="mlp",
    )(x2, mla_o.reshape(T, MIX), gdn_o.reshape(T, MIX), w_out[:MIX], w_out[MIX:],
      p["mlp_norm_w"].reshape(1, D_MODEL), p["w_up"].astype(BF16), p["w_down"].astype(BF16))
    return out.reshape(B, S, D_MODEL)


def kernel(x, positions, attn_norm_w, w_in, q_lat_norm_w, w_uq, kv_lat_norm_w, w_ukv, q_norm_w,
           k_norm_w, mla_out_norm_w, conv_w, a_log, dt_bias, gdn_norm_w, w_out, mlp_norm_w, w_up,
           w_down):
    B, S, _ = x.shape
    half = ROPE // 2
    inv_freq = ROPE_THETA ** (-jnp.arange(half, dtype=F32) / half)
    params = dict(attn_norm_w=attn_norm_w, w_in=w_in, q_lat_norm_w=q_lat_norm_w, w_uq=w_uq,
                  kv_lat_norm_w=kv_lat_norm_w, w_ukv=w_ukv, q_norm_w=q_norm_w, k_norm_w=k_norm_w,
                  mla_out_norm_w=mla_out_norm_w, conv_w=conv_w, a_log=a_log, dt_bias=dt_bias,
                  gdn_norm_w=gdn_norm_w, w_out=w_out, mlp_norm_w=mlp_norm_w, w_up=w_up,
                  w_down=w_down)
    h = x
    for l in range(attn_norm_w.shape[0]):
        h = _layer(h, positions, inv_freq, {name: val[l] for name, val in params.items()})
    return h
```

```python
import functools

import jax
import jax.numpy as jnp
from jax import lax
from jax.experimental import pallas as pl
from jax.experimental.pallas import tpu as pltpu

F32 = jnp.float32
BF16 = jnp.bfloat16

D_MODEL = 1024
N_HEADS = 4
LORA = 256
NOPE = 128
ROPE = 64
HEAD = 128
QK_HEAD = NOPE + ROPE
ROPE_THETA = 10000.0
CONV_W = 4
CHUNK = 64
D_FF = 4 * D_MODEL
EPS = 1e-6
LOG2E = 1.4426950408889634
LANES = 128
MIX = N_HEADS * HEAD

PREP_TM = 256
GDN_UNROLL = 4
GDN_BATCHES = 1
GDN_SEGMENTS = 1
MLP_TM = 512
MLP_FC = 1024
VMEM_LIMIT = 56 * 1024 * 1024


def _mm(a, b):
    return jnp.dot(a.astype(BF16), b.astype(BF16), preferred_element_type=F32)


def _mm_nt(a, b):
    return lax.dot_general(a.astype(BF16), b.astype(BF16), (((1,), (1,)), ((), ())),
                           preferred_element_type=F32)


def _mm_tn(a, b):
    return lax.dot_general(a.astype(BF16), b.astype(BF16), (((0,), (0,)), ((), ())),
                           preferred_element_type=F32)


def _split3(x):
    hi = x.astype(BF16)
    r1 = x - hi.astype(F32)
    mid = r1.astype(BF16)
    lo = (r1 - mid.astype(F32)).astype(BF16)
    return hi, mid, lo


def _interleave(*gens):
    live = list(gens)
    while live:
        for g in list(live):
            try:
                next(g)
            except StopIteration:
                live.remove(g)


def _rms(x, w, n):
    return x * lax.rsqrt(jnp.sum(x * x, axis=-1, keepdims=True) * (1.0 / n) + EPS) * w


def _sigmoid(x):
    return 1.0 / (1.0 + jnp.exp(-x))


def _rope(x, cosf, sinf, lane):
    rot = jnp.where(lane < ROPE // 2, pltpu.roll(x, LANES - ROPE // 2, 1),
                    pltpu.roll(x, ROPE // 2, 1))
    return x * cosf + rot * sinf


def _prep_kernel(x0_ref, xa_ref, xb_ref, posr_ref, anw_ref, wlat_ref, wkpe_ref, wab_ref, wg_ref, wz_ref,
                 qlnw_ref, kvlnw_ref, wuqnt_ref, wuqpt_ref, wukk_ref, wukvt_ref,
                 qnwn_ref, qnwp_ref, knwn_ref, knwp_ref, invft_ref, convw_ref,
                 alog_ref, dtb_ref,
                 qt_out, k_out, vt_out, gq_out, gk_out, gv_out, z_out, gb_out, rows_out,
                 lat_s, g_s, kpe_s, ab_s, z_s, tail, tri_s, *, tm, tiles_per_seq):
    i = pl.program_id(0)
    half = ROPE // 2

    def front(x_ref, slot):
        xn = _rms(x_ref[...], anw_ref[...], D_MODEL).astype(BF16)
        yield
        lat_s[slot] = jnp.dot(xn, wlat_ref[...], preferred_element_type=F32)
        yield
        g_s[slot] = jnp.dot(xn, wg_ref[...], preferred_element_type=F32)
        yield
        kpe_s[slot] = jnp.dot(xn, wkpe_ref[...], preferred_element_type=F32)
        ab_s[slot] = jnp.dot(xn, wab_ref[...], preferred_element_type=F32)
        z_s[slot] = jnp.dot(xn, wz_ref[...], preferred_element_type=F32).astype(BF16)

    def back(slot, sub, a):
        rows = slice(a * tm, (a + 1) * tm)
        src = slice(sub * tm, (sub + 1) * tm)
        z_out[rows, :] = z_s[slot, src, :]
        ang_t = invft_ref[...] * posr_ref[a].astype(F32)
        cos_t = jnp.cos(ang_t)
        sin_t = jnp.sin(ang_t)
        lane = lax.broadcasted_iota(jnp.int32, (tm, LANES), 1)
        table = jnp.concatenate([cos_t, sin_t, jnp.zeros((LANES - ROPE, tm), F32)], axis=0).T
        cosf = jnp.where(lane < half, table, jnp.where(lane < ROPE, pltpu.roll(table, half, 1), 0.0))
        sinf = jnp.where(lane < half, -pltpu.roll(table, LANES - half, 1),
                         jnp.where(lane < ROPE, table, 0.0))
        yield

        qn = _rms(lat_s[slot, src, :LORA], qlnw_ref[...], LORA)
        kvn = _rms(lat_s[slot, src, LORA:], kvlnw_ref[...], LORA)
        qn_t = qn.T.astype(BF16)
        kvn_t = kvn.T.astype(BF16)
        kvn = kvn.astype(BF16)
        qt_nope = jnp.dot(wuqnt_ref[...], qn_t, preferred_element_type=F32)
        qt_pe = jnp.dot(wuqpt_ref[...], qn_t, preferred_element_type=F32)
        vt_out[a] = jnp.dot(wukvt_ref[...], kvn_t, preferred_element_type=F32).astype(BF16)
        k_nope = jnp.dot(kvn, wukk_ref[...], preferred_element_type=F32)
        k_pe = _rope(_rms(kpe_s[slot, src, :], knwp_ref[...], ROPE), cosf, sinf, lane).astype(BF16)
        yield

        scale = QK_HEAD ** -0.5 * LOG2E
        qwn = qnwn_ref[...] * scale
        qwp = qnwp_ref[...] * scale
        for h in range(N_HEADS):
            lo = h * HEAD
            base = h * 2 * HEAD
            xn = qt_nope[lo:lo + NOPE, :]
            xn = xn * lax.rsqrt(jnp.sum(xn * xn, axis=0, keepdims=True) * (1.0 / NOPE) + EPS) * qwn
            xp = qt_pe[h * ROPE:(h + 1) * ROPE, :]
            xp = xp * lax.rsqrt(jnp.sum(xp * xp, axis=0, keepdims=True) * (1.0 / ROPE) + EPS) * qwp
            t1, t2 = xp[:half], xp[half:]
            qt_out[a, base:base + NOPE, :] = xn.astype(BF16)
            qt_out[a, base + NOPE:base + NOPE + half, :] = (t1 * cos_t - t2 * sin_t).astype(BF16)
            qt_out[a, base + NOPE + half:base + QK_HEAD, :] = (t2 * cos_t + t1 * sin_t).astype(BF16)
            qt_out[a, base + QK_HEAD:base + 2 * HEAD, :] = jnp.zeros((2 * HEAD - QK_HEAD, tm), BF16)
            k_out[rows, base:base + HEAD] = _rms(k_nope[:, lo:lo + HEAD], knwn_ref[...], NOPE).astype(BF16)
            k_out[rows, base + HEAD:base + 2 * HEAD] = k_pe
        yield

        seq_start = (4 * i + a) % tiles_per_seq == 0
        row8 = lax.broadcasted_iota(jnp.int32, (8, HEAD), 0)
        outs = ((gq_out, HEAD ** -0.5), (gk_out, 1.0), (gv_out, None))
        for part, (dst, norm_scale) in enumerate(outs):
            for h in range(N_HEADS):
                cols = slice(part * MIX + h * HEAD, part * MIX + (h + 1) * HEAD)
                g_blk = g_s[slot, src, cols]
                prev = jnp.where(seq_start, 0.0, tail[:, cols])
                conv = convw_ref[CONV_W - 1:CONV_W, cols] * g_blk
                for s in range(1, CONV_W):
                    rolled = pltpu.roll(g_blk, s, 0)
                    first = jnp.where(row8 < s, pltpu.roll(prev, s, 0), rolled[:8])
                    shifted = jnp.concatenate([first, rolled[8:]], axis=0)
                    conv = conv + convw_ref[CONV_W - 1 - s:CONV_W - s, cols] * shifted
                tail[:, cols] = g_blk[tm - 8:tm, :]
                half_conv = 0.5 * conv
                act = half_conv + half_conv * jnp.tanh(half_conv)
                if norm_scale is not None:
                    act = act * (lax.rsqrt(jnp.sum(act * act, axis=-1, keepdims=True) + EPS) * norm_scale)
                dst[rows, h * HEAD:(h + 1) * HEAD] = act.astype(BF16)
            yield

        ab = ab_s[slot, src, :]
        sp_in = ab + dtb_ref[...]
        softplus = jnp.maximum(sp_in, 0.0) + jnp.log(1.0 + jnp.exp(-jnp.abs(sp_in)))
        g = -jnp.exp(alog_ref[...]) * softplus
        beta = _sigmoid(ab)
        tri = tri_s[...]
        g_hi, g_mid, g_lo = _split3(g)
        gcum = (jnp.dot(tri, g_hi, preferred_element_type=F32)
                + jnp.dot(tri, g_mid, preferred_element_type=F32)
                + jnp.dot(tri, g_lo, preferred_element_type=F32))
        gb = jnp.where(lane < N_HEADS, gcum, jnp.where(lane < 2 * N_HEADS, beta, 0.0))
        gb_out[rows, :] = gb
        er = lax.broadcasted_iota(jnp.int32, (8, LANES), 0)
        ec = lax.broadcasted_iota(jnp.int32, (8, LANES), 1)
        eye8 = jnp.where(er == ec, 1.0, 0.0).astype(BF16)
        for c in range(tm // CHUNK):
            parts = _split3(gb[c * CHUNK:(c + 1) * CHUNK, :])
            acc = None
            for p in parts:
                t = lax.dot_general(eye8, p, (((1,), (1,)), ((), ())), preferred_element_type=F32)
                acc = t if acc is None else acc + t
            rows_out[a * (tm // CHUNK) + c] = acc

    @pl.when(i == 0)
    def _():
        tail[...] = jnp.zeros(tail.shape, F32)
        rr = lax.broadcasted_iota(jnp.int32, (tm, tm), 0)
        cc = lax.broadcasted_iota(jnp.int32, (tm, tm), 1)
        tri_s[...] = jnp.where((cc <= rr) & ((cc // CHUNK) == (rr // CHUNK)), 1.0, 0.0).astype(BF16)
        _interleave(front(x0_ref, 0))

    def backs(slot, first_tile):
        yield from back(slot, 0, first_tile)
        yield from back(slot, 1, first_tile + 1)

    _interleave(front(xa_ref, 1), backs(0, 0))
    _interleave(front(xb_ref, 0), backs(1, 2))


def _attend(pairs, qt_ref, k_ref, vt_ref, m_s, l_s, acc_s, st_s, *, tq, nq):
    heads = range(N_HEADS)
    steps = nq + 1
    kk = lax.broadcasted_iota(jnp.int32, (tq, tq), 0)
    qq = lax.broadcasted_iota(jnp.int32, (tq, tq), 1)
    ones = jnp.ones((16, tq), BF16)
    hk = lambda h: slice(2 * HEAD * h, 2 * HEAD * (h + 1))
    hv = lambda h: slice(HEAD * h, HEAD * (h + 1))

    def blocks(s):
        return [(jnp.where(s <= p, p, nq - 1 - p), jnp.where(s <= p, p - s, s - p - 1)) for p in pairs]

    def scores(s, c, h, q, kv):
        r0 = pl.multiple_of(kv * tq, tq)
        st_s[s % 2, c * N_HEADS + h] = jnp.dot(k_ref[pl.ds(r0, tq), hk(h)], qt_ref[q, hk(h), :],
                                               preferred_element_type=F32)

    for c, (q, kv) in enumerate(blocks(0)):
        for h in heads:
            scores(0, c, h, q, kv)
    yield
    for s in range(steps):
        nxt = blocks(s + 1) if s + 1 < steps else None
        for c, (q, kv) in enumerate(blocks(s)):
            for h in heads:
                if nxt is not None:
                    scores(s + 1, c, h, *nxt[c])
                i = q * N_HEADS + h
                st = st_s[s % 2, c * N_HEADS + h]
                if s in (0, steps - 1):
                    st = jnp.where(kk <= qq, st, -jnp.inf)
                m_old = m_s[i]
                m_new = jnp.maximum(m_old, jnp.max(st, axis=0, keepdims=True))
                alpha = jnp.exp2(m_old - m_new)
                p_t = jnp.exp2(st - m_new).astype(BF16)
                pv = jnp.dot(jnp.concatenate([vt_ref[kv, hv(h), :], ones], axis=0), p_t,
                             preferred_element_type=F32)
                m_s[i] = m_new
                l_s[i] = alpha * l_s[i] + pv[HEAD:HEAD + 1]
                acc_s[i] = alpha * acc_s[i] + pv[:HEAD]
            yield


def _attn_kernel(qt_ref, k_ref, vt_ref, w_ref, o_ref, m_s, l_s, acc_s, st_s, *, tq, nq):
    m_s[...] = jnp.full(m_s.shape, -jnp.inf, F32)
    l_s[...] = jnp.zeros(l_s.shape, F32)
    acc_s[...] = jnp.zeros(acc_s.shape, F32)
    n_pairs = nq // 2

    def body(t, carry):
        _interleave(_attend([t, n_pairs - 1 - t], qt_ref, k_ref, vt_ref, m_s, l_s, acc_s, st_s,
                            tq=tq, nq=nq))
        return carry

    lax.fori_loop(0, n_pairs // 2, body, 0)
    rr = lax.broadcasted_iota(jnp.int32, (tq, tq), 0)
    cc = lax.broadcasted_iota(jnp.int32, (tq, tq), 1)
    eye = jnp.where(rr == cc, 1.0, 0.0).astype(BF16)
    for q in range(nq):
        for h in range(N_HEADS):
            i = q * N_HEADS + h
            o_t = acc_s[i] / l_s[i]
            o_t = o_t * lax.rsqrt(jnp.sum(o_t * o_t, axis=0, keepdims=True) * (1.0 / HEAD) + EPS) * w_ref[h]
            o = lax.dot_general(eye, o_t.astype(BF16), (((1,), (1,)), ((), ())),
                                preferred_element_type=F32)
            o_ref[q * tq:(q + 1) * tq, HEAD * h:HEAD * (h + 1)] = o.astype(BF16)


def _gdn_kernel(q_ref, k_ref, v_ref, z_ref, gb_ref, rows_ref, gnw_ref, o_ref,
                s_ref, u_s, wq_s, kd_s, at_s, *, n_chunks, unroll, nb):
    ii = lax.broadcasted_iota(jnp.int32, (CHUNK, CHUNK), 0)
    jj = lax.broadcasted_iota(jnp.int32, (CHUNK, CHUNK), 1)
    eye = jnp.where(ii == jj, 1.0, 0.0)
    heads = range(N_HEADS)
    ring = 2 * unroll
    slot = lambda n, h: (n & (ring - 1)) * N_HEADS + h

    def solve(t, bb):
        ns = [t * unroll + c for c in range(unroll)]
        r0 = [pl.multiple_of(n * CHUNK, CHUNK) for n in ns]
        scal = [gb_ref[bb, pl.ds(r, CHUNK), :] for r in r0]
        rows = [rows_ref[bb, n] for n in ns]
        items = [(c, h) for c in range(unroll) for h in heads]
        cols = lambda h: slice(h * HEAD, (h + 1) * HEAD)
        q = [q_ref[bb, pl.ds(r0[c], CHUNK), cols(h)] for c, h in items]
        k = [k_ref[bb, pl.ds(r0[c], CHUNK), cols(h)] for c, h in items]
        v = [v_ref[bb, pl.ds(r0[c], CHUNK), cols(h)] for c, h in items]
        g_col = [jnp.broadcast_to(scal[c][:, h:h + 1], (CHUNK, LANES)) for c, h in items]
        b_col = [jnp.broadcast_to(scal[c][:, N_HEADS + h:N_HEADS + h + 1], (CHUNK, LANES))
                 for c, h in items]
        g_row = [rows[c][h:h + 1, :] for c, h in items]
        n_it = range(len(items))
        decay = [jnp.exp(jnp.where(ii >= jj, g_col[x][:, :CHUNK] - g_row[x], -jnp.inf)) for x in n_it]
        qkk = [_mm_nt(jnp.concatenate([q[x], k[x]], axis=0), k[x]) for x in n_it]
        yield
        lmat = [jnp.where(ii > jj, b_col[x][:, :CHUNK] * qkk[x][CHUNK:] * decay[x], 0.0) for x in n_it]
        inv = [eye - lmat[x] for x in n_it]
        pw = [_mm(lmat[x], lmat[x]) for x in n_it]
        yield
        for step in range(4):
            both = [_mm(jnp.concatenate([inv[x], pw[x]], axis=0), pw[x]) for x in n_it]
            inv = [inv[x] + both[x][:CHUNK] for x in n_it]
            pw = [both[x][CHUNK:] for x in n_it]
            yield
        inv = [inv[x] + _mm(inv[x], pw[x]) for x in n_it]
        yield
        e_col = [jnp.exp(g_col[x]) for x in n_it]
        rhs = [jnp.concatenate([v[x].astype(F32) * b_col[x], k[x].astype(F32) * (b_col[x] * e_col[x])],
                               axis=1) for x in n_it]
        uw = [_mm(inv[x], rhs[x]) for x in n_it]
        yield
        for x, (c, h) in enumerate(items):
            idx = slot(ns[c], h)
            g_last = g_col[x][CHUNK - 1:CHUNK, :]
            u_s[bb, idx] = uw[x][:, :HEAD]
            wq_s[bb, idx, 0:CHUNK, :] = uw[x][:, HEAD:].astype(BF16)
            wq_s[bb, idx, CHUNK:2 * CHUNK, :] = (q[x].astype(F32) * e_col[x]).astype(BF16)
            kd_s[bb, idx] = (k[x].astype(F32) * jnp.exp(g_last - g_col[x])).astype(BF16)
            at_s[bb, idx] = (qkk[x][:CHUNK] * decay[x]).astype(BF16)

    gnw = gnw_ref[...]

    def scan(t, bb):
        for c in range(unroll):
            n = t * unroll + c
            r0 = pl.multiple_of(n * CHUNK, CHUNK)
            g_last = gb_ref[bb, pl.ds(r0 + CHUNK - 1, 1), :]
            state = [s_ref[bb, h] for h in heads]
            sb = [x.astype(BF16) for x in state]
            ws = [_mm(wq_s[bb, slot(n, h)], sb[h]) for h in heads]
            yield
            v_new = [(u_s[bb, slot(n, h)] - ws[h][:CHUNK]).astype(BF16) for h in heads]
            o = [ws[h][CHUNK:] + _mm(at_s[bb, slot(n, h)], v_new[h]) for h in heads]
            upd = [_mm_tn(kd_s[bb, slot(n, h)], v_new[h]) for h in heads]
            yield
            for h in heads:
                c_dec = jnp.exp(jnp.broadcast_to(g_last[:, h:h + 1], (HEAD, HEAD)))
                s_ref[bb, h] = state[h] * c_dec + upd[h]
                zh = z_ref[bb, pl.ds(r0, CHUNK), h * HEAD:(h + 1) * HEAD].astype(F32)
                gated = _rms(o[h], gnw, HEAD) * (zh * _sigmoid(zh))
                o_ref[bb, pl.ds(r0, CHUNK), h * HEAD:(h + 1) * HEAD] = gated.astype(BF16)

    n_groups = n_chunks // unroll
    batches = range(nb)

    @pl.when(pl.program_id(1) == 0)
    def _():
        s_ref[...] = jnp.zeros_like(s_ref)

    _interleave(*[solve(0, bb) for bb in batches])

    def body(t, carry):
        _interleave(*[solve(t, bb) for bb in batches], *[scan(t - 1, bb) for bb in batches])
        return carry

    lax.fori_loop(1, n_groups, body, 0)
    _interleave(*[scan(n_groups - 1, bb) for bb in batches])


def _mlp_kernel(x_ref, mla_ref, gdn_ref, woa_ref, wob_ref, nw_ref, wup_ref, wdn_ref, o_ref):
    h = (x_ref[...]
         + jnp.dot(mla_ref[...], woa_ref[...], preferred_element_type=F32)
         + jnp.dot(gdn_ref[...], wob_ref[...], preferred_element_type=F32))
    hn = _rms(h, nw_ref[...], D_MODEL).astype(BF16)
    o_ref[...] = h
    for c in range(D_FF // MLP_FC):
        u = jnp.dot(hn, wup_ref[:, c * MLP_FC:(c + 1) * MLP_FC], preferred_element_type=F32)
        a = jnp.square(jnp.maximum(u, 0.0)).astype(BF16)
        o_ref[...] += jnp.dot(a, wdn_ref[c * MLP_FC:(c + 1) * MLP_FC, :], preferred_element_type=F32)


def _const_spec(shape):
    nd = len(shape)
    return pl.BlockSpec(shape, lambda *_: (0,) * nd, pipeline_mode=pl.Buffered(1))


def _pad_lanes(v, width=LANES):
    v = v.reshape(1, -1).astype(F32)
    return jnp.pad(v, ((0, 0), (0, width - v.shape[1])))


def _layer(h, pos, inv_freq, p):
    B, S, _ = h.shape
    T = B * S
    x2 = h.reshape(T, D_MODEL)

    w_in = p["w_in"]
    o_q, o_kv, o_pe = 0, LORA, 2 * LORA
    o_g = o_pe + ROPE
    o_z = o_g + 3 * MIX
    o_a = o_z + MIX
    w_lat = w_in[:, o_q:o_pe].astype(BF16)
    w_kpe = jnp.pad(w_in[:, o_pe:o_g], ((0, 0), (0, LANES - ROPE))).astype(BF16)
    w_g = w_in[:, o_g:o_z].astype(BF16)
    w_z = w_in[:, o_z:o_a].astype(BF16)
    w_ab = jnp.pad(w_in[:, o_a:], ((0, 0), (0, LANES - 2 * N_HEADS))).astype(BF16)
    w_uq = p["w_uq"].reshape(LORA, N_HEADS, QK_HEAD)
    w_uq_nt = w_uq[:, :, :NOPE].reshape(LORA, N_HEADS * NOPE).T.astype(BF16)
    w_uq_pt = w_uq[:, :, NOPE:].reshape(LORA, N_HEADS * ROPE).T.astype(BF16)
    w_ukv = p["w_ukv"].reshape(LORA, N_HEADS, NOPE + HEAD)
    w_uk_k = w_ukv[:, :, :NOPE].reshape(LORA, N_HEADS * NOPE).astype(BF16)
    w_uk_vt = w_ukv[:, :, NOPE:].reshape(LORA, N_HEADS * HEAD).T.astype(BF16)

    tm = PREP_TM
    n_tiles = T // tm
    assert n_tiles % 4 == 0
    n_pairs = n_tiles // 2
    row_spec = lambda w: pl.BlockSpec((4 * tm, w), lambda i: (i, 0))
    col_bcast = lambda v: jnp.broadcast_to(v.astype(F32)[:, None], (v.shape[0], tm))
    x_pair = lambda index: pl.BlockSpec((2 * tm, D_MODEL), index)
    prep_in = [
        (x2, x_pair(lambda i: (0, 0))),
        (x2, x_pair(lambda i: (2 * i + 1, 0))),
        (x2, x_pair(lambda i: (jnp.minimum(2 * i + 2, n_pairs - 1), 0))),
        (pos.reshape(n_tiles, 1, tm), pl.BlockSpec((4, 1, tm), lambda i: (i, 0, 0))),
        (p["attn_norm_w"].reshape(1, D_MODEL), None),
        (w_lat, None), (w_kpe, None), (w_ab, None), (w_g, None), (w_z, None),
        (p["q_lat_norm_w"].reshape(1, LORA), None),
        (p["kv_lat_norm_w"].reshape(1, LORA), None),
        (w_uq_nt, None), (w_uq_pt, None), (w_uk_k, None), (w_uk_vt, None),
        (col_bcast(p["q_norm_w"][:NOPE]), None),
        (col_bcast(p["q_norm_w"][NOPE:]), None),
        (p["k_norm_w"][:NOPE].reshape(1, NOPE), None),
        (_pad_lanes(p["k_norm_w"][NOPE:]), None),
        (col_bcast(inv_freq), None),
        (p["conv_w"], None),
        (_pad_lanes(p["a_log"]), None),
        (_pad_lanes(p["dt_bias"]), None),
    ]
    prep_args = [a for a, _ in prep_in]
    prep_specs = [s if s is not None else _const_spec(a.shape) for a, s in prep_in]
    out_shapes = [
        jax.ShapeDtypeStruct((n_tiles, 2 * MIX, tm), BF16),
        jax.ShapeDtypeStruct((T, 2 * MIX), BF16),
        jax.ShapeDtypeStruct((n_tiles, MIX, tm), BF16),
        jax.ShapeDtypeStruct((T, MIX), BF16),
        jax.ShapeDtypeStruct((T, MIX), BF16),
        jax.ShapeDtypeStruct((T, MIX), BF16),
        jax.ShapeDtypeStruct((T, MIX), BF16),
        jax.ShapeDtypeStruct((T, LANES), F32),
        jax.ShapeDtypeStruct((T // CHUNK, 8, CHUNK), F32),
    ]
    tile_spec = lambda r: pl.BlockSpec((4, r, tm), lambda i: (i, 0, 0))
    out_specs = [tile_spec(2 * MIX), row_spec(2 * MIX), tile_spec(MIX)] + [row_spec(MIX)] * 4 + [
        row_spec(LANES),
        pl.BlockSpec((4 * tm // CHUNK, 8, CHUNK), lambda i: (i, 0, 0)),
    ]
    qt, k, vt, gq, gk, gv, gz, gb, rows = pl.pallas_call(
        functools.partial(_prep_kernel, tm=tm, tiles_per_seq=S // tm),
        grid=(n_tiles // 4,),
        in_specs=prep_specs,
        out_specs=out_specs,
        out_shape=out_shapes,
        scratch_shapes=[
            pltpu.VMEM((2, 2 * tm, 2 * LORA), F32),
            pltpu.VMEM((2, 2 * tm, 3 * MIX), F32),
            pltpu.VMEM((2, 2 * tm, LANES), F32),
            pltpu.VMEM((2, 2 * tm, LANES), F32),
            pltpu.VMEM((2, 2 * tm, MIX), BF16),
            pltpu.VMEM((8, 3 * MIX), F32),
            pltpu.VMEM((tm, tm), BF16),
        ],
        compiler_params=pltpu.CompilerParams(dimension_semantics=("arbitrary",),
                                             vmem_limit_bytes=VMEM_LIMIT),
        name="prep",
    )(*prep_args)

    tq = tm
    nq = S // tq
    assert nq % 4 == 0
    mla_o = pl.pallas_call(
        functools.partial(_attn_kernel, tq=tq, nq=nq),
        grid=(B,),
        in_specs=[
            pl.BlockSpec((nq, 2 * MIX, tq), lambda b: (b, 0, 0)),
            pl.BlockSpec((None, S, 2 * MIX), lambda b: (b, 0, 0)),
            pl.BlockSpec((None, nq, MIX, tq), lambda b: (b, 0, 0, 0)),
            _const_spec((N_HEADS, HEAD, tq)),
        ],
        out_specs=pl.BlockSpec((None, S, MIX), lambda b: (b, 0, 0)),
        out_shape=jax.ShapeDtypeStruct((B, S, MIX), BF16),
        scratch_shapes=[pltpu.VMEM((nq * N_HEADS, 1, tq), F32),
                        pltpu.VMEM((nq * N_HEADS, 1, tq), F32),
                        pltpu.VMEM((nq * N_HEADS, HEAD, tq), F32),
                        pltpu.VMEM((2, 2 * N_HEADS, tq, tq), F32)],
        compiler_params=pltpu.CompilerParams(
            dimension_semantics=("arbitrary",),
            vmem_limit_bytes=VMEM_LIMIT),
        name="attn",
    )(qt, k.reshape(B, S, 2 * MIX), vt.reshape(B, nq, MIX, tq),
      jnp.broadcast_to(p["mla_out_norm_w"].astype(F32)[:, :, None], (N_HEADS, HEAD, tq)))

    n_chunks = S // CHUNK
    nb = GDN_BATCHES
    n_seg = GDN_SEGMENTS
    seg_chunks = n_chunks // n_seg
    assert B % nb == 0 and n_chunks % n_seg == 0 and seg_chunks % GDN_UNROLL == 0
    seq_spec = lambda w: pl.BlockSpec((nb, S // n_seg, w), lambda b, j: (b, j, 0))
    ring_slots = 2 * GDN_UNROLL * N_HEADS
    gdn_o = pl.pallas_call(
        functools.partial(_gdn_kernel, n_chunks=seg_chunks, unroll=GDN_UNROLL, nb=nb),
        grid=(B // nb, n_seg),
        in_specs=[seq_spec(MIX)] * 4 + [
            seq_spec(LANES),
            pl.BlockSpec((nb, seg_chunks, 8, CHUNK), lambda b, j: (b, j, 0, 0)),
            _const_spec((1, HEAD)),
        ],
        out_specs=seq_spec(MIX),
        out_shape=jax.ShapeDtypeStruct((B, S, MIX), BF16),
        scratch_shapes=[
            pltpu.VMEM((nb, N_HEADS, HEAD, HEAD), F32),
            pltpu.VMEM((nb, ring_slots, CHUNK, HEAD), F32),
            pltpu.VMEM((nb, ring_slots, 2 * CHUNK, HEAD), BF16),
            pltpu.VMEM((nb, ring_slots, CHUNK, HEAD), BF16),
            pltpu.VMEM((nb, ring_slots, CHUNK, CHUNK), BF16),
        ],
        compiler_params=pltpu.CompilerParams(dimension_semantics=("arbitrary", "arbitrary"),
                                             vmem_limit_bytes=VMEM_LIMIT),
        name="gdn",
    )(gq.reshape(B, S, MIX), gk.reshape(B, S, MIX), gv.reshape(B, S, MIX), gz.reshape(B, S, MIX),
      gb.reshape(B, S, LANES), rows.reshape(B, n_chunks, 8, CHUNK),
      p["gdn_norm_w"].reshape(1, HEAD))

    tm2 = MLP_TM
    w_out = p["w_out"].astype(BF16)
    tok_spec = lambda w: pl.BlockSpec((tm2, w), lambda i: (i, 0))
    out = pl.pallas_call(
        _mlp_kernel,
        grid=(T // tm2,),
        in_specs=[
            tok_spec(D_MODEL), tok_spec(MIX), tok_spec(MIX),
            _const_spec((MIX, D_MODEL)), _const_spec((MIX, D_MODEL)),
            _const_spec((1, D_MODEL)),
            _const_spec((D_MODEL, D_FF)), _const_spec((D_FF, D_MODEL)),
        ],
        out_specs=tok_spec(D_MODEL),
        out_shape=jax.ShapeDtypeStruct((T, D_MODEL), F32),
        compiler_params=pltpu.CompilerParams(dimension_semantics=("arbitrary",),
                                             vmem_limit_bytes=VMEM_LIMIT),
        name="mlp",
    )(x2, mla_o.reshape(T, MIX), gdn_o.reshape(T, MIX), w_out[:MIX], w_out[MIX:],
      p["mlp_norm_w"].reshape(1, D_MODEL), p["w_up"].astype(BF16), p["w_down"].astype(BF16))
    return out.reshape(B, S, D_MODEL)


def kernel(x, positions, attn_norm_w, w_in, q_lat_norm_w, w_uq, kv_lat_norm_w, w_ukv, q_norm_w,
           k_norm_w, mla_out_norm_w, conv_w, a_log, dt_bias, gdn_norm_w, w_out, mlp_norm_w, w_up,
           w_down):
    B, S, _ = x.shape
    half = ROPE // 2
    inv_freq = ROPE_THETA ** (-jnp.arange(half, dtype=F32) / half)
    params = dict(attn_norm_w=attn_norm_w, w_in=w_in, q_lat_norm_w=q_lat_norm_w, w_uq=w_uq,
                  kv_lat_norm_w=kv_lat_norm_w, w_ukv=w_ukv, q_norm_w=q_norm_w, k_norm_w=k_norm_w,
                  mla_out_norm_w=mla_out_norm_w, conv_w=conv_w, a_log=a_log, dt_bias=dt_bias,
                  gdn_norm_w=gdn_norm_w, w_out=w_out, mlp_norm_w=mlp_norm_w, w_up=w_up,
                  w_down=w_down)
    h = x
    for l in range(attn_norm_w.shape[0]):
        h = _layer(h, positions, inv_freq, {name: val[l] for name, val in params.items()})
    return h
```

```python
import functools

import jax
import jax.numpy as jnp
from jax import lax
from jax.experimental import pallas as pl
from jax.experimental.pallas import tpu as pltpu

F32 = jnp.float32
BF16 = jnp.bfloat16

D_MODEL = 1024
N_HEADS = 4
LORA = 256
NOPE = 128
ROPE = 64
HEAD = 128
QK_HEAD = NOPE + ROPE
ROPE_THETA = 10000.0
CONV_W = 4
CHUNK = 64
D_FF = 4 * D_MODEL
EPS = 1e-6
LOG2E = 1.4426950408889634
LANES = 128
MIX = N_HEADS * HEAD

PREP_TM = 256
GDN_UNROLL = 4
GDN_BATCHES = 1
GDN_SEGMENTS = 1
MLP_TM = 1024
MLP_FC = 1024
VMEM_LIMIT = 56 * 1024 * 1024


def _mm(a, b):
    return jnp.dot(a.astype(BF16), b.astype(BF16), preferred_element_type=F32)


def _mm_nt(a, b):
    return lax.dot_general(a.astype(BF16), b.astype(BF16), (((1,), (1,)), ((), ())),
                           preferred_element_type=F32)


def _mm_tn(a, b):
    return lax.dot_general(a.astype(BF16), b.astype(BF16), (((0,), (0,)), ((), ())),
                           preferred_element_type=F32)


def _split3(x):
    hi = x.astype(BF16)
    r1 = x - hi.astype(F32)
    mid = r1.astype(BF16)
    lo = (r1 - mid.astype(F32)).astype(BF16)
    return hi, mid, lo


def _interleave(*gens):
    live = list(gens)
    while live:
        for g in list(live):
            try:
                next(g)
            except StopIteration:
                live.remove(g)


def _rms(x, w, n):
    return x * lax.rsqrt(jnp.sum(x * x, axis=-1, keepdims=True) * (1.0 / n) + EPS) * w


def _sigmoid(x):
    return 1.0 / (1.0 + jnp.exp(-x))


def _rope(x, cosf, sinf, lane):
    rot = jnp.where(lane < ROPE // 2, pltpu.roll(x, LANES - ROPE // 2, 1),
                    pltpu.roll(x, ROPE // 2, 1))
    return x * cosf + rot * sinf


def _prep_kernel(x0_ref, xa_ref, xb_ref, posr_ref, anw_ref, wlat_ref, wkpe_ref, wab_ref, wg_ref, wz_ref,
                 qlnw_ref, kvlnw_ref, wuqnt_ref, wuqpt_ref, wukk_ref, wukvt_ref,
                 qnwn_ref, qnwp_ref, knwn_ref, knwp_ref, invft_ref, convw_ref,
                 alog_ref, dtb_ref,
                 qt_out, k_out, vt_out, gq_out, gk_out, gv_out, z_out, gb_out, rows_out,
                 lat_s0, g_s0, kpe_s0, ab_s0, z_s0, lat_s1, g_s1, kpe_s1, ab_s1, z_s1, tail, tri_s,
                 *, tm, tiles_per_seq):
    i = pl.program_id(0)
    half = ROPE // 2

    sets = ((lat_s0, g_s0, kpe_s0, ab_s0, z_s0), (lat_s1, g_s1, kpe_s1, ab_s1, z_s1))

    def front(x_ref, slot):
        lat_s, g_s, kpe_s, ab_s, z_s = sets[slot]
        xn = _rms(x_ref[...], anw_ref[...], D_MODEL).astype(BF16)
        yield
        step = 2 * LANES
        for w_ref, dst in ((wlat_ref, lat_s), (wg_ref, g_s)):
            for c0 in range(0, w_ref.shape[1], step):
                dst[:, c0:c0 + step] = jnp.dot(xn, w_ref[:, c0:c0 + step], preferred_element_type=F32)
                yield
        kpe_s[...] = jnp.dot(xn, wkpe_ref[...], preferred_element_type=F32)
        ab_s[...] = jnp.dot(xn, wab_ref[...], preferred_element_type=F32)
        yield
        for c0 in range(0, MIX, step):
            z_s[:, c0:c0 + step] = jnp.dot(xn, wz_ref[:, c0:c0 + step],
                                           preferred_element_type=F32).astype(BF16)
            yield

    def back(slot, sub, a):
        lat_s, g_s, kpe_s, ab_s, z_s = sets[slot]
        rows = slice(a * tm, (a + 1) * tm)
        src = slice(sub * tm, (sub + 1) * tm)
        z_out[rows, :] = z_s[src, :]
        ang_t = invft_ref[...] * posr_ref[a].astype(F32)
        cos_t = jnp.cos(ang_t)
        sin_t = jnp.sin(ang_t)
        lane = lax.broadcasted_iota(jnp.int32, (tm, LANES), 1)
        table = jnp.concatenate([cos_t, sin_t, jnp.zeros((LANES - ROPE, tm), F32)], axis=0).T
        cosf = jnp.where(lane < half, table, jnp.where(lane < ROPE, pltpu.roll(table, half, 1), 0.0))
        sinf = jnp.where(lane < half, -pltpu.roll(table, LANES - half, 1),
                         jnp.where(lane < ROPE, table, 0.0))
        yield

        qn = _rms(lat_s[src, :LORA], qlnw_ref[...], LORA)
        kvn = _rms(lat_s[src, LORA:], kvlnw_ref[...], LORA)
        qn_t = qn.T.astype(BF16)
        kvn_t = kvn.T.astype(BF16)
        kvn = kvn.astype(BF16)
        qt_nope = jnp.dot(wuqnt_ref[...], qn_t, preferred_element_type=F32)
        qt_pe = jnp.dot(wuqpt_ref[...], qn_t, preferred_element_type=F32)
        vt_out[a] = jnp.dot(wukvt_ref[...], kvn_t, preferred_element_type=F32).astype(BF16)
        k_nope = jnp.dot(kvn, wukk_ref[...], preferred_element_type=F32)
        k_pe = _rope(_rms(kpe_s[src, :], knwp_ref[...], ROPE), cosf, sinf, lane).astype(BF16)
        yield

        scale = QK_HEAD ** -0.5 * LOG2E
        qwn = qnwn_ref[...] * scale
        qwp = qnwp_ref[...] * scale
        for h in range(N_HEADS):
            lo = h * HEAD
            base = h * 2 * HEAD
            xn = qt_nope[lo:lo + NOPE, :]
            xn = xn * lax.rsqrt(jnp.sum(xn * xn, axis=0, keepdims=True) * (1.0 / NOPE) + EPS) * qwn
            xp = qt_pe[h * ROPE:(h + 1) * ROPE, :]
            xp = xp * lax.rsqrt(jnp.sum(xp * xp, axis=0, keepdims=True) * (1.0 / ROPE) + EPS) * qwp
            t1, t2 = xp[:half], xp[half:]
            qt_out[a, base:base + NOPE, :] = xn.astype(BF16)
            qt_out[a, base + NOPE:base + NOPE + half, :] = (t1 * cos_t - t2 * sin_t).astype(BF16)
            qt_out[a, base + NOPE + half:base + QK_HEAD, :] = (t2 * cos_t + t1 * sin_t).astype(BF16)
            qt_out[a, base + QK_HEAD:base + 2 * HEAD, :] = jnp.zeros((2 * HEAD - QK_HEAD, tm), BF16)
            k_out[rows, base:base + HEAD] = _rms(k_nope[:, lo:lo + HEAD], knwn_ref[...], NOPE).astype(BF16)
            k_out[rows, base + HEAD:base + 2 * HEAD] = k_pe
        yield

        seq_start = (4 * i + a) % tiles_per_seq == 0
        row8 = lax.broadcasted_iota(jnp.int32, (8, HEAD), 0)
        outs = ((gq_out, HEAD ** -0.5), (gk_out, 1.0), (gv_out, None))
        for part, (dst, norm_scale) in enumerate(outs):
            for h in range(N_HEADS):
                cols = slice(part * MIX + h * HEAD, part * MIX + (h + 1) * HEAD)
                g_blk = g_s[src, cols]
                prev = jnp.where(seq_start, 0.0, tail[:, cols])
                conv = convw_ref[CONV_W - 1:CONV_W, cols] * g_blk
                for s in range(1, CONV_W):
                    rolled = pltpu.roll(g_blk, s, 0)
                    first = jnp.where(row8 < s, pltpu.roll(prev, s, 0), rolled[:8])
                    shifted = jnp.concatenate([first, rolled[8:]], axis=0)
                    conv = conv + convw_ref[CONV_W - 1 - s:CONV_W - s, cols] * shifted
                tail[:, cols] = g_blk[tm - 8:tm, :]
                half_conv = 0.5 * conv
                act = half_conv + half_conv * jnp.tanh(half_conv)
                if norm_scale is not None:
                    act = act * (lax.rsqrt(jnp.sum(act * act, axis=-1, keepdims=True) + EPS) * norm_scale)
                dst[rows, h * HEAD:(h + 1) * HEAD] = act.astype(BF16)
            yield

        ab = ab_s[src, :]
        sp_in = ab + dtb_ref[...]
        softplus = jnp.maximum(sp_in, 0.0) + jnp.log(1.0 + jnp.exp(-jnp.abs(sp_in)))
        g = -jnp.exp(alog_ref[...]) * softplus
        beta = _sigmoid(ab)
        tri = tri_s[...]
        g_hi, g_mid, g_lo = _split3(g)
        gcum = (jnp.dot(tri, g_hi, preferred_element_type=F32)
                + jnp.dot(tri, g_mid, preferred_element_type=F32)
                + jnp.dot(tri, g_lo, preferred_element_type=F32))
        gb = jnp.where(lane < N_HEADS, gcum, jnp.where(lane < 2 * N_HEADS, beta, 0.0))
        gb_out[rows, :] = gb
        er = lax.broadcasted_iota(jnp.int32, (8, LANES), 0)
        ec = lax.broadcasted_iota(jnp.int32, (8, LANES), 1)
        eye8 = jnp.where(er == ec, 1.0, 0.0).astype(BF16)
        for c in range(tm // CHUNK):
            parts = _split3(gb[c * CHUNK:(c + 1) * CHUNK, :])
            acc = None
            for p in parts:
                t = lax.dot_general(eye8, p, (((1,), (1,)), ((), ())), preferred_element_type=F32)
                acc = t if acc is None else acc + t
            rows_out[a * (tm // CHUNK) + c] = acc

    @pl.when(i == 0)
    def _():
        tail[...] = jnp.zeros(tail.shape, F32)
        rr = lax.broadcasted_iota(jnp.int32, (tm, tm), 0)
        cc = lax.broadcasted_iota(jnp.int32, (tm, tm), 1)
        tri_s[...] = jnp.where((cc <= rr) & ((cc // CHUNK) == (rr // CHUNK)), 1.0, 0.0).astype(BF16)
        _interleave(front(x0_ref, 0))

    def backs(slot, first_tile):
        yield from back(slot, 0, first_tile)
        yield from back(slot, 1, first_tile + 1)

    _interleave(front(xa_ref, 1), backs(0, 0))
    _interleave(front(xb_ref, 0), backs(1, 2))


def _attend(pairs, qt_ref, k_ref, vt_ref, m_s, l_s, acc_s, st_s, *, tq, nq):
    heads = range(N_HEADS)
    steps = nq + 1
    kk = lax.broadcasted_iota(jnp.int32, (tq, tq), 0)
    qq = lax.broadcasted_iota(jnp.int32, (tq, tq), 1)
    ones = jnp.ones((16, tq), BF16)
    hk = lambda h: slice(2 * HEAD * h, 2 * HEAD * (h + 1))
    hv = lambda h: slice(HEAD * h, HEAD * (h + 1))

    def blocks(s):
        return [(jnp.where(s <= p, p, nq - 1 - p), jnp.where(s <= p, p - s, s - p - 1)) for p in pairs]

    def scores(s, c, h, q, kv):
        r0 = pl.multiple_of(kv * tq, tq)
        st_s[s % 2, c * N_HEADS + h] = jnp.dot(k_ref[pl.ds(r0, tq), hk(h)], qt_ref[q, hk(h), :],
                                               preferred_element_type=F32)

    for c, (q, kv) in enumerate(blocks(0)):
        for h in heads:
            scores(0, c, h, q, kv)
    yield
    for s in range(steps):
        nxt = blocks(s + 1) if s + 1 < steps else None
        for c, (q, kv) in enumerate(blocks(s)):
            for h in heads:
                if nxt is not None:
                    scores(s + 1, c, h, *nxt[c])
                i = q * N_HEADS + h
                st = st_s[s % 2, c * N_HEADS + h]
                if s in (0, steps - 1):
                    st = jnp.where(kk <= qq, st, -jnp.inf)
                m_old = m_s[i]
                m_new = jnp.maximum(m_old, jnp.max(st, axis=0, keepdims=True))
                alpha = jnp.exp2(m_old - m_new)
                p_t = jnp.exp2(st - m_new).astype(BF16)
                pv = jnp.dot(jnp.concatenate([vt_ref[kv, hv(h), :], ones], axis=0), p_t,
                             preferred_element_type=F32)
                m_s[i] = m_new
                l_s[i] = alpha * l_s[i] + pv[HEAD:HEAD + 1]
                acc_s[i] = alpha * acc_s[i] + pv[:HEAD]
            yield


def _attn_kernel(qt_ref, k_ref, vt_ref, w_ref, o_ref, m_s, l_s, acc_s, st_s, *, tq, nq):
    m_s[...] = jnp.full(m_s.shape, -jnp.inf, F32)
    l_s[...] = jnp.zeros(l_s.shape, F32)
    acc_s[...] = jnp.zeros(acc_s.shape, F32)
    n_pairs = nq // 2

    def body(t, carry):
        _interleave(_attend([t, n_pairs - 1 - t], qt_ref, k_ref, vt_ref, m_s, l_s, acc_s, st_s,
                            tq=tq, nq=nq))
        return carry

    lax.fori_loop(0, n_pairs // 2, body, 0)
    rr = lax.broadcasted_iota(jnp.int32, (tq, tq), 0)
    cc = lax.broadcasted_iota(jnp.int32, (tq, tq), 1)
    eye = jnp.where(rr == cc, 1.0, 0.0).astype(BF16)
    for q in range(nq):
        for h in range(N_HEADS):
            i = q * N_HEADS + h
            o_t = acc_s[i] / l_s[i]
            o_t = o_t * lax.rsqrt(jnp.sum(o_t * o_t, axis=0, keepdims=True) * (1.0 / HEAD) + EPS) * w_ref[h]
            o = lax.dot_general(eye, o_t.astype(BF16), (((1,), (1,)), ((), ())),
                                preferred_element_type=F32)
            o_ref[q * tq:(q + 1) * tq, HEAD * h:HEAD * (h + 1)] = o.astype(BF16)


def _gdn_kernel(q_ref, k_ref, v_ref, z_ref, gb_ref, rows_ref, gnw_ref, o_ref,
                s_ref, u_s, wq_s, kd_s, at_s, *, n_chunks, unroll, nb):
    ii = lax.broadcasted_iota(jnp.int32, (CHUNK, CHUNK), 0)
    jj = lax.broadcasted_iota(jnp.int32, (CHUNK, CHUNK), 1)
    eye = jnp.where(ii == jj, 1.0, 0.0)
    heads = range(N_HEADS)
    ring = 2 * unroll
    slot = lambda n, h: (n & (ring - 1)) * N_HEADS + h

    def solve(t, bb):
        ns = [t * unroll + c for c in range(unroll)]
        r0 = [pl.multiple_of(n * CHUNK, CHUNK) for n in ns]
        scal = [gb_ref[bb, pl.ds(r, CHUNK), :] for r in r0]
        rows = [rows_ref[bb, n] for n in ns]
        items = [(c, h) for c in range(unroll) for h in heads]
        cols = lambda h: slice(h * HEAD, (h + 1) * HEAD)
        q = [q_ref[bb, pl.ds(r0[c], CHUNK), cols(h)] for c, h in items]
        k = [k_ref[bb, pl.ds(r0[c], CHUNK), cols(h)] for c, h in items]
        v = [v_ref[bb, pl.ds(r0[c], CHUNK), cols(h)] for c, h in items]
        g_col = [jnp.broadcast_to(scal[c][:, h:h + 1], (CHUNK, LANES)) for c, h in items]
        b_col = [jnp.broadcast_to(scal[c][:, N_HEADS + h:N_HEADS + h + 1], (CHUNK, LANES))
                 for c, h in items]
        g_row = [rows[c][h:h + 1, :] for c, h in items]
        n_it = range(len(items))
        decay = [jnp.exp(jnp.where(ii >= jj, g_col[x][:, :CHUNK] - g_row[x], -jnp.inf)) for x in n_it]
        qkk = [_mm_nt(jnp.concatenate([q[x], k[x]], axis=0), k[x]) for x in n_it]
        yield
        lmat = [jnp.where(ii > jj, b_col[x][:, :CHUNK] * qkk[x][CHUNK:] * decay[x], 0.0) for x in n_it]
        inv = [eye - lmat[x] for x in n_it]
        pw = [_mm(lmat[x], lmat[x]) for x in n_it]
        yield
        for step in range(4):
            both = [_mm(jnp.concatenate([inv[x], pw[x]], axis=0), pw[x]) for x in n_it]
            inv = [inv[x] + both[x][:CHUNK] for x in n_it]
            pw = [both[x][CHUNK:] for x in n_it]
            yield
        inv = [inv[x] + _mm(inv[x], pw[x]) for x in n_it]
        yield
        e_col = [jnp.exp(g_col[x]) for x in n_it]
        rhs = [jnp.concatenate([v[x].astype(F32) * b_col[x], k[x].astype(F32) * (b_col[x] * e_col[x])],
                               axis=1) for x in n_it]
        uw = [_mm(inv[x], rhs[x]) for x in n_it]
        yield
        for x, (c, h) in enumerate(items):
            idx = slot(ns[c], h)
            g_last = g_col[x][CHUNK - 1:CHUNK, :]
            u_s[bb, idx] = uw[x][:, :HEAD]
            wq_s[bb, idx, 0:CHUNK, :] = uw[x][:, HEAD:].astype(BF16)
            wq_s[bb, idx, CHUNK:2 * CHUNK, :] = (q[x].astype(F32) * e_col[x]).astype(BF16)
            kd_s[bb, idx] = (k[x].astype(F32) * jnp.exp(g_last - g_col[x])).astype(BF16)
            at_s[bb, idx] = (qkk[x][:CHUNK] * decay[x]).astype(BF16)

    gnw = gnw_ref[...]

    def scan(t, bb):
        for c in range(unroll):
            n = t * unroll + c
            r0 = pl.multiple_of(n * CHUNK, CHUNK)
            g_last = gb_ref[bb, pl.ds(r0 + CHUNK - 1, 1), :]
            state = [s_ref[bb, h] for h in heads]
            sb = [x.astype(BF16) for x in state]
            ws = [_mm(wq_s[bb, slot(n, h)], sb[h]) for h in heads]
            yield
            v_new = [(u_s[bb, slot(n, h)] - ws[h][:CHUNK]).astype(BF16) for h in heads]
            o = [ws[h][CHUNK:] + _mm(at_s[bb, slot(n, h)], v_new[h]) for h in heads]
            upd = [_mm_tn(kd_s[bb, slot(n, h)], v_new[h]) for h in heads]
            yield
            for h in heads:
                c_dec = jnp.exp(jnp.broadcast_to(g_last[:, h:h + 1], (HEAD, HEAD)))
                s_ref[bb, h] = state[h] * c_dec + upd[h]
                zh = z_ref[bb, pl.ds(r0, CHUNK), h * HEAD:(h + 1) * HEAD].astype(F32)
                gated = _rms(o[h], gnw, HEAD) * (zh * _sigmoid(zh))
                o_ref[bb, pl.ds(r0, CHUNK), h * HEAD:(h + 1) * HEAD] = gated.astype(BF16)

    n_groups = n_chunks // unroll
    batches = range(nb)

    @pl.when(pl.program_id(1) == 0)
    def _():
        s_ref[...] = jnp.zeros_like(s_ref)

    _interleave(*[solve(0, bb) for bb in batches])

    def body(t, carry):
        _interleave(*[solve(t, bb) for bb in batches], *[scan(t - 1, bb) for bb in batches])
        return carry

    lax.fori_loop(1, n_groups, body, 0)
    _interleave(*[scan(n_groups - 1, bb) for bb in batches])


def _mlp_kernel(x_ref, mla_ref, gdn_ref, woa_ref, wob_ref, nw_ref, wup_ref, wdn_ref, o_ref):
    h = (x_ref[...]
         + jnp.dot(mla_ref[...], woa_ref[...], preferred_element_type=F32)
         + jnp.dot(gdn_ref[...], wob_ref[...], preferred_element_type=F32))
    hn = _rms(h, nw_ref[...], D_MODEL).astype(BF16)
    o_ref[...] = h
    for c in range(D_FF // MLP_FC):
        u = jnp.dot(hn, wup_ref[:, c * MLP_FC:(c + 1) * MLP_FC], preferred_element_type=F32)
        a = jnp.square(jnp.maximum(u, 0.0)).astype(BF16)
        o_ref[...] += jnp.dot(a, wdn_ref[c * MLP_FC:(c + 1) * MLP_FC, :], preferred_element_type=F32)


def _const_spec(shape):
    nd = len(shape)
    return pl.BlockSpec(shape, lambda *_: (0,) * nd, pipeline_mode=pl.Buffered(1))


def _pad_lanes(v, width=LANES):
    v = v.reshape(1, -1).astype(F32)
    return jnp.pad(v, ((0, 0), (0, width - v.shape[1])))


def _layer(h, pos, inv_freq, p):
    B, S, _ = h.shape
    T = B * S
    x2 = h.reshape(T, D_MODEL)

    w_in = p["w_in"]
    o_q, o_kv, o_pe = 0, LORA, 2 * LORA
    o_g = o_pe + ROPE
    o_z = o_g + 3 * MIX
    o_a = o_z + MIX
    w_lat = w_in[:, o_q:o_pe].astype(BF16)
    w_kpe = jnp.pad(w_in[:, o_pe:o_g], ((0, 0), (0, LANES - ROPE))).astype(BF16)
    w_g = w_in[:, o_g:o_z].astype(BF16)
    w_z = w_in[:, o_z:o_a].astype(BF16)
    w_ab = jnp.pad(w_in[:, o_a:], ((0, 0), (0, LANES - 2 * N_HEADS))).astype(BF16)
    w_uq = p["w_uq"].reshape(LORA, N_HEADS, QK_HEAD)
    w_uq_nt = w_uq[:, :, :NOPE].reshape(LORA, N_HEADS * NOPE).T.astype(BF16)
    w_uq_pt = w_uq[:, :, NOPE:].reshape(LORA, N_HEADS * ROPE).T.astype(BF16)
    w_ukv = p["w_ukv"].reshape(LORA, N_HEADS, NOPE + HEAD)
    w_uk_k = w_ukv[:, :, :NOPE].reshape(LORA, N_HEADS * NOPE).astype(BF16)
    w_uk_vt = w_ukv[:, :, NOPE:].reshape(LORA, N_HEADS * HEAD).T.astype(BF16)

    tm = PREP_TM
    n_tiles = T // tm
    assert n_tiles % 4 == 0
    n_pairs = n_tiles // 2
    row_spec = lambda w: pl.BlockSpec((4 * tm, w), lambda i: (i, 0))
    col_bcast = lambda v: jnp.broadcast_to(v.astype(F32)[:, None], (v.shape[0], tm))
    x_pair = lambda index: pl.BlockSpec((2 * tm, D_MODEL), index)
    prep_in = [
        (x2, x_pair(lambda i: (0, 0))),
        (x2, x_pair(lambda i: (2 * i + 1, 0))),
        (x2, x_pair(lambda i: (jnp.minimum(2 * i + 2, n_pairs - 1), 0))),
        (pos.reshape(n_tiles, 1, tm), pl.BlockSpec((4, 1, tm), lambda i: (i, 0, 0))),
        (p["attn_norm_w"].reshape(1, D_MODEL), None),
        (w_lat, None), (w_kpe, None), (w_ab, None), (w_g, None), (w_z, None),
        (p["q_lat_norm_w"].reshape(1, LORA), None),
        (p["kv_lat_norm_w"].reshape(1, LORA), None),
        (w_uq_nt, None), (w_uq_pt, None), (w_uk_k, None), (w_uk_vt, None),
        (col_bcast(p["q_norm_w"][:NOPE]), None),
        (col_bcast(p["q_norm_w"][NOPE:]), None),
        (p["k_norm_w"][:NOPE].reshape(1, NOPE), None),
        (_pad_lanes(p["k_norm_w"][NOPE:]), None),
        (col_bcast(inv_freq), None),
        (p["conv_w"], None),
        (_pad_lanes(p["a_log"]), None),
        (_pad_lanes(p["dt_bias"]), None),
    ]
    prep_args = [a for a, _ in prep_in]
    prep_specs = [s if s is not None else _const_spec(a.shape) for a, s in prep_in]
    out_shapes = [
        jax.ShapeDtypeStruct((n_tiles, 2 * MIX, tm), BF16),
        jax.ShapeDtypeStruct((T, 2 * MIX), BF16),
        jax.ShapeDtypeStruct((n_tiles, MIX, tm), BF16),
        jax.ShapeDtypeStruct((T, MIX), BF16),
        jax.ShapeDtypeStruct((T, MIX), BF16),
        jax.ShapeDtypeStruct((T, MIX), BF16),
        jax.ShapeDtypeStruct((T, MIX), BF16),
        jax.ShapeDtypeStruct((T, LANES), F32),
        jax.ShapeDtypeStruct((T // CHUNK, 8, CHUNK), F32),
    ]
    tile_spec = lambda r: pl.BlockSpec((4, r, tm), lambda i: (i, 0, 0))
    out_specs = [tile_spec(2 * MIX), row_spec(2 * MIX), tile_spec(MIX)] + [row_spec(MIX)] * 4 + [
        row_spec(LANES),
        pl.BlockSpec((4 * tm // CHUNK, 8, CHUNK), lambda i: (i, 0, 0)),
    ]
    qt, k, vt, gq, gk, gv, gz, gb, rows = pl.pallas_call(
        functools.partial(_prep_kernel, tm=tm, tiles_per_seq=S // tm),
        grid=(n_tiles // 4,),
        in_specs=prep_specs,
        out_specs=out_specs,
        out_shape=out_shapes,
        scratch_shapes=2 * [
            pltpu.VMEM((2 * tm, 2 * LORA), F32),
            pltpu.VMEM((2 * tm, 3 * MIX), F32),
            pltpu.VMEM((2 * tm, LANES), F32),
            pltpu.VMEM((2 * tm, LANES), F32),
            pltpu.VMEM((2 * tm, MIX), BF16),
        ] + [
            pltpu.VMEM((8, 3 * MIX), F32),
            pltpu.VMEM((tm, tm), BF16),
        ],
        compiler_params=pltpu.CompilerParams(dimension_semantics=("arbitrary",),
                                             vmem_limit_bytes=VMEM_LIMIT),
        name="prep",
    )(*prep_args)

    tq = tm
    nq = S // tq
    assert nq % 4 == 0
    mla_o = pl.pallas_call(
        functools.partial(_attn_kernel, tq=tq, nq=nq),
        grid=(B,),
        in_specs=[
            pl.BlockSpec((nq, 2 * MIX, tq), lambda b: (b, 0, 0)),
            pl.BlockSpec((None, S, 2 * MIX), lambda b: (b, 0, 0)),
            pl.BlockSpec((None, nq, MIX, tq), lambda b: (b, 0, 0, 0)),
            _const_spec((N_HEADS, HEAD, tq)),
        ],
        out_specs=pl.BlockSpec((None, S, MIX), lambda b: (b, 0, 0)),
        out_shape=jax.ShapeDtypeStruct((B, S, MIX), BF16),
        scratch_shapes=[pltpu.VMEM((nq * N_HEADS, 1, tq), F32),
                        pltpu.VMEM((nq * N_HEADS, 1, tq), F32),
                        pltpu.VMEM((nq * N_HEADS, HEAD, tq), F32),
                        pltpu.VMEM((2, 2 * N_HEADS, tq, tq), F32)],
        compiler_params=pltpu.CompilerParams(
            dimension_semantics=("arbitrary",),
            vmem_limit_bytes=VMEM_LIMIT),
        name="attn",
    )(qt, k.reshape(B, S, 2 * MIX), vt.reshape(B, nq, MIX, tq),
      jnp.broadcast_to(p["mla_out_norm_w"].astype(F32)[:, :, None], (N_HEADS, HEAD, tq)))

    n_chunks = S // CHUNK
    nb = GDN_BATCHES
    n_seg = GDN_SEGMENTS
    seg_chunks = n_chunks // n_seg
    assert B % nb == 0 and n_chunks % n_seg == 0 and seg_chunks % GDN_UNROLL == 0
    seq_spec = lambda w: pl.BlockSpec((nb, S // n_seg, w), lambda b, j: (b, j, 0))
    ring_slots = 2 * GDN_UNROLL * N_HEADS
    gdn_o = pl.pallas_call(
        functools.partial(_gdn_kernel, n_chunks=seg_chunks, unroll=GDN_UNROLL, nb=nb),
        grid=(B // nb, n_seg),
        in_specs=[seq_spec(MIX)] * 4 + [
            seq_spec(LANES),
            pl.BlockSpec((nb, seg_chunks, 8, CHUNK), lambda b, j: (b, j, 0, 0)),
            _const_spec((1, HEAD)),
        ],
        out_specs=seq_spec(MIX),
        out_shape=jax.ShapeDtypeStruct((B, S, MIX), BF16),
        scratch_shapes=[
            pltpu.VMEM((nb, N_HEADS, HEAD, HEAD), F32),
            pltpu.VMEM((nb, ring_slots, CHUNK, HEAD), F32),
            pltpu.VMEM((nb, ring_slots, 2 * CHUNK, HEAD), BF16),
            pltpu.VMEM((nb, ring_slots, CHUNK, HEAD), BF16),
            pltpu.VMEM((nb, ring_slots, CHUNK, CHUNK), BF16),
        ],
        compiler_params=pltpu.CompilerParams(dimension_semantics=("arbitrary", "arbitrary"),
                                             vmem_limit_bytes=VMEM_LIMIT),
        name="gdn",
    )(gq.reshape(B, S, MIX), gk.reshape(B, S, MIX), gv.reshape(B, S, MIX), gz.reshape(B, S, MIX),
      gb.reshape(B, S, LANES), rows.reshape(B, n_chunks, 8, CHUNK),
      p["gdn_norm_w"].reshape(1, HEAD))

    tm2 = MLP_TM
    w_out = p["w_out"].astype(BF16)
    tok_spec = lambda w: pl.BlockSpec((tm2, w), lambda i: (i, 0))
    out = pl.pallas_call(
        _mlp_kernel,
        grid=(T // tm2,),
        in_specs=[
            tok_spec(D_MODEL), tok_spec(MIX), tok_spec(MIX),
            _const_spec((MIX, D_MODEL)), _const_spec((MIX, D_MODEL)),
            _const_spec((1, D_MODEL)),
            _const_spec((D_MODEL, D_FF)), _const_spec((D_FF, D_MODEL)),
        ],
        out_specs=tok_spec(D_MODEL),
        out_shape=jax.ShapeDtypeStruct((T, D_MODEL), F32),
        compiler_params=pltpu.CompilerParams(dimension_semantics=("arbitrary",),
                                             vmem_limit_bytes=VMEM_LIMIT),
        name="mlp",
    )(x2, mla_o.reshape(T, MIX), gdn_o.reshape(T, MIX), w_out[:MIX], w_out[MIX:],
      p["mlp_norm_w"].reshape(1, D_MODEL), p["w_up"].astype(BF16), p["w_down"].astype(BF16))
    return out.reshape(B, S, D_MODEL)


def kernel(x, positions, attn_norm_w, w_in, q_lat_norm_w, w_uq, kv_lat_norm_w, w_ukv, q_norm_w,
           k_norm_w, mla_out_norm_w, conv_w, a_log, dt_bias, gdn_norm_w, w_out, mlp_norm_w, w_up,
           w_down):
    B, S, _ = x.shape
    half = ROPE // 2
    inv_freq = ROPE_THETA ** (-jnp.arange(half, dtype=F32) / half)
    params = dict(attn_norm_w=attn_norm_w, w_in=w_in, q_lat_norm_w=q_lat_norm_w, w_uq=w_uq,
                  kv_lat_norm_w=kv_lat_norm_w, w_ukv=w_ukv, q_norm_w=q_norm_w, k_norm_w=k_norm_w,
                  mla_out_norm_w=mla_out_norm_w, conv_w=conv_w, a_log=a_log, dt_bias=dt_bias,
                  gdn_norm_w=gdn_norm_w, w_out=w_out, mlp_norm_w=mlp_norm_w, w_up=w_up,
                  w_down=w_down)
    h = x
    for l in range(attn_norm_w.shape[0]):
        h = _layer(h, positions, inv_freq, {name: val[l] for name, val in params.items()})
    return h
```

```python
import functools

import jax
import jax.numpy as jnp
from jax import lax
from jax.experimental import pallas as pl
from jax.experimental.pallas import tpu as pltpu

F32 = jnp.float32
BF16 = jnp.bfloat16

D_MODEL = 1024
N_HEADS = 4
LORA = 256
NOPE = 128
ROPE = 64
HEAD = 128
QK_HEAD = NOPE + ROPE
ROPE_THETA = 10000.0
CONV_W = 4
CHUNK = 64
D_FF = 4 * D_MODEL
EPS = 1e-6
LOG2E = 1.4426950408889634
LANES = 128
MIX = N_HEADS * HEAD

PREP_TM = 256
GDN_UNROLL = 4
GDN_BATCHES = 1
GDN_SEGMENTS = 1
MLP_TM = 1024
MLP_FC = 1024
VMEM_LIMIT = 56 * 1024 * 1024


def _mm(a, b):
    return jnp.dot(a.astype(BF16), b.astype(BF16), preferred_element_type=F32)


def _mm_nt(a, b):
    return lax.dot_general(a.astype(BF16), b.astype(BF16), (((1,), (1,)), ((), ())),
                           preferred_element_type=F32)


def _mm_tn(a, b):
    return lax.dot_general(a.astype(BF16), b.astype(BF16), (((0,), (0,)), ((), ())),
                           preferred_element_type=F32)


def _split3(x):
    hi = x.astype(BF16)
    r1 = x - hi.astype(F32)
    mid = r1.astype(BF16)
    lo = (r1 - mid.astype(F32)).astype(BF16)
    return hi, mid, lo


def _interleave(*gens):
    live = list(gens)
    while live:
        for g in list(live):
            try:
                next(g)
            except StopIteration:
                live.remove(g)


def _rms(x, w, n):
    return x * lax.rsqrt(jnp.sum(x * x, axis=-1, keepdims=True) * (1.0 / n) + EPS) * w


def _sigmoid(x):
    return 1.0 / (1.0 + jnp.exp(-x))


def _rope(x, cosf, sinf, lane):
    rot = jnp.where(lane < ROPE // 2, pltpu.roll(x, LANES - ROPE // 2, 1),
                    pltpu.roll(x, ROPE // 2, 1))
    return x * cosf + rot * sinf


def _prep_kernel(x0_ref, xa_ref, xb_ref, posr_ref, anw_ref, wlat_ref, wkpe_ref, wab_ref, wg_ref, wz_ref,
                 qlnw_ref, kvlnw_ref, wuqnt_ref, wuqpt_ref, wukk_ref, wukvt_ref,
                 qnwn_ref, qnwp_ref, knwn_ref, knwp_ref, invft_ref, convw_ref,
                 alog_ref, dtb_ref,
                 qt_out, k_out, vt_out, gq_out, gk_out, gv_out, z_out, gb_out, rows_out,
                 lat_s0, g_s0, kpe_s0, ab_s0, z_s0, lat_s1, g_s1, kpe_s1, ab_s1, z_s1, tail, tri_s,
                 *, tm, tiles_per_seq):
    i = pl.program_id(0)
    half = ROPE // 2

    sets = ((lat_s0, g_s0, kpe_s0, ab_s0, z_s0), (lat_s1, g_s1, kpe_s1, ab_s1, z_s1))

    def front(x_ref, slot):
        lat_s, g_s, kpe_s, ab_s, z_s = sets[slot]
        xn = _rms(x_ref[...], anw_ref[...], D_MODEL).astype(BF16)
        yield
        step = 2 * LANES
        for w_ref, dst in ((wlat_ref, lat_s), (wg_ref, g_s)):
            for c0 in range(0, w_ref.shape[1], step):
                dst[:, c0:c0 + step] = jnp.dot(xn, w_ref[:, c0:c0 + step], preferred_element_type=F32)
                yield
        kpe_s[...] = jnp.dot(xn, wkpe_ref[...], preferred_element_type=F32)
        ab_s[...] = jnp.dot(xn, wab_ref[...], preferred_element_type=F32)
        yield
        for c0 in range(0, MIX, step):
            z_s[:, c0:c0 + step] = jnp.dot(xn, wz_ref[:, c0:c0 + step],
                                           preferred_element_type=F32).astype(BF16)
            yield

    def back(slot, sub, a):
        lat_s, g_s, kpe_s, ab_s, z_s = sets[slot]
        rows = slice(a * tm, (a + 1) * tm)
        src = slice(sub * tm, (sub + 1) * tm)
        z_out[rows, :] = z_s[src, :]
        lane = lax.broadcasted_iota(jnp.int32, (tm, LANES), 1)
        yield from scalars(ab_s[src, :], lane, rows, a)
        ang_t = invft_ref[...] * posr_ref[a].astype(F32)
        cos_t = jnp.cos(ang_t)
        sin_t = jnp.sin(ang_t)
        table = jnp.concatenate([cos_t, sin_t, jnp.zeros((LANES - ROPE, tm), F32)], axis=0).T
        cosf = jnp.where(lane < half, table, jnp.where(lane < ROPE, pltpu.roll(table, half, 1), 0.0))
        sinf = jnp.where(lane < half, -pltpu.roll(table, LANES - half, 1),
                         jnp.where(lane < ROPE, table, 0.0))
        yield

        qn = _rms(lat_s[src, :LORA], qlnw_ref[...], LORA)
        kvn = _rms(lat_s[src, LORA:], kvlnw_ref[...], LORA)
        qn_t = qn.T.astype(BF16)
        kvn_t = kvn.T.astype(BF16)
        kvn = kvn.astype(BF16)
        qt_nope = jnp.dot(wuqnt_ref[...], qn_t, preferred_element_type=F32)
        qt_pe = jnp.dot(wuqpt_ref[...], qn_t, preferred_element_type=F32)
        vt_out[a] = jnp.dot(wukvt_ref[...], kvn_t, preferred_element_type=F32).astype(BF16)
        k_nope = jnp.dot(kvn, wukk_ref[...], preferred_element_type=F32)
        k_pe = _rope(_rms(kpe_s[src, :], knwp_ref[...], ROPE), cosf, sinf, lane).astype(BF16)
        yield

        scale = QK_HEAD ** -0.5 * LOG2E
        qwn = qnwn_ref[...] * scale
        qwp = qnwp_ref[...] * scale
        for h in range(N_HEADS):
            lo = h * HEAD
            base = h * 2 * HEAD
            xn = qt_nope[lo:lo + NOPE, :]
            xn = xn * lax.rsqrt(jnp.sum(xn * xn, axis=0, keepdims=True) * (1.0 / NOPE) + EPS) * qwn
            xp = qt_pe[h * ROPE:(h + 1) * ROPE, :]
            xp = xp * lax.rsqrt(jnp.sum(xp * xp, axis=0, keepdims=True) * (1.0 / ROPE) + EPS) * qwp
            t1, t2 = xp[:half], xp[half:]
            qt_out[a, base:base + NOPE, :] = xn.astype(BF16)
            qt_out[a, base + NOPE:base + NOPE + half, :] = (t1 * cos_t - t2 * sin_t).astype(BF16)
            qt_out[a, base + NOPE + half:base + QK_HEAD, :] = (t2 * cos_t + t1 * sin_t).astype(BF16)
            qt_out[a, base + QK_HEAD:base + 2 * HEAD, :] = jnp.zeros((2 * HEAD - QK_HEAD, tm), BF16)
            k_out[rows, base:base + HEAD] = _rms(k_nope[:, lo:lo + HEAD], knwn_ref[...], NOPE).astype(BF16)
            k_out[rows, base + HEAD:base + 2 * HEAD] = k_pe
        yield

        seq_start = (4 * i + a) % tiles_per_seq == 0
        row8 = lax.broadcasted_iota(jnp.int32, (8, HEAD), 0)
        outs = ((gq_out, HEAD ** -0.5), (gk_out, 1.0), (gv_out, None))
        for part, (dst, norm_scale) in enumerate(outs):
            for h in range(N_HEADS):
                cols = slice(part * MIX + h * HEAD, part * MIX + (h + 1) * HEAD)
                g_blk = g_s[src, cols]
                prev = jnp.where(seq_start, 0.0, tail[:, cols])
                conv = convw_ref[CONV_W - 1:CONV_W, cols] * g_blk
                for s in range(1, CONV_W):
                    rolled = pltpu.roll(g_blk, s, 0)
                    first = jnp.where(row8 < s, pltpu.roll(prev, s, 0), rolled[:8])
                    shifted = jnp.concatenate([first, rolled[8:]], axis=0)
                    conv = conv + convw_ref[CONV_W - 1 - s:CONV_W - s, cols] * shifted
                tail[:, cols] = g_blk[tm - 8:tm, :]
                half_conv = 0.5 * conv
                act = half_conv + half_conv * jnp.tanh(half_conv)
                if norm_scale is not None:
                    act = act * (lax.rsqrt(jnp.sum(act * act, axis=-1, keepdims=True) + EPS) * norm_scale)
                dst[rows, h * HEAD:(h + 1) * HEAD] = act.astype(BF16)
            yield

    def scalars(ab, lane, rows, a):
        sp_in = ab + dtb_ref[...]
        softplus = jnp.maximum(sp_in, 0.0) + jnp.log(1.0 + jnp.exp(-jnp.abs(sp_in)))
        g = -jnp.exp(alog_ref[...]) * softplus
        beta = _sigmoid(ab)
        tri = tri_s[...]
        g_hi, g_mid, g_lo = _split3(g)
        gcum = (jnp.dot(tri, g_hi, preferred_element_type=F32)
                + jnp.dot(tri, g_mid, preferred_element_type=F32)
                + jnp.dot(tri, g_lo, preferred_element_type=F32))
        gb = jnp.where(lane < N_HEADS, gcum, jnp.where(lane < 2 * N_HEADS, beta, 0.0))
        gb_out[rows, :] = gb
        er = lax.broadcasted_iota(jnp.int32, (8, LANES), 0)
        ec = lax.broadcasted_iota(jnp.int32, (8, LANES), 1)
        eye8 = jnp.where(er == ec, 1.0, 0.0).astype(BF16)
        for c in range(tm // CHUNK):
            parts = _split3(gb[c * CHUNK:(c + 1) * CHUNK, :])
            acc = None
            for p in parts:
                t = lax.dot_general(eye8, p, (((1,), (1,)), ((), ())), preferred_element_type=F32)
                acc = t if acc is None else acc + t
            rows_out[a * (tm // CHUNK) + c] = acc
        yield

    @pl.when(i == 0)
    def _():
        tail[...] = jnp.zeros(tail.shape, F32)
        rr = lax.broadcasted_iota(jnp.int32, (tm, tm), 0)
        cc = lax.broadcasted_iota(jnp.int32, (tm, tm), 1)
        tri_s[...] = jnp.where((cc <= rr) & ((cc // CHUNK) == (rr // CHUNK)), 1.0, 0.0).astype(BF16)
        _interleave(front(x0_ref, 0))

    def backs(slot, first_tile):
        yield from back(slot, 0, first_tile)
        yield from back(slot, 1, first_tile + 1)

    _interleave(front(xa_ref, 1), backs(0, 0))
    _interleave(front(xb_ref, 0), backs(1, 2))


def _attend(pairs, qt_ref, k_ref, vt_ref, m_s, l_s, acc_s, st_s, *, tq, nq):
    heads = range(N_HEADS)
    steps = nq + 1
    kk = lax.broadcasted_iota(jnp.int32, (tq, tq), 0)
    qq = lax.broadcasted_iota(jnp.int32, (tq, tq), 1)
    ones = jnp.ones((16, tq), BF16)
    hk = lambda h: slice(2 * HEAD * h, 2 * HEAD * (h + 1))
    hv = lambda h: slice(HEAD * h, HEAD * (h + 1))

    def blocks(s):
        return [(jnp.where(s <= p, p, nq - 1 - p), jnp.where(s <= p, p - s, s - p - 1)) for p in pairs]

    def scores(s, c, h, q, kv):
        r0 = pl.multiple_of(kv * tq, tq)
        st_s[s % 2, c * N_HEADS + h] = jnp.dot(k_ref[pl.ds(r0, tq), hk(h)], qt_ref[q, hk(h), :],
                                               preferred_element_type=F32)

    for c, (q, kv) in enumerate(blocks(0)):
        for h in heads:
            scores(0, c, h, q, kv)
    yield
    for s in range(steps):
        nxt = blocks(s + 1) if s + 1 < steps else None
        for c, (q, kv) in enumerate(blocks(s)):
            for h in heads:
                if nxt is not None:
                    scores(s + 1, c, h, *nxt[c])
                i = q * N_HEADS + h
                st = st_s[s % 2, c * N_HEADS + h]
                if s in (0, steps - 1):
                    st = jnp.where(kk <= qq, st, -jnp.inf)
                m_old = m_s[i]
                m_new = jnp.maximum(m_old, jnp.max(st, axis=0, keepdims=True))
                alpha = jnp.exp2(m_old - m_new)
                p_t = jnp.exp2(st - m_new).astype(BF16)
                pv = jnp.dot(jnp.concatenate([vt_ref[kv, hv(h), :], ones], axis=0), p_t,
                             preferred_element_type=F32)
                m_s[i] = m_new
                l_s[i] = alpha * l_s[i] + pv[HEAD:HEAD + 1]
                acc_s[i] = alpha * acc_s[i] + pv[:HEAD]
            yield


def _attn_kernel(qt_ref, k_ref, vt_ref, w_ref, o_ref, m_s, l_s, acc_s, st_s, *, tq, nq):
    m_s[...] = jnp.full(m_s.shape, -jnp.inf, F32)
    l_s[...] = jnp.zeros(l_s.shape, F32)
    acc_s[...] = jnp.zeros(acc_s.shape, F32)
    n_pairs = nq // 2

    def body(t, carry):
        _interleave(_attend([t, n_pairs - 1 - t], qt_ref, k_ref, vt_ref, m_s, l_s, acc_s, st_s,
                            tq=tq, nq=nq))
        return carry

    lax.fori_loop(0, n_pairs // 2, body, 0)
    rr = lax.broadcasted_iota(jnp.int32, (tq, tq), 0)
    cc = lax.broadcasted_iota(jnp.int32, (tq, tq), 1)
    eye = jnp.where(rr == cc, 1.0, 0.0).astype(BF16)
    for q in range(nq):
        for h in range(N_HEADS):
            i = q * N_HEADS + h
            o_t = acc_s[i] / l_s[i]
            o_t = o_t * lax.rsqrt(jnp.sum(o_t * o_t, axis=0, keepdims=True) * (1.0 / HEAD) + EPS) * w_ref[h]
            o = lax.dot_general(eye, o_t.astype(BF16), (((1,), (1,)), ((), ())),
                                preferred_element_type=F32)
            o_ref[q * tq:(q + 1) * tq, HEAD * h:HEAD * (h + 1)] = o.astype(BF16)


def _gdn_kernel(q_ref, k_ref, v_ref, z_ref, gb_ref, rows_ref, gnw_ref, o_ref,
                s_ref, u_s, wq_s, kd_s, at_s, *, n_chunks, unroll, nb):
    ii = lax.broadcasted_iota(jnp.int32, (CHUNK, CHUNK), 0)
    jj = lax.broadcasted_iota(jnp.int32, (CHUNK, CHUNK), 1)
    eye = jnp.where(ii == jj, 1.0, 0.0)
    heads = range(N_HEADS)
    ring = 2 * unroll
    slot = lambda n, h: (n & (ring - 1)) * N_HEADS + h

    def solve(t, bb):
        ns = [t * unroll + c for c in range(unroll)]
        r0 = [pl.multiple_of(n * CHUNK, CHUNK) for n in ns]
        scal = [gb_ref[bb, pl.ds(r, CHUNK), :] for r in r0]
        rows = [rows_ref[bb, n] for n in ns]
        items = [(c, h) for c in range(unroll) for h in heads]
        cols = lambda h: slice(h * HEAD, (h + 1) * HEAD)
        q = [q_ref[bb, pl.ds(r0[c], CHUNK), cols(h)] for c, h in items]
        k = [k_ref[bb, pl.ds(r0[c], CHUNK), cols(h)] for c, h in items]
        v = [v_ref[bb, pl.ds(r0[c], CHUNK), cols(h)] for c, h in items]
        g_col = [jnp.broadcast_to(scal[c][:, h:h + 1], (CHUNK, LANES)) for c, h in items]
        b_col = [jnp.broadcast_to(scal[c][:, N_HEADS + h:N_HEADS + h + 1], (CHUNK, LANES))
                 for c, h in items]
        g_row = [rows[c][h:h + 1, :] for c, h in items]
        n_it = range(len(items))
        decay = [jnp.exp(jnp.where(ii >= jj, g_col[x][:, :CHUNK] - g_row[x], -jnp.inf)) for x in n_it]
        qkk = [_mm_nt(jnp.concatenate([q[x], k[x]], axis=0), k[x]) for x in n_it]
        yield
        lmat = [jnp.where(ii > jj, b_col[x][:, :CHUNK] * qkk[x][CHUNK:] * decay[x], 0.0) for x in n_it]
        inv = [eye - lmat[x] for x in n_it]
        pw = [_mm(lmat[x], lmat[x]) for x in n_it]
        yield
        for step in range(4):
            both = [_mm(jnp.concatenate([inv[x], pw[x]], axis=0), pw[x]) for x in n_it]
            inv = [inv[x] + both[x][:CHUNK] for x in n_it]
            pw = [both[x][CHUNK:] for x in n_it]
            yield
        inv = [inv[x] + _mm(inv[x], pw[x]) for x in n_it]
        yield
        e_col = [jnp.exp(g_col[x]) for x in n_it]
        rhs = [jnp.concatenate([v[x].astype(F32) * b_col[x], k[x].astype(F32) * (b_col[x] * e_col[x])],
                               axis=1) for x in n_it]
        uw = [_mm(inv[x], rhs[x]) for x in n_it]
        yield
        for x, (c, h) in enumerate(items):
            idx = slot(ns[c], h)
            g_last = g_col[x][CHUNK - 1:CHUNK, :]
            u_s[bb, idx] = uw[x][:, :HEAD]
            wq_s[bb, idx, 0:CHUNK, :] = uw[x][:, HEAD:].astype(BF16)
            wq_s[bb, idx, CHUNK:2 * CHUNK, :] = (q[x].astype(F32) * e_col[x]).astype(BF16)
            kd_s[bb, idx] = (k[x].astype(F32) * jnp.exp(g_last - g_col[x])).astype(BF16)
            at_s[bb, idx] = (qkk[x][:CHUNK] * decay[x]).astype(BF16)

    gnw = gnw_ref[...]

    def scan(t, bb):
        for c in range(unroll):
            n = t * unroll + c
            r0 = pl.multiple_of(n * CHUNK, CHUNK)
            g_last = gb_ref[bb, pl.ds(r0 + CHUNK - 1, 1), :]
            state = [s_ref[bb, h] for h in heads]
            sb = [x.astype(BF16) for x in state]
            ws = [_mm(wq_s[bb, slot(n, h)], sb[h]) for h in heads]
            yield
            v_new = [(u_s[bb, slot(n, h)] - ws[h][:CHUNK]).astype(BF16) for h in heads]
            o = [ws[h][CHUNK:] + _mm(at_s[bb, slot(n, h)], v_new[h]) for h in heads]
            upd = [_mm_tn(kd_s[bb, slot(n, h)], v_new[h]) for h in heads]
            yield
            for h in heads:
                c_dec = jnp.exp(jnp.broadcast_to(g_last[:, h:h + 1], (HEAD, HEAD)))
                s_ref[bb, h] = state[h] * c_dec + upd[h]
                zh = z_ref[bb, pl.ds(r0, CHUNK), h * HEAD:(h + 1) * HEAD].astype(F32)
                gated = _rms(o[h], gnw, HEAD) * (zh * _sigmoid(zh))
                o_ref[bb, pl.ds(r0, CHUNK), h * HEAD:(h + 1) * HEAD] = gated.astype(BF16)

    n_groups = n_chunks // unroll
    batches = range(nb)

    @pl.when(pl.program_id(1) == 0)
    def _():
        s_ref[...] = jnp.zeros_like(s_ref)

    _interleave(*[solve(0, bb) for bb in batches])

    def body(t, carry):
        _interleave(*[solve(t, bb) for bb in batches], *[scan(t - 1, bb) for bb in batches])
        return carry

    lax.fori_loop(1, n_groups, body, 0)
    _interleave(*[scan(n_groups - 1, bb) for bb in batches])


def _mlp_kernel(x_ref, mla_ref, gdn_ref, woa_ref, wob_ref, nw_ref, wup_ref, wdn_ref, o_ref):
    h = (x_ref[...]
         + jnp.dot(mla_ref[...], woa_ref[...], preferred_element_type=F32)
         + jnp.dot(gdn_ref[...], wob_ref[...], preferred_element_type=F32))
    hn = _rms(h, nw_ref[...], D_MODEL).astype(BF16)
    o_ref[...] = h
    for c in range(D_FF // MLP_FC):
        u = jnp.dot(hn, wup_ref[:, c * MLP_FC:(c + 1) * MLP_FC], preferred_element_type=F32)
        a = jnp.square(jnp.maximum(u, 0.0)).astype(BF16)
        o_ref[...] += jnp.dot(a, wdn_ref[c * MLP_FC:(c + 1) * MLP_FC, :], preferred_element_type=F32)


def _const_spec(shape):
    nd = len(shape)
    return pl.BlockSpec(shape, lambda *_: (0,) * nd, pipeline_mode=pl.Buffered(1))


def _pad_lanes(v, width=LANES):
    v = v.reshape(1, -1).astype(F32)
    return jnp.pad(v, ((0, 0), (0, width - v.shape[1])))


def _layer(h, pos, inv_freq, p):
    B, S, _ = h.shape
    T = B * S
    x2 = h.reshape(T, D_MODEL)

    w_in = p["w_in"]
    o_q, o_kv, o_pe = 0, LORA, 2 * LORA
    o_g = o_pe + ROPE
    o_z = o_g + 3 * MIX
    o_a = o_z + MIX
    w_lat = w_in[:, o_q:o_pe].astype(BF16)
    w_kpe = jnp.pad(w_in[:, o_pe:o_g], ((0, 0), (0, LANES - ROPE))).astype(BF16)
    w_g = w_in[:, o_g:o_z].astype(BF16)
    w_z = w_in[:, o_z:o_a].astype(BF16)
    w_ab = jnp.pad(w_in[:, o_a:], ((0, 0), (0, LANES - 2 * N_HEADS))).astype(BF16)
    w_uq = p["w_uq"].reshape(LORA, N_HEADS, QK_HEAD)
    w_uq_nt = w_uq[:, :, :NOPE].reshape(LORA, N_HEADS * NOPE).T.astype(BF16)
    w_uq_pt = w_uq[:, :, NOPE:].reshape(LORA, N_HEADS * ROPE).T.astype(BF16)
    w_ukv = p["w_ukv"].reshape(LORA, N_HEADS, NOPE + HEAD)
    w_uk_k = w_ukv[:, :, :NOPE].reshape(LORA, N_HEADS * NOPE).astype(BF16)
    w_uk_vt = w_ukv[:, :, NOPE:].reshape(LORA, N_HEADS * HEAD).T.astype(BF16)

    tm = PREP_TM
    n_tiles = T // tm
    assert n_tiles % 4 == 0
    n_pairs = n_tiles // 2
    row_spec = lambda w: pl.BlockSpec((4 * tm, w), lambda i: (i, 0))
    col_bcast = lambda v: jnp.broadcast_to(v.astype(F32)[:, None], (v.shape[0], tm))
    x_pair = lambda index: pl.BlockSpec((2 * tm, D_MODEL), index)
    prep_in = [
        (x2, x_pair(lambda i: (0, 0))),
        (x2, x_pair(lambda i: (2 * i + 1, 0))),
        (x2, x_pair(lambda i: (jnp.minimum(2 * i + 2, n_pairs - 1), 0))),
        (pos.reshape(n_tiles, 1, tm), pl.BlockSpec((4, 1, tm), lambda i: (i, 0, 0))),
        (p["attn_norm_w"].reshape(1, D_MODEL), None),
        (w_lat, None), (w_kpe, None), (w_ab, None), (w_g, None), (w_z, None),
        (p["q_lat_norm_w"].reshape(1, LORA), None),
        (p["kv_lat_norm_w"].reshape(1, LORA), None),
        (w_uq_nt, None), (w_uq_pt, None), (w_uk_k, None), (w_uk_vt, None),
        (col_bcast(p["q_norm_w"][:NOPE]), None),
        (col_bcast(p["q_norm_w"][NOPE:]), None),
        (p["k_norm_w"][:NOPE].reshape(1, NOPE), None),
        (_pad_lanes(p["k_norm_w"][NOPE:]), None),
        (col_bcast(inv_freq), None),
        (p["conv_w"], None),
        (_pad_lanes(p["a_log"]), None),
        (_pad_lanes(p["dt_bias"]), None),
    ]
    prep_args = [a for a, _ in prep_in]
    prep_specs = [s if s is not None else _const_spec(a.shape) for a, s in prep_in]
    out_shapes = [
        jax.ShapeDtypeStruct((n_tiles, 2 * MIX, tm), BF16),
        jax.ShapeDtypeStruct((T, 2 * MIX), BF16),
        jax.ShapeDtypeStruct((n_tiles, MIX, tm), BF16),
        jax.ShapeDtypeStruct((T, MIX), BF16),
        jax.ShapeDtypeStruct((T, MIX), BF16),
        jax.ShapeDtypeStruct((T, MIX), BF16),
        jax.ShapeDtypeStruct((T, MIX), BF16),
        jax.ShapeDtypeStruct((T, LANES), F32),
        jax.ShapeDtypeStruct((T // CHUNK, 8, CHUNK), F32),
    ]
    tile_spec = lambda r: pl.BlockSpec((4, r, tm), lambda i: (i, 0, 0))
    out_specs = [tile_spec(2 * MIX), row_spec(2 * MIX), tile_spec(MIX)] + [row_spec(MIX)] * 4 + [
        row_spec(LANES),
        pl.BlockSpec((4 * tm // CHUNK, 8, CHUNK), lambda i: (i, 0, 0)),
    ]
    qt, k, vt, gq, gk, gv, gz, gb, rows = pl.pallas_call(
        functools.partial(_prep_kernel, tm=tm, tiles_per_seq=S // tm),
        grid=(n_tiles // 4,),
        in_specs=prep_specs,
        out_specs=out_specs,
        out_shape=out_shapes,
        scratch_shapes=2 * [
            pltpu.VMEM((2 * tm, 2 * LORA), F32),
            pltpu.VMEM((2 * tm, 3 * MIX), F32),
            pltpu.VMEM((2 * tm, LANES), F32),
            pltpu.VMEM((2 * tm, LANES), F32),
            pltpu.VMEM((2 * tm, MIX), BF16),
        ] + [
            pltpu.VMEM((8, 3 * MIX), F32),
            pltpu.VMEM((tm, tm), BF16),
        ],
        compiler_params=pltpu.CompilerParams(dimension_semantics=("arbitrary",),
                                             vmem_limit_bytes=VMEM_LIMIT),
        name="prep",
    )(*prep_args)

    tq = tm
    nq = S // tq
    assert nq % 4 == 0
    mla_o = pl.pallas_call(
        functools.partial(_attn_kernel, tq=tq, nq=nq),
        grid=(B,),
        in_specs=[
            pl.BlockSpec((nq, 2 * MIX, tq), lambda b: (b, 0, 0)),
            pl.BlockSpec((None, S, 2 * MIX), lambda b: (b, 0, 0)),
            pl.BlockSpec((None, nq, MIX, tq), lambda b: (b, 0, 0, 0)),
            _const_spec((N_HEADS, HEAD, tq)),
        ],
        out_specs=pl.BlockSpec((None, S, MIX), lambda b: (b, 0, 0)),
        out_shape=jax.ShapeDtypeStruct((B, S, MIX), BF16),
        scratch_shapes=[pltpu.VMEM((nq * N_HEADS, 1, tq), F32),
                        pltpu.VMEM((nq * N_HEADS, 1, tq), F32),
                        pltpu.VMEM((nq * N_HEADS, HEAD, tq), F32),
                        pltpu.VMEM((2, 2 * N_HEADS, tq, tq), F32)],
        compiler_params=pltpu.CompilerParams(
            dimension_semantics=("arbitrary",),
            vmem_limit_bytes=VMEM_LIMIT),
        name="attn",
    )(qt, k.reshape(B, S, 2 * MIX), vt.reshape(B, nq, MIX, tq),
      jnp.broadcast_to(p["mla_out_norm_w"].astype(F32)[:, :, None], (N_HEADS, HEAD, tq)))

    n_chunks = S // CHUNK
    nb = GDN_BATCHES
    n_seg = GDN_SEGMENTS
    seg_chunks = n_chunks // n_seg
    assert B % nb == 0 and n_chunks % n_seg == 0 and seg_chunks % GDN_UNROLL == 0
    seq_spec = lambda w: pl.BlockSpec((nb, S // n_seg, w), lambda b, j: (b, j, 0))
    ring_slots = 2 * GDN_UNROLL * N_HEADS
    gdn_o = pl.pallas_call(
        functools.partial(_gdn_kernel, n_chunks=seg_chunks, unroll=GDN_UNROLL, nb=nb),
        grid=(B // nb, n_seg),
        in_specs=[seq_spec(MIX)] * 4 + [
            seq_spec(LANES),
            pl.BlockSpec((nb, seg_chunks, 8, CHUNK), lambda b, j: (b, j, 0, 0)),
            _const_spec((1, HEAD)),
        ],
        out_specs=seq_spec(MIX),
        out_shape=jax.ShapeDtypeStruct((B, S, MIX), BF16),
        scratch_shapes=[
            pltpu.VMEM((nb, N_HEADS, HEAD, HEAD), F32),
            pltpu.VMEM((nb, ring_slots, CHUNK, HEAD), F32),
            pltpu.VMEM((nb, ring_slots, 2 * CHUNK, HEAD), BF16),
            pltpu.VMEM((nb, ring_slots, CHUNK, HEAD), BF16),
            pltpu.VMEM((nb, ring_slots, CHUNK, CHUNK), BF16),
        ],
        compiler_params=pltpu.CompilerParams(dimension_semantics=("arbitrary", "arbitrary"),
                                             vmem_limit_bytes=VMEM_LIMIT),
        name="gdn",
    )(gq.reshape(B, S, MIX), gk.reshape(B, S, MIX), gv.reshape(B, S, MIX), gz.reshape(B, S, MIX),
      gb.reshape(B, S, LANES), rows.reshape(B, n_chunks, 8, CHUNK),
      p["gdn_norm_w"].reshape(1, HEAD))

    tm2 = MLP_TM
    w_out = p["w_out"].astype(BF16)
    tok_spec = lambda w: pl.BlockSpec((tm2, w), lambda i: (i, 0))
    out = pl.pallas_call(
        _mlp_kernel,
        grid=(T // tm2,),
        in_specs=[
            tok_spec(D_MODEL), tok_spec(MIX), tok_spec(MIX),
            _const_spec((MIX, D_MODEL)), _const_spec((MIX, D_MODEL)),
            _const_spec((1, D_MODEL)),
            _const_spec((D_MODEL, D_FF)), _const_spec((D_FF, D_MODEL)),
        ],
        out_specs=tok_spec(D_MODEL),
        out_shape=jax.ShapeDtypeStruct((T, D_MODEL), F32),
        compiler_params=pltpu.CompilerParams(dimension_semantics=("arbitrary",),
                                             vmem_limit_bytes=VMEM_LIMIT),
        name="mlp",
    )(x2, mla_o.reshape(T, MIX), gdn_o.reshape(T, MIX), w_out[:MIX], w_out[MIX:],
      p["mlp_norm_w"].reshape(1, D_MODEL), p["w_up"].astype(BF16), p["w_down"].astype(BF16))
    return out.reshape(B, S, D_MODEL)


def kernel(x, positions, attn_norm_w, w_in, q_lat_norm_w, w_uq, kv_lat_norm_w, w_ukv, q_norm_w,
           k_norm_w, mla_out_norm_w, conv_w, a_log, dt_bias, gdn_norm_w, w_out, mlp_norm_w, w_up,
           w_down):
    B, S, _ = x.shape
    half = ROPE // 2
    inv_freq = ROPE_THETA ** (-jnp.arange(half, dtype=F32) / half)
    params = dict(attn_norm_w=attn_norm_w, w_in=w_in, q_lat_norm_w=q_lat_norm_w, w_uq=w_uq,
                  kv_lat_norm_w=kv_lat_norm_w, w_ukv=w_ukv, q_norm_w=q_norm_w, k_norm_w=k_norm_w,
                  mla_out_norm_w=mla_out_norm_w, conv_w=conv_w, a_log=a_log, dt_bias=dt_bias,
                  gdn_norm_w=gdn_norm_w, w_out=w_out, mlp_norm_w=mlp_norm_w, w_up=w_up,
                  w_down=w_down)
    h = x
    for l in range(attn_norm_w.shape[0]):
        h = _layer(h, positions, inv_freq, {name: val[l] for name, val in params.items()})
    return h
```

```python
import functools

import jax
import jax.numpy as jnp
from jax import lax
from jax.experimental import pallas as pl
from jax.experimental.pallas import tpu as pltpu

F32 = jnp.float32
BF16 = jnp.bfloat16

D_MODEL = 1024
N_HEADS = 4
LORA = 256
NOPE = 128
ROPE = 64
HEAD = 128
QK_HEAD = NOPE + ROPE
ROPE_THETA = 10000.0
CONV_W = 4
CHUNK = 64
D_FF = 4 * D_MODEL
EPS = 1e-6
LOG2E = 1.4426950408889634
LANES = 128
MIX = N_HEADS * HEAD

PREP_TM = 256
GDN_UNROLL = 4
GDN_BATCHES = 1
GDN_SEGMENTS = 1
MLP_TM = 1024
MLP_FC = 1024
VMEM_LIMIT = 56 * 1024 * 1024


def _mm(a, b):
    return jnp.dot(a.astype(BF16), b.astype(BF16), preferred_element_type=F32)


def _mm_nt(a, b):
    return lax.dot_general(a.astype(BF16), b.astype(BF16), (((1,), (1,)), ((), ())),
                           preferred_element_type=F32)


def _mm_tn(a, b):
    return lax.dot_general(a.astype(BF16), b.astype(BF16), (((0,), (0,)), ((), ())),
                           preferred_element_type=F32)


def _split3(x):
    hi = x.astype(BF16)
    r1 = x - hi.astype(F32)
    mid = r1.astype(BF16)
    lo = (r1 - mid.astype(F32)).astype(BF16)
    return hi, mid, lo


def _interleave(*gens):
    live = list(gens)
    while live:
        for g in list(live):
            try:
                next(g)
            except StopIteration:
                live.remove(g)


def _rms(x, w, n):
    return x * lax.rsqrt(jnp.sum(x * x, axis=-1, keepdims=True) * (1.0 / n) + EPS) * w


def _sigmoid(x):
    return 1.0 / (1.0 + jnp.exp(-x))


def _rope(x, cosf, sinf, lane):
    rot = jnp.where(lane < ROPE // 2, pltpu.roll(x, LANES - ROPE // 2, 1),
                    pltpu.roll(x, ROPE // 2, 1))
    return x * cosf + rot * sinf


def _prep_kernel(x0_ref, xa_ref, xb_ref, posr_ref, anw_ref, wlat_ref, wkpe_ref, wab_ref, wg_ref, wz_ref,
                 qlnw_ref, kvlnw_ref, wuqnt_ref, wuqpt_ref, wukk_ref, wukvt_ref,
                 qnwn_ref, qnwp_ref, knwn_ref, knwp_ref, invft_ref, convw_ref,
                 alog_ref, dtb_ref,
                 qt_out, k_out, vt_out, gq_out, gk_out, gv_out, z_out, gb_out, rows_out,
                 lat_s0, g_s0, kpe_s0, ab_s0, z_s0, lat_s1, g_s1, kpe_s1, ab_s1, z_s1, tail, tri_s,
                 *, tm, tiles_per_seq):
    i = pl.program_id(0)
    half = ROPE // 2

    sets = ((lat_s0, g_s0, kpe_s0, ab_s0, z_s0), (lat_s1, g_s1, kpe_s1, ab_s1, z_s1))

    def front(x_ref, slot):
        lat_s, g_s, kpe_s, ab_s, z_s = sets[slot]
        xn = _rms(x_ref[...], anw_ref[...], D_MODEL).astype(BF16)
        yield
        step = 2 * LANES
        for w_ref, dst in ((wlat_ref, lat_s), (wg_ref, g_s)):
            for c0 in range(0, w_ref.shape[1], step):
                dst[:, c0:c0 + step] = jnp.dot(xn, w_ref[:, c0:c0 + step], preferred_element_type=F32)
                yield
        kpe_s[...] = jnp.dot(xn, wkpe_ref[...], preferred_element_type=F32)
        ab_s[...] = jnp.dot(xn, wab_ref[...], preferred_element_type=F32)
        yield
        for c0 in range(0, MIX, step):
            z_s[:, c0:c0 + step] = jnp.dot(xn, wz_ref[:, c0:c0 + step],
                                           preferred_element_type=F32).astype(BF16)
            yield

    def back(slot, sub, a):
        lat_s, g_s, kpe_s, ab_s, z_s = sets[slot]
        rows = slice(a * tm, (a + 1) * tm)
        src = slice(sub * tm, (sub + 1) * tm)
        z_out[rows, :] = z_s[src, :]
        ang_t = invft_ref[...] * posr_ref[a].astype(F32)
        cos_t = jnp.cos(ang_t)
        sin_t = jnp.sin(ang_t)
        lane = lax.broadcasted_iota(jnp.int32, (tm, LANES), 1)
        table = jnp.concatenate([cos_t, sin_t, jnp.zeros((LANES - ROPE, tm), F32)], axis=0).T
        cosf = jnp.where(lane < half, table, jnp.where(lane < ROPE, pltpu.roll(table, half, 1), 0.0))
        sinf = jnp.where(lane < half, -pltpu.roll(table, LANES - half, 1),
                         jnp.where(lane < ROPE, table, 0.0))
        yield

        qn = _rms(lat_s[src, :LORA], qlnw_ref[...], LORA)
        kvn = _rms(lat_s[src, LORA:], kvlnw_ref[...], LORA)
        qn_t = qn.T.astype(BF16)
        kvn_t = kvn.T.astype(BF16)
        kvn = kvn.astype(BF16)
        qt_nope = jnp.dot(wuqnt_ref[...], qn_t, preferred_element_type=F32)
        qt_pe = jnp.dot(wuqpt_ref[...], qn_t, preferred_element_type=F32)
        vt_out[a] = jnp.dot(wukvt_ref[...], kvn_t, preferred_element_type=F32).astype(BF16)
        k_nope = jnp.dot(kvn, wukk_ref[...], preferred_element_type=F32)
        k_pe = _rope(_rms(kpe_s[src, :], knwp_ref[...], ROPE), cosf, sinf, lane).astype(BF16)
        yield

        scale = QK_HEAD ** -0.5 * LOG2E
        qwn = qnwn_ref[...] * scale
        qwp = qnwp_ref[...] * scale
        for h in range(N_HEADS):
            lo = h * HEAD
            base = h * 2 * HEAD
            xn = qt_nope[lo:lo + NOPE, :]
            xn = xn * lax.rsqrt(jnp.sum(xn * xn, axis=0, keepdims=True) * (1.0 / NOPE) + EPS) * qwn
            xp = qt_pe[h * ROPE:(h + 1) * ROPE, :]
            xp = xp * lax.rsqrt(jnp.sum(xp * xp, axis=0, keepdims=True) * (1.0 / ROPE) + EPS) * qwp
            t1, t2 = xp[:half], xp[half:]
            qt_out[a, base:base + NOPE, :] = xn.astype(BF16)
            qt_out[a, base + NOPE:base + NOPE + half, :] = (t1 * cos_t - t2 * sin_t).astype(BF16)
            qt_out[a, base + NOPE + half:base + QK_HEAD, :] = (t2 * cos_t + t1 * sin_t).astype(BF16)
            qt_out[a, base + QK_HEAD:base + 2 * HEAD, :] = jnp.zeros((2 * HEAD - QK_HEAD, tm), BF16)
            k_out[rows, base:base + HEAD] = _rms(k_nope[:, lo:lo + HEAD], knwn_ref[...], NOPE).astype(BF16)
            k_out[rows, base + HEAD:base + 2 * HEAD] = k_pe
        yield

        seq_start = (4 * i + a) % tiles_per_seq == 0
        row8 = lax.broadcasted_iota(jnp.int32, (8, HEAD), 0)
        outs = ((gq_out, HEAD ** -0.5), (gk_out, 1.0), (gv_out, None))
        for part, (dst, norm_scale) in enumerate(outs):
            for h in range(N_HEADS):
                cols = slice(part * MIX + h * HEAD, part * MIX + (h + 1) * HEAD)
                g_blk = g_s[src, cols]
                prev = jnp.where(seq_start, 0.0, tail[:, cols])
                conv = convw_ref[CONV_W - 1:CONV_W, cols] * g_blk
                for s in range(1, CONV_W):
                    rolled = pltpu.roll(g_blk, s, 0)
                    first = jnp.where(row8 < s, pltpu.roll(prev, s, 0), rolled[:8])
                    shifted = jnp.concatenate([first, rolled[8:]], axis=0)
                    conv = conv + convw_ref[CONV_W - 1 - s:CONV_W - s, cols] * shifted
                tail[:, cols] = g_blk[tm - 8:tm, :]
                half_conv = 0.5 * conv
                act = half_conv + half_conv * jnp.tanh(half_conv)
                if norm_scale is not None:
                    act = act * (lax.rsqrt(jnp.sum(act * act, axis=-1, keepdims=True) + EPS) * norm_scale)
                dst[rows, h * HEAD:(h + 1) * HEAD] = act.astype(BF16)
            yield

        ab = ab_s[src, :]
        sp_in = ab + dtb_ref[...]
        softplus = jnp.maximum(sp_in, 0.0) + jnp.log(1.0 + jnp.exp(-jnp.abs(sp_in)))
        g = -jnp.exp(alog_ref[...]) * softplus
        beta = _sigmoid(ab)
        tri = tri_s[...]
        g_hi, g_mid, g_lo = _split3(g)
        gcum = (jnp.dot(tri, g_hi, preferred_element_type=F32)
                + jnp.dot(tri, g_mid, preferred_element_type=F32)
                + jnp.dot(tri, g_lo, preferred_element_type=F32))
        gb = jnp.where(lane < N_HEADS, gcum, jnp.where(lane < 2 * N_HEADS, beta, 0.0))
        gb_out[rows, :] = gb
        er = lax.broadcasted_iota(jnp.int32, (8, LANES), 0)
        ec = lax.broadcasted_iota(jnp.int32, (8, LANES), 1)
        eye8 = jnp.where(er == ec, 1.0, 0.0).astype(BF16)
        for c in range(tm // CHUNK):
            parts = _split3(gb[c * CHUNK:(c + 1) * CHUNK, :])
            acc = None
            for p in parts:
                t = lax.dot_general(eye8, p, (((1,), (1,)), ((), ())), preferred_element_type=F32)
                acc = t if acc is None else acc + t
            rows_out[a * (tm // CHUNK) + c] = acc

    @pl.when(i == 0)
    def _():
        tail[...] = jnp.zeros(tail.shape, F32)
        rr = lax.broadcasted_iota(jnp.int32, (tm, tm), 0)
        cc = lax.broadcasted_iota(jnp.int32, (tm, tm), 1)
        tri_s[...] = jnp.where((cc <= rr) & ((cc // CHUNK) == (rr // CHUNK)), 1.0, 0.0).astype(BF16)
        _interleave(front(x0_ref, 0))

    def backs(slot, first_tile):
        yield from back(slot, 0, first_tile)
        yield from back(slot, 1, first_tile + 1)

    _interleave(front(xa_ref, 1), backs(0, 0))
    _interleave(front(xb_ref, 0), backs(1, 2))


def _attend(pairs, qt_ref, k_ref, vt_ref, m_s, l_s, acc_s, st_s, *, tq, nq):
    heads = range(N_HEADS)
    steps = nq + 1
    kk = lax.broadcasted_iota(jnp.int32, (tq, tq), 0)
    qq = lax.broadcasted_iota(jnp.int32, (tq, tq), 1)
    ones = jnp.ones((16, tq), BF16)
    hk = lambda h: slice(2 * HEAD * h, 2 * HEAD * (h + 1))
    hv = lambda h: slice(HEAD * h, HEAD * (h + 1))

    def blocks(s):
        return [(jnp.where(s <= p, p, nq - 1 - p), jnp.where(s <= p, p - s, s - p - 1)) for p in pairs]

    def scores(s, c, h, q, kv):
        r0 = pl.multiple_of(kv * tq, tq)
        st_s[s % 2, c * N_HEADS + h] = jnp.dot(k_ref[pl.ds(r0, tq), hk(h)], qt_ref[q, hk(h), :],
                                               preferred_element_type=F32)

    for c, (q, kv) in enumerate(blocks(0)):
        for h in heads:
            scores(0, c, h, q, kv)
    yield
    for s in range(steps):
        nxt = blocks(s + 1) if s + 1 < steps else None
        for c, (q, kv) in enumerate(blocks(s)):
            for h in heads:
                if nxt is not None:
                    scores(s + 1, c, h, *nxt[c])
                i = q * N_HEADS + h
                st = st_s[s % 2, c * N_HEADS + h]
                if s in (0, steps - 1):
                    st = jnp.where(kk <= qq, st, -jnp.inf)
                m_old = m_s[i]
                m_new = jnp.maximum(m_old, jnp.max(st, axis=0, keepdims=True))
                alpha = jnp.exp2(m_old - m_new)
                p_t = jnp.exp2(st - m_new).astype(BF16)
                pv = jnp.dot(jnp.concatenate([vt_ref[kv, hv(h), :], ones], axis=0), p_t,
                             preferred_element_type=F32)
                m_s[i] = m_new
                l_s[i] = alpha * l_s[i] + pv[HEAD:HEAD + 1]
                acc_s[i] = alpha * acc_s[i] + pv[:HEAD]
            yield


def _attn_kernel(qt_ref, k_ref, vt_ref, w_ref, o_ref, m_s, l_s, acc_s, st_s, *, tq, nq):
    m_s[...] = jnp.full(m_s.shape, -jnp.inf, F32)
    l_s[...] = jnp.zeros(l_s.shape, F32)
    acc_s[...] = jnp.zeros(acc_s.shape, F32)
    n_pairs = nq // 2

    def body(t, carry):
        _interleave(_attend([t, n_pairs - 1 - t], qt_ref, k_ref, vt_ref, m_s, l_s, acc_s, st_s,
                            tq=tq, nq=nq))
        return carry

    lax.fori_loop(0, n_pairs // 2, body, 0)
    rr = lax.broadcasted_iota(jnp.int32, (tq, tq), 0)
    cc = lax.broadcasted_iota(jnp.int32, (tq, tq), 1)
    eye = jnp.where(rr == cc, 1.0, 0.0).astype(BF16)
    for q in range(nq):
        for h in range(N_HEADS):
            i = q * N_HEADS + h
            o_t = acc_s[i] / l_s[i]
            o_t = o_t * lax.rsqrt(jnp.sum(o_t * o_t, axis=0, keepdims=True) * (1.0 / HEAD) + EPS) * w_ref[h]
            o = lax.dot_general(eye, o_t.astype(BF16), (((1,), (1,)), ((), ())),
                                preferred_element_type=F32)
            o_ref[q * tq:(q + 1) * tq, HEAD * h:HEAD * (h + 1)] = o.astype(BF16)


def _gdn_kernel(q_ref, k_ref, v_ref, z_ref, gb_ref, rows_ref, gnw_ref, o_ref,
                s_ref, u_s, wq_s, kd_s, at_s, *, n_chunks, unroll, nb):
    ii = lax.broadcasted_iota(jnp.int32, (CHUNK, CHUNK), 0)
    jj = lax.broadcasted_iota(jnp.int32, (CHUNK, CHUNK), 1)
    eye = jnp.where(ii == jj, 1.0, 0.0)
    heads = range(N_HEADS)
    ring = 2 * unroll
    slot = lambda n, h: (n & (ring - 1)) * N_HEADS + h

    def solve(t, bb):
        ns = [t * unroll + c for c in range(unroll)]
        r0 = [pl.multiple_of(n * CHUNK, CHUNK) for n in ns]
        scal = [gb_ref[bb, pl.ds(r, CHUNK), :] for r in r0]
        rows = [rows_ref[bb, n] for n in ns]
        items = [(c, h) for c in range(unroll) for h in heads]
        cols = lambda h: slice(h * HEAD, (h + 1) * HEAD)
        q = [q_ref[bb, pl.ds(r0[c], CHUNK), cols(h)] for c, h in items]
        k = [k_ref[bb, pl.ds(r0[c], CHUNK), cols(h)] for c, h in items]
        v = [v_ref[bb, pl.ds(r0[c], CHUNK), cols(h)] for c, h in items]
        g_col = [jnp.broadcast_to(scal[c][:, h:h + 1], (CHUNK, LANES)) for c, h in items]
        b_col = [jnp.broadcast_to(scal[c][:, N_HEADS + h:N_HEADS + h + 1], (CHUNK, LANES))
                 for c, h in items]
        g_row = [rows[c][h:h + 1, :] for c, h in items]
        n_it = range(len(items))
        decay = [jnp.exp(jnp.where(ii >= jj, g_col[x][:, :CHUNK] - g_row[x], -jnp.inf)) for x in n_it]
        qkk = [_mm_nt(jnp.concatenate([q[x], k[x]], axis=0), k[x]) for x in n_it]
        yield
        lmat = [jnp.where(ii > jj, b_col[x][:, :CHUNK] * qkk[x][CHUNK:] * decay[x], 0.0) for x in n_it]
        inv = [eye - lmat[x] for x in n_it]
        pw = [_mm(lmat[x], lmat[x]) for x in n_it]
        yield
        for step in range(4):
            both = [_mm(jnp.concatenate([inv[x], pw[x]], axis=0), pw[x]) for x in n_it]
            inv = [inv[x] + both[x][:CHUNK] for x in n_it]
            pw = [both[x][CHUNK:] for x in n_it]
            yield
        inv = [inv[x] + _mm(inv[x], pw[x]) for x in n_it]
        yield
        e_col = [jnp.exp(g_col[x]) for x in n_it]
        rhs = [jnp.concatenate([v[x].astype(F32) * b_col[x], k[x].astype(F32) * (b_col[x] * e_col[x])],
                               axis=1) for x in n_it]
        uw = [_mm(inv[x], rhs[x]) for x in n_it]
        yield
        for x, (c, h) in enumerate(items):
            idx = slot(ns[c], h)
            g_last = g_col[x][CHUNK - 1:CHUNK, :]
            u_s[bb, idx] = uw[x][:, :HEAD]
            wq_s[bb, idx, 0:CHUNK, :] = uw[x][:, HEAD:].astype(BF16)
            wq_s[bb, idx, CHUNK:2 * CHUNK, :] = (q[x].astype(F32) * e_col[x]).astype(BF16)
            kd_s[bb, idx] = (k[x].astype(F32) * jnp.exp(g_last - g_col[x])).astype(BF16)
            at_s[bb, idx] = (qkk[x][:CHUNK] * decay[x]).astype(BF16)

    gnw = gnw_ref[...]

    def scan(t, bb):
        for c in range(unroll):
            n = t * unroll + c
            r0 = pl.multiple_of(n * CHUNK, CHUNK)
            g_last = gb_ref[bb, pl.ds(r0 + CHUNK - 1, 1), :]
            state = [s_ref[bb, h] for h in heads]
            sb = [x.astype(BF16) for x in state]
            ws = [_mm(wq_s[bb, slot(n, h)], sb[h]) for h in heads]
            yield
            v_new = [(u_s[bb, slot(n, h)] - ws[h][:CHUNK]).astype(BF16) for h in heads]
            o = [ws[h][CHUNK:] + _mm(at_s[bb, slot(n, h)], v_new[h]) for h in heads]
            upd = [_mm_tn(kd_s[bb, slot(n, h)], v_new[h]) for h in heads]
            yield
            for h in heads:
                c_dec = jnp.exp(jnp.broadcast_to(g_last[:, h:h + 1], (HEAD, HEAD)))
                s_ref[bb, h] = state[h] * c_dec + upd[h]
                zh = z_ref[bb, pl.ds(r0, CHUNK), h * HEAD:(h + 1) * HEAD].astype(F32)
                gated = _rms(o[h], gnw, HEAD) * (zh * _sigmoid(zh))
                o_ref[bb, pl.ds(r0, CHUNK), h * HEAD:(h + 1) * HEAD] = gated.astype(BF16)

    n_groups = n_chunks // unroll
    batches = range(nb)

    @pl.when(pl.program_id(1) == 0)
    def _():
        s_ref[...] = jnp.zeros_like(s_ref)

    _interleave(*[solve(0, bb) for bb in batches])

    def body(t, carry):
        _interleave(*[solve(t, bb) for bb in batches], *[scan(t - 1, bb) for bb in batches])
        return carry

    lax.fori_loop(1, n_groups, body, 0)
    _interleave(*[scan(n_groups - 1, bb) for bb in batches])


def _mlp_kernel(x_ref, mla_ref, gdn_ref, woa_ref, wob_ref, nw_ref, wup_ref, wdn_ref, o_ref):
    h = (x_ref[...]
         + jnp.dot(mla_ref[...], woa_ref[...], preferred_element_type=F32)
         + jnp.dot(gdn_ref[...], wob_ref[...], preferred_element_type=F32))
    hn = _rms(h, nw_ref[...], D_MODEL).astype(BF16)
    o_ref[...] = h
    for c in range(D_FF // MLP_FC):
        u = jnp.dot(hn, wup_ref[:, c * MLP_FC:(c + 1) * MLP_FC], preferred_element_type=F32)
        a = jnp.square(jnp.maximum(u, 0.0)).astype(BF16)
        o_ref[...] += jnp.dot(a, wdn_ref[c * MLP_FC:(c + 1) * MLP_FC, :], preferred_element_type=F32)


def _const_spec(shape):
    nd = len(shape)
    return pl.BlockSpec(shape, lambda *_: (0,) * nd, pipeline_mode=pl.Buffered(1))


def _pad_lanes(v, width=LANES):
    v = v.reshape(1, -1).astype(F32)
    return jnp.pad(v, ((0, 0), (0, width - v.shape[1])))


def _layer(h, pos, inv_freq, p):
    B, S, _ = h.shape
    T = B * S
    x2 = h.reshape(T, D_MODEL)

    w_in = p["w_in"]
    o_q, o_kv, o_pe = 0, LORA, 2 * LORA
    o_g = o_pe + ROPE
    o_z = o_g + 3 * MIX
    o_a = o_z + MIX
    w_lat = w_in[:, o_q:o_pe].astype(BF16)
    w_kpe = jnp.pad(w_in[:, o_pe:o_g], ((0, 0), (0, LANES - ROPE))).astype(BF16)
    w_g = w_in[:, o_g:o_z].astype(BF16)
    w_z = w_in[:, o_z:o_a].astype(BF16)
    w_ab = jnp.pad(w_in[:, o_a:], ((0, 0), (0, LANES - 2 * N_HEADS))).astype(BF16)
    w_uq = p["w_uq"].reshape(LORA, N_HEADS, QK_HEAD)
    w_uq_nt = w_uq[:, :, :NOPE].reshape(LORA, N_HEADS * NOPE).T.astype(BF16)
    w_uq_pt = w_uq[:, :, NOPE:].reshape(LORA, N_HEADS * ROPE).T.astype(BF16)
    w_ukv = p["w_ukv"].reshape(LORA, N_HEADS, NOPE + HEAD)
    w_uk_k = w_ukv[:, :, :NOPE].reshape(LORA, N_HEADS * NOPE).astype(BF16)
    w_uk_vt = w_ukv[:, :, NOPE:].reshape(LORA, N_HEADS * HEAD).T.astype(BF16)

    tm = PREP_TM
    n_tiles = T // tm
    assert n_tiles % 4 == 0
    n_pairs = n_tiles // 2
    row_spec = lambda w: pl.BlockSpec((4 * tm, w), lambda i: (i, 0))
    col_bcast = lambda v: jnp.broadcast_to(v.astype(F32)[:, None], (v.shape[0], tm))
    x_pair = lambda index: pl.BlockSpec((2 * tm, D_MODEL), index)
    prep_in = [
        (x2, x_pair(lambda i: (0, 0))),
        (x2, x_pair(lambda i: (2 * i + 1, 0))),
        (x2, x_pair(lambda i: (jnp.minimum(2 * i + 2, n_pairs - 1), 0))),
        (pos.reshape(n_tiles, 1, tm), pl.BlockSpec((4, 1, tm), lambda i: (i, 0, 0))),
        (p["attn_norm_w"].reshape(1, D_MODEL), None),
        (w_lat, None), (w_kpe, None), (w_ab, None), (w_g, None), (w_z, None),
        (p["q_lat_norm_w"].reshape(1, LORA), None),
        (p["kv_lat_norm_w"].reshape(1, LORA), None),
        (w_uq_nt, None), (w_uq_pt, None), (w_uk_k, None), (w_uk_vt, None),
        (col_bcast(p["q_norm_w"][:NOPE]), None),
        (col_bcast(p["q_norm_w"][NOPE:]), None),
        (p["k_norm_w"][:NOPE].reshape(1, NOPE), None),
        (_pad_lanes(p["k_norm_w"][NOPE:]), None),
        (col_bcast(inv_freq), None),
        (p["conv_w"], None),
        (_pad_lanes(p["a_log"]), None),
        (_pad_lanes(p["dt_bias"]), None),
    ]
    prep_args = [a for a, _ in prep_in]
    prep_specs = [s if s is not None else _const_spec(a.shape) for a, s in prep_in]
    out_shapes = [
        jax.ShapeDtypeStruct((n_tiles, 2 * MIX, tm), BF16),
        jax.ShapeDtypeStruct((T, 2 * MIX), BF16),
        jax.ShapeDtypeStruct((n_tiles, MIX, tm), BF16),
        jax.ShapeDtypeStruct((T, MIX), BF16),
        jax.ShapeDtypeStruct((T, MIX), BF16),
        jax.ShapeDtypeStruct((T, MIX), BF16),
        jax.ShapeDtypeStruct((T, MIX), BF16),
        jax.ShapeDtypeStruct((T, LANES), F32),
        jax.ShapeDtypeStruct((T // CHUNK, 8, CHUNK), F32),
    ]
    tile_spec = lambda r: pl.BlockSpec((4, r, tm), lambda i: (i, 0, 0))
    out_specs = [tile_spec(2 * MIX), row_spec(2 * MIX), tile_spec(MIX)] + [row_spec(MIX)] * 4 + [
        row_spec(LANES),
        pl.BlockSpec((4 * tm // CHUNK, 8, CHUNK), lambda i: (i, 0, 0)),
    ]
    qt, k, vt, gq, gk, gv, gz, gb, rows = pl.pallas_call(
        functools.partial(_prep_kernel, tm=tm, tiles_per_seq=S // tm),
        grid=(n_tiles // 4,),
        in_specs=prep_specs,
        out_specs=out_specs,
        out_shape=out_shapes,
        scratch_shapes=2 * [
            pltpu.VMEM((2 * tm, 2 * LORA), F32),
            pltpu.VMEM((2 * tm, 3 * MIX), F32),
            pltpu.VMEM((2 * tm, LANES), F32),
            pltpu.VMEM((2 * tm, LANES), F32),
            pltpu.VMEM((2 * tm, MIX), BF16),
        ] + [
            pltpu.VMEM((8, 3 * MIX), F32),
            pltpu.VMEM((tm, tm), BF16),
        ],
        compiler_params=pltpu.CompilerParams(dimension_semantics=("arbitrary",),
                                             vmem_limit_bytes=VMEM_LIMIT),
        name="prep",
    )(*prep_args)

    tq = tm
    nq = S // tq
    assert nq % 4 == 0
    mla_o = pl.pallas_call(
        functools.partial(_attn_kernel, tq=tq, nq=nq),
        grid=(B,),
        in_specs=[
            pl.BlockSpec((nq, 2 * MIX, tq), lambda b: (b, 0, 0)),
            pl.BlockSpec((None, S, 2 * MIX), lambda b: (b, 0, 0)),
            pl.BlockSpec((None, nq, MIX, tq), lambda b: (b, 0, 0, 0)),
            _const_spec((N_HEADS, HEAD, tq)),
        ],
        out_specs=pl.BlockSpec((None, S, MIX), lambda b: (b, 0, 0)),
        out_shape=jax.ShapeDtypeStruct((B, S, MIX), BF16),
        scratch_shapes=[pltpu.VMEM((nq * N_HEADS, 1, tq), F32),
                        pltpu.VMEM((nq * N_HEADS, 1, tq), F32),
                        pltpu.VMEM((nq * N_HEADS, HEAD, tq), F32),
                        pltpu.VMEM((2, 2 * N_HEADS, tq, tq), F32)],
        compiler_params=pltpu.CompilerParams(
            dimension_semantics=("arbitrary",),
            vmem_limit_bytes=VMEM_LIMIT),
        name="attn",
    )(qt, k.reshape(B, S, 2 * MIX), vt.reshape(B, nq, MIX, tq),
      jnp.broadcast_to(p["mla_out_norm_w"].astype(F32)[:, :, None], (N_HEADS, HEAD, tq)))

    n_chunks = S // CHUNK
    nb = GDN_BATCHES
    n_seg = GDN_SEGMENTS
    seg_chunks = n_chunks // n_seg
    assert B % nb == 0 and n_chunks % n_seg == 0 and seg_chunks % GDN_UNROLL == 0
    seq_spec = lambda w: pl.BlockSpec((nb, S // n_seg, w), lambda b, j: (b, j, 0))
    ring_slots = 2 * GDN_UNROLL * N_HEADS
    gdn_o = pl.pallas_call(
        functools.partial(_gdn_kernel, n_chunks=seg_chunks, unroll=GDN_UNROLL, nb=nb),
        grid=(B // nb, n_seg),
        in_specs=[seq_spec(MIX)] * 4 + [
            seq_spec(LANES),
            pl.BlockSpec((nb, seg_chunks, 8, CHUNK), lambda b, j: (b, j, 0, 0)),
            _const_spec((1, HEAD)),
        ],
        out_specs=seq_spec(MIX),
        out_shape=jax.ShapeDtypeStruct((B, S, MIX), BF16),
        scratch_shapes=[
            pltpu.VMEM((nb, N_HEADS, HEAD, HEAD), F32),
            pltpu.VMEM((nb, ring_slots, CHUNK, HEAD), F32),
            pltpu.VMEM((nb, ring_slots, 2 * CHUNK, HEAD), BF16),
            pltpu.VMEM((nb, ring_slots, CHUNK, HEAD), BF16),
            pltpu.VMEM((nb, ring_slots, CHUNK, CHUNK), BF16),
        ],
        compiler_params=pltpu.CompilerParams(dimension_semantics=("arbitrary", "arbitrary"),
                                             vmem_limit_bytes=VMEM_LIMIT),
        name="gdn",
    )(gq.reshape(B, S, MIX), gk.reshape(B, S, MIX), gv.reshape(B, S, MIX), gz.reshape(B, S, MIX),
      gb.reshape(B, S, LANES), rows.reshape(B, n_chunks, 8, CHUNK),
      p["gdn_norm_w"].reshape(1, HEAD))

    tm2 = MLP_TM
    w_out = p["w_out"].astype(BF16)
    tok_spec = lambda w: pl.BlockSpec((tm2, w), lambda i: (i, 0))
    out = pl.pallas_call(
        _mlp_kernel,
        grid=(T // tm2,),
        in_specs=[
            tok_spec(D_MODEL), tok_spec(MIX), tok_spec(MIX),
            _const_spec((MIX, D_MODEL)), _const_spec((MIX, D_MODEL)),
            _const_spec((1, D_MODEL)),
            _const_spec((D_MODEL, D_FF)), _const_spec((D_FF, D_MODEL)),
        ],
        out_specs=tok_spec(D_MODEL),
        out_shape=jax.ShapeDtypeStruct((T, D_MODEL), F32),
        compiler_params=pltpu.CompilerParams(dimension_semantics=("arbitrary",),
                                             vmem_limit_bytes=VMEM_LIMIT),
        name="mlp",
    )(x2, mla_o.reshape(T, MIX), gdn_o.reshape(T, MIX), w_out[:MIX], w_out[MIX:],
      p["mlp_norm_w"].reshape(1, D_MODEL), p["w_up"].astype(BF16), p["w_down"].astype(BF16))
    return out.reshape(B, S, D_MODEL)


def kernel(x, positions, attn_norm_w, w_in, q_lat_norm_w, w_uq, kv_lat_norm_w, w_ukv, q_norm_w,
           k_norm_w, mla_out_norm_w, conv_w, a_log, dt_bias, gdn_norm_w, w_out, mlp_norm_w, w_up,
           w_down):
    B, S, _ = x.shape
    half = ROPE // 2
    inv_freq = ROPE_THETA ** (-jnp.arange(half, dtype=F32) / half)
    params = dict(attn_norm_w=attn_norm_w, w_in=w_in, q_lat_norm_w=q_lat_norm_w, w_uq=w_uq,
                  kv_lat_norm_w=kv_lat_norm_w, w_ukv=w_ukv, q_norm_w=q_norm_w, k_norm_w=k_norm_w,
                  mla_out_norm_w=mla_out_norm_w, conv_w=conv_w, a_log=a_log, dt_bias=dt_bias,
                  gdn_norm_w=gdn_norm_w, w_out=w_out, mlp_norm_w=mlp_norm_w, w_up=w_up,
                  w_down=w_down)
    h = x
    for l in range(attn_norm_w.shape[0]):
        h = _layer(h, positions, inv_freq, {name: val[l] for name, val in params.items()})
    return h
```

```python
import functools

import jax
import jax.numpy as jnp
from jax import lax
from jax.experimental import pallas as pl
from jax.experimental.pallas import tpu as pltpu

F32 = jnp.float32
BF16 = jnp.bfloat16

D_MODEL = 1024
N_HEADS = 4
LORA = 256
NOPE = 128
ROPE = 64
HEAD = 128
QK_HEAD = NOPE + ROPE
ROPE_THETA = 10000.0
CONV_W = 4
CHUNK = 64
D_FF = 4 * D_MODEL
EPS = 1e-6
LOG2E = 1.4426950408889634
LANES = 128
MIX = N_HEADS * HEAD

PREP_TM = 256
GDN_UNROLL = 4
GDN_BATCHES = 1
GDN_SEGMENTS = 1
MLP_TM = 1024
MLP_FC = 1024
VMEM_LIMIT = 56 * 1024 * 1024


def _mm(a, b):
    return jnp.dot(a.astype(BF16), b.astype(BF16), preferred_element_type=F32)


def _mm_nt(a, b):
    return lax.dot_general(a.astype(BF16), b.astype(BF16), (((1,), (1,)), ((), ())),
                           preferred_element_type=F32)


def _mm_tn(a, b):
    return lax.dot_general(a.astype(BF16), b.astype(BF16), (((0,), (0,)), ((), ())),
                           preferred_element_type=F32)


def _split3(x):
    hi = x.astype(BF16)
    r1 = x - hi.astype(F32)
    mid = r1.astype(BF16)
    lo = (r1 - mid.astype(F32)).astype(BF16)
    return hi, mid, lo


def _interleave(*gens):
    live = list(gens)
    while live:
        for g in list(live):
            try:
                next(g)
            except StopIteration:
                live.remove(g)


def _rms(x, w, n):
    return x * lax.rsqrt(jnp.sum(x * x, axis=-1, keepdims=True) * (1.0 / n) + EPS) * w


def _sigmoid(x):
    return 1.0 / (1.0 + jnp.exp(-x))


def _rope(x, cosf, sinf, lane):
    rot = jnp.where(lane < ROPE // 2, pltpu.roll(x, LANES - ROPE // 2, 1),
                    pltpu.roll(x, ROPE // 2, 1))
    return x * cosf + rot * sinf


def _prep_kernel(x0_ref, xa_ref, xb_ref, posr_ref, anw_ref, wlat_ref, wkpe_ref, wab_ref, wg_ref, wz_ref,
                 qlnw_ref, kvlnw_ref, wuqnt_ref, wuqpt_ref, wukk_ref, wukvt_ref,
                 qnwn_ref, qnwp_ref, knwn_ref, knwp_ref, invft_ref, convw_ref,
                 alog_ref, dtb_ref,
                 qt_out, k_out, vt_out, gq_out, gk_out, gv_out, z_out, gb_out, rows_out,
                 lat_s0, g_s0, kpe_s0, ab_s0, z_s0, lat_s1, g_s1, kpe_s1, ab_s1, z_s1, tail, tri_s,
                 *, tm, tiles_per_seq):
    i = pl.program_id(0)
    half = ROPE // 2

    sets = ((lat_s0, g_s0, kpe_s0, ab_s0, z_s0), (lat_s1, g_s1, kpe_s1, ab_s1, z_s1))

    def front(x_ref, slot):
        lat_s, g_s, kpe_s, ab_s, z_s = sets[slot]
        xn = _rms(x_ref[...], anw_ref[...], D_MODEL).astype(BF16)
        yield
        step = 2 * LANES
        for w_ref, dst in ((wlat_ref, lat_s), (wg_ref, g_s)):
            for c0 in range(0, w_ref.shape[1], step):
                dst[:, c0:c0 + step] = jnp.dot(xn, w_ref[:, c0:c0 + step], preferred_element_type=F32)
                yield
        kpe_s[...] = jnp.dot(xn, wkpe_ref[...], preferred_element_type=F32)
        ab_s[...] = jnp.dot(xn, wab_ref[...], preferred_element_type=F32)
        yield
        for c0 in range(0, MIX, step):
            z_s[:, c0:c0 + step] = jnp.dot(xn, wz_ref[:, c0:c0 + step],
                                           preferred_element_type=F32).astype(BF16)
            yield

    def back(slot, sub, a):
        lat_s, g_s, kpe_s, ab_s, z_s = sets[slot]
        rows = slice(a * tm, (a + 1) * tm)
        src = slice(sub * tm, (sub + 1) * tm)
        z_out[rows, :] = z_s[src, :]
        ang_t = invft_ref[...] * posr_ref[a].astype(F32)
        cos_t = jnp.cos(ang_t)
        sin_t = jnp.sin(ang_t)
        lane = lax.broadcasted_iota(jnp.int32, (tm, LANES), 1)
        table = jnp.concatenate([cos_t, sin_t, jnp.zeros((LANES - ROPE, tm), F32)], axis=0).T
        cosf = jnp.where(lane < half, table, jnp.where(lane < ROPE, pltpu.roll(table, half, 1), 0.0))
        sinf = jnp.where(lane < half, -pltpu.roll(table, LANES - half, 1),
                         jnp.where(lane < ROPE, table, 0.0))
        yield

        qn = _rms(lat_s[src, :LORA], qlnw_ref[...], LORA)
        kvn = _rms(lat_s[src, LORA:], kvlnw_ref[...], LORA)
        qn_t = qn.T.astype(BF16)
        kvn_t = kvn.T.astype(BF16)
        kvn = kvn.astype(BF16)
        qt_nope = jnp.dot(wuqnt_ref[...], qn_t, preferred_element_type=F32)
        qt_pe = jnp.dot(wuqpt_ref[...], qn_t, preferred_element_type=F32)
        vt_out[a] = jnp.dot(wukvt_ref[...], kvn_t, preferred_element_type=F32).astype(BF16)
        k_nope = jnp.dot(kvn, wukk_ref[...], preferred_element_type=F32)
        k_pe = _rope(_rms(kpe_s[src, :], knwp_ref[...], ROPE), cosf, sinf, lane).astype(BF16)
        yield

        scale = QK_HEAD ** -0.5 * LOG2E
        qwn = qnwn_ref[...] * scale
        qwp = qnwp_ref[...] * scale
        for h in range(N_HEADS):
            lo = h * HEAD
            base = h * 2 * HEAD
            xn = qt_nope[lo:lo + NOPE, :]
            xn = xn * lax.rsqrt(jnp.sum(xn * xn, axis=0, keepdims=True) * (1.0 / NOPE) + EPS) * qwn
            xp = qt_pe[h * ROPE:(h + 1) * ROPE, :]
            xp = xp * lax.rsqrt(jnp.sum(xp * xp, axis=0, keepdims=True) * (1.0 / ROPE) + EPS) * qwp
            t1, t2 = xp[:half], xp[half:]
            qt_out[a, base:base + NOPE, :] = xn.astype(BF16)
            qt_out[a, base + NOPE:base + NOPE + half, :] = (t1 * cos_t - t2 * sin_t).astype(BF16)
            qt_out[a, base + NOPE + half:base + QK_HEAD, :] = (t2 * cos_t + t1 * sin_t).astype(BF16)
            qt_out[a, base + QK_HEAD:base + 2 * HEAD, :] = jnp.zeros((2 * HEAD - QK_HEAD, tm), BF16)
            k_out[rows, base:base + HEAD] = _rms(k_nope[:, lo:lo + HEAD], knwn_ref[...], NOPE).astype(BF16)
            k_out[rows, base + HEAD:base + 2 * HEAD] = k_pe
        yield

        seq_start = (4 * i + a) % tiles_per_seq == 0
        row8 = lax.broadcasted_iota(jnp.int32, (8, HEAD), 0)
        outs = ((gq_out, HEAD ** -0.5), (gk_out, 1.0), (gv_out, None))
        for part, (dst, norm_scale) in enumerate(outs):
            for h in range(N_HEADS):
                cols = slice(part * MIX + h * HEAD, part * MIX + (h + 1) * HEAD)
                g_blk = g_s[src, cols]
                prev = jnp.where(seq_start, 0.0, tail[:, cols])
                conv = convw_ref[CONV_W - 1:CONV_W, cols] * g_blk
                for s in range(1, CONV_W):
                    rolled = pltpu.roll(g_blk, s, 0)
                    first = jnp.where(row8 < s, pltpu.roll(prev, s, 0), rolled[:8])
                    shifted = jnp.concatenate([first, rolled[8:]], axis=0)
                    conv = conv + convw_ref[CONV_W - 1 - s:CONV_W - s, cols] * shifted
                tail[:, cols] = g_blk[tm - 8:tm, :]
                half_conv = 0.5 * conv
                act = half_conv + half_conv * jnp.tanh(half_conv)
                if norm_scale is not None:
                    act = act * (lax.rsqrt(jnp.sum(act * act, axis=-1, keepdims=True) + EPS) * norm_scale)
                dst[rows, h * HEAD:(h + 1) * HEAD] = act.astype(BF16)
            yield

        ab = ab_s[src, :]
        sp_in = ab + dtb_ref[...]
        softplus = jnp.maximum(sp_in, 0.0) + jnp.log(1.0 + jnp.exp(-jnp.abs(sp_in)))
        g = -jnp.exp(alog_ref[...]) * softplus
        beta = _sigmoid(ab)
        tri = tri_s[...]
        g_hi, g_mid, g_lo = _split3(g)
        gcum = (jnp.dot(tri, g_hi, preferred_element_type=F32)
                + jnp.dot(tri, g_mid, preferred_element_type=F32)
                + jnp.dot(tri, g_lo, preferred_element_type=F32))
        gb = jnp.where(lane < N_HEADS, gcum, jnp.where(lane < 2 * N_HEADS, beta, 0.0))
        gb_out[rows, :] = gb
        er = lax.broadcasted_iota(jnp.int32, (8, LANES), 0)
        ec = lax.broadcasted_iota(jnp.int32, (8, LANES), 1)
        eye8 = jnp.where(er == ec, 1.0, 0.0).astype(BF16)
        for c in range(tm // CHUNK):
            parts = _split3(gb[c * CHUNK:(c + 1) * CHUNK, :])
            acc = None
            for p in parts:
                t = lax.dot_general(eye8, p, (((1,), (1,)), ((), ())), preferred_element_type=F32)
                acc = t if acc is None else acc + t
            rows_out[a * (tm // CHUNK) + c] = acc

    @pl.when(i == 0)
    def _():
        tail[...] = jnp.zeros(tail.shape, F32)
        rr = lax.broadcasted_iota(jnp.int32, (tm, tm), 0)
        cc = lax.broadcasted_iota(jnp.int32, (tm, tm), 1)
        tri_s[...] = jnp.where((cc <= rr) & ((cc // CHUNK) == (rr // CHUNK)), 1.0, 0.0).astype(BF16)
        _interleave(front(x0_ref, 0))

    def backs(slot, first_tile):
        yield from back(slot, 0, first_tile)
        yield from back(slot, 1, first_tile + 1)

    _interleave(front(xa_ref, 1), backs(0, 0))
    _interleave(front(xb_ref, 0), backs(1, 2))


def _attend(pairs, qt_ref, k_ref, vt_ref, m_s, l_s, acc_s, st_s, *, tq, nq):
    heads = range(N_HEADS)
    steps = nq + 1
    kk = lax.broadcasted_iota(jnp.int32, (tq, tq), 0)
    qq = lax.broadcasted_iota(jnp.int32, (tq, tq), 1)
    ones = jnp.ones((16, tq), BF16)
    hk = lambda h: slice(2 * HEAD * h, 2 * HEAD * (h + 1))
    hv = lambda h: slice(HEAD * h, HEAD * (h + 1))

    def blocks(s):
        return [(jnp.where(s <= p, p, nq - 1 - p), jnp.where(s <= p, p - s, s - p - 1)) for p in pairs]

    def scores(s, c, h, q, kv):
        r0 = pl.multiple_of(kv * tq, tq)
        st_s[s % 2, c * N_HEADS + h] = jnp.dot(k_ref[pl.ds(r0, tq), hk(h)], qt_ref[q, hk(h), :],
                                               preferred_element_type=F32)

    for c, (q, kv) in enumerate(blocks(0)):
        for h in heads:
            scores(0, c, h, q, kv)
    yield
    for s in range(steps):
        nxt = blocks(s + 1) if s + 1 < steps else None
        for c, (q, kv) in enumerate(blocks(s)):
            for h in heads:
                if nxt is not None:
                    scores(s + 1, c, h, *nxt[c])
                i = q * N_HEADS + h
                st = st_s[s % 2, c * N_HEADS + h]
                if s in (0, steps - 1):
                    st = jnp.where(kk <= qq, st, -jnp.inf)
                m_old = m_s[i]
                m_new = jnp.maximum(m_old, jnp.max(st, axis=0, keepdims=True))
                alpha = jnp.exp2(m_old - m_new)
                p_t = jnp.exp2(st - m_new).astype(BF16)
                pv = jnp.dot(jnp.concatenate([vt_ref[kv, hv(h), :], ones], axis=0), p_t,
                             preferred_element_type=F32)
                m_s[i] = m_new
                l_s[i] = alpha * l_s[i] + pv[HEAD:HEAD + 1]
                acc_s[i] = alpha * acc_s[i] + pv[:HEAD]
            yield


def _attn_kernel(qt_ref, k_ref, vt_ref, w_ref, o_ref, m_s, l_s, acc_s, st_s, *, tq, nq):
    m_s[...] = jnp.full(m_s.shape, -jnp.inf, F32)
    l_s[...] = jnp.zeros(l_s.shape, F32)
    acc_s[...] = jnp.zeros(acc_s.shape, F32)
    n_pairs = nq // 2

    def body(t, carry):
        _interleave(_attend([t, n_pairs - 1 - t], qt_ref, k_ref, vt_ref, m_s, l_s, acc_s, st_s,
                            tq=tq, nq=nq))
        return carry

    lax.fori_loop(0, n_pairs // 2, body, 0)
    for q in range(nq):
        for h in range(N_HEADS):
            i = q * N_HEADS + h
            o_t = acc_s[i] / l_s[i]
            o_t = o_t * lax.rsqrt(jnp.sum(o_t * o_t, axis=0, keepdims=True) * (1.0 / HEAD) + EPS) * w_ref[h]
            o_ref[q, HEAD * h:HEAD * (h + 1), :] = o_t.astype(BF16)


def _gdn_kernel(q_ref, k_ref, v_ref, z_ref, gb_ref, rows_ref, gnw_ref, o_ref,
                s_ref, u_s, wq_s, kd_s, at_s, *, n_chunks, unroll, nb):
    ii = lax.broadcasted_iota(jnp.int32, (CHUNK, CHUNK), 0)
    jj = lax.broadcasted_iota(jnp.int32, (CHUNK, CHUNK), 1)
    eye = jnp.where(ii == jj, 1.0, 0.0)
    heads = range(N_HEADS)
    ring = 2 * unroll
    slot = lambda n, h: (n & (ring - 1)) * N_HEADS + h

    def solve(t, bb):
        ns = [t * unroll + c for c in range(unroll)]
        r0 = [pl.multiple_of(n * CHUNK, CHUNK) for n in ns]
        scal = [gb_ref[bb, pl.ds(r, CHUNK), :] for r in r0]
        rows = [rows_ref[bb, n] for n in ns]
        items = [(c, h) for c in range(unroll) for h in heads]
        cols = lambda h: slice(h * HEAD, (h + 1) * HEAD)
        q = [q_ref[bb, pl.ds(r0[c], CHUNK), cols(h)] for c, h in items]
        k = [k_ref[bb, pl.ds(r0[c], CHUNK), cols(h)] for c, h in items]
        v = [v_ref[bb, pl.ds(r0[c], CHUNK), cols(h)] for c, h in items]
        g_col = [jnp.broadcast_to(scal[c][:, h:h + 1], (CHUNK, LANES)) for c, h in items]
        b_col = [jnp.broadcast_to(scal[c][:, N_HEADS + h:N_HEADS + h + 1], (CHUNK, LANES))
                 for c, h in items]
        g_row = [rows[c][h:h + 1, :] for c, h in items]
        n_it = range(len(items))
        decay = [jnp.exp(jnp.where(ii >= jj, g_col[x][:, :CHUNK] - g_row[x], -jnp.inf)) for x in n_it]
        qkk = [_mm_nt(jnp.concatenate([q[x], k[x]], axis=0), k[x]) for x in n_it]
        yield
        lmat = [jnp.where(ii > jj, b_col[x][:, :CHUNK] * qkk[x][CHUNK:] * decay[x], 0.0) for x in n_it]
        inv = [eye - lmat[x] for x in n_it]
        pw = [_mm(lmat[x], lmat[x]) for x in n_it]
        yield
        for step in range(4):
            both = [_mm(jnp.concatenate([inv[x], pw[x]], axis=0), pw[x]) for x in n_it]
            inv = [inv[x] + both[x][:CHUNK] for x in n_it]
            pw = [both[x][CHUNK:] for x in n_it]
            yield
        inv = [inv[x] + _mm(inv[x], pw[x]) for x in n_it]
        yield
        e_col = [jnp.exp(g_col[x]) for x in n_it]
        rhs = [jnp.concatenate([v[x].astype(F32) * b_col[x], k[x].astype(F32) * (b_col[x] * e_col[x])],
                               axis=1) for x in n_it]
        uw = [_mm(inv[x], rhs[x]) for x in n_it]
        yield
        for x, (c, h) in enumerate(items):
            idx = slot(ns[c], h)
            g_last = g_col[x][CHUNK - 1:CHUNK, :]
            u_s[bb, idx] = uw[x][:, :HEAD]
            wq_s[bb, idx, 0:CHUNK, :] = uw[x][:, HEAD:].astype(BF16)
            wq_s[bb, idx, CHUNK:2 * CHUNK, :] = (q[x].astype(F32) * e_col[x]).astype(BF16)
            kd_s[bb, idx] = (k[x].astype(F32) * jnp.exp(g_last - g_col[x])).astype(BF16)
            at_s[bb, idx] = (qkk[x][:CHUNK] * decay[x]).astype(BF16)

    gnw = gnw_ref[...]

    def scan(t, bb):
        for c in range(unroll):
            n = t * unroll + c
            r0 = pl.multiple_of(n * CHUNK, CHUNK)
            g_last = gb_ref[bb, pl.ds(r0 + CHUNK - 1, 1), :]
            state = [s_ref[bb, h] for h in heads]
            sb = [x.astype(BF16) for x in state]
            ws = [_mm(wq_s[bb, slot(n, h)], sb[h]) for h in heads]
            yield
            v_new = [(u_s[bb, slot(n, h)] - ws[h][:CHUNK]).astype(BF16) for h in heads]
            o = [ws[h][CHUNK:] + _mm(at_s[bb, slot(n, h)], v_new[h]) for h in heads]
            upd = [_mm_tn(kd_s[bb, slot(n, h)], v_new[h]) for h in heads]
            yield
            for h in heads:
                c_dec = jnp.exp(jnp.broadcast_to(g_last[:, h:h + 1], (HEAD, HEAD)))
                s_ref[bb, h] = state[h] * c_dec + upd[h]
                zh = z_ref[bb, pl.ds(r0, CHUNK), h * HEAD:(h + 1) * HEAD].astype(F32)
                gated = _rms(o[h], gnw, HEAD) * (zh * _sigmoid(zh))
                o_ref[bb, pl.ds(r0, CHUNK), h * HEAD:(h + 1) * HEAD] = gated.astype(BF16)

    n_groups = n_chunks // unroll
    batches = range(nb)

    @pl.when(pl.program_id(1) == 0)
    def _():
        s_ref[...] = jnp.zeros_like(s_ref)

    _interleave(*[solve(0, bb) for bb in batches])

    def body(t, carry):
        _interleave(*[solve(t, bb) for bb in batches], *[scan(t - 1, bb) for bb in batches])
        return carry

    lax.fori_loop(1, n_groups, body, 0)
    _interleave(*[scan(n_groups - 1, bb) for bb in batches])


def _mlp_kernel(x_ref, mlat_ref, gdn_ref, woa_ref, wob_ref, nw_ref, wup_ref, wdn_ref, o_ref):
    mla_t = jnp.concatenate([mlat_ref[t] for t in range(mlat_ref.shape[0])], axis=1)
    h = (x_ref[...]
         + lax.dot_general(mla_t, woa_ref[...], (((0,), (0,)), ((), ())), preferred_element_type=F32)
         + jnp.dot(gdn_ref[...], wob_ref[...], preferred_element_type=F32))
    hn = _rms(h, nw_ref[...], D_MODEL).astype(BF16)
    o_ref[...] = h
    for c in range(D_FF // MLP_FC):
        u = jnp.dot(hn, wup_ref[:, c * MLP_FC:(c + 1) * MLP_FC], preferred_element_type=F32)
        a = jnp.square(jnp.maximum(u, 0.0)).astype(BF16)
        o_ref[...] += jnp.dot(a, wdn_ref[c * MLP_FC:(c + 1) * MLP_FC, :], preferred_element_type=F32)


def _const_spec(shape):
    nd = len(shape)
    return pl.BlockSpec(shape, lambda *_: (0,) * nd, pipeline_mode=pl.Buffered(1))


def _pad_lanes(v, width=LANES):
    v = v.reshape(1, -1).astype(F32)
    return jnp.pad(v, ((0, 0), (0, width - v.shape[1])))


def _layer(h, pos, inv_freq, p):
    B, S, _ = h.shape
    T = B * S
    x2 = h.reshape(T, D_MODEL)

    w_in = p["w_in"]
    o_q, o_kv, o_pe = 0, LORA, 2 * LORA
    o_g = o_pe + ROPE
    o_z = o_g + 3 * MIX
    o_a = o_z + MIX
    w_lat = w_in[:, o_q:o_pe].astype(BF16)
    w_kpe = jnp.pad(w_in[:, o_pe:o_g], ((0, 0), (0, LANES - ROPE))).astype(BF16)
    w_g = w_in[:, o_g:o_z].astype(BF16)
    w_z = w_in[:, o_z:o_a].astype(BF16)
    w_ab = jnp.pad(w_in[:, o_a:], ((0, 0), (0, LANES - 2 * N_HEADS))).astype(BF16)
    w_uq = p["w_uq"].reshape(LORA, N_HEADS, QK_HEAD)
    w_uq_nt = w_uq[:, :, :NOPE].reshape(LORA, N_HEADS * NOPE).T.astype(BF16)
    w_uq_pt = w_uq[:, :, NOPE:].reshape(LORA, N_HEADS * ROPE).T.astype(BF16)
    w_ukv = p["w_ukv"].reshape(LORA, N_HEADS, NOPE + HEAD)
    w_uk_k = w_ukv[:, :, :NOPE].reshape(LORA, N_HEADS * NOPE).astype(BF16)
    w_uk_vt = w_ukv[:, :, NOPE:].reshape(LORA, N_HEADS * HEAD).T.astype(BF16)

    tm = PREP_TM
    n_tiles = T // tm
    assert n_tiles % 4 == 0
    n_pairs = n_tiles // 2
    row_spec = lambda w: pl.BlockSpec((4 * tm, w), lambda i: (i, 0))
    col_bcast = lambda v: jnp.broadcast_to(v.astype(F32)[:, None], (v.shape[0], tm))
    x_pair = lambda index: pl.BlockSpec((2 * tm, D_MODEL), index)
    prep_in = [
        (x2, x_pair(lambda i: (0, 0))),
        (x2, x_pair(lambda i: (2 * i + 1, 0))),
        (x2, x_pair(lambda i: (jnp.minimum(2 * i + 2, n_pairs - 1), 0))),
        (pos.reshape(n_tiles, 1, tm), pl.BlockSpec((4, 1, tm), lambda i: (i, 0, 0))),
        (p["attn_norm_w"].reshape(1, D_MODEL), None),
        (w_lat, None), (w_kpe, None), (w_ab, None), (w_g, None), (w_z, None),
        (p["q_lat_norm_w"].reshape(1, LORA), None),
        (p["kv_lat_norm_w"].reshape(1, LORA), None),
        (w_uq_nt, None), (w_uq_pt, None), (w_uk_k, None), (w_uk_vt, None),
        (col_bcast(p["q_norm_w"][:NOPE]), None),
        (col_bcast(p["q_norm_w"][NOPE:]), None),
        (p["k_norm_w"][:NOPE].reshape(1, NOPE), None),
        (_pad_lanes(p["k_norm_w"][NOPE:]), None),
        (col_bcast(inv_freq), None),
        (p["conv_w"], None),
        (_pad_lanes(p["a_log"]), None),
        (_pad_lanes(p["dt_bias"]), None),
    ]
    prep_args = [a for a, _ in prep_in]
    prep_specs = [s if s is not None else _const_spec(a.shape) for a, s in prep_in]
    out_shapes = [
        jax.ShapeDtypeStruct((n_tiles, 2 * MIX, tm), BF16),
        jax.ShapeDtypeStruct((T, 2 * MIX), BF16),
        jax.ShapeDtypeStruct((n_tiles, MIX, tm), BF16),
        jax.ShapeDtypeStruct((T, MIX), BF16),
        jax.ShapeDtypeStruct((T, MIX), BF16),
        jax.ShapeDtypeStruct((T, MIX), BF16),
        jax.ShapeDtypeStruct((T, MIX), BF16),
        jax.ShapeDtypeStruct((T, LANES), F32),
        jax.ShapeDtypeStruct((T // CHUNK, 8, CHUNK), F32),
    ]
    tile_spec = lambda r: pl.BlockSpec((4, r, tm), lambda i: (i, 0, 0))
    out_specs = [tile_spec(2 * MIX), row_spec(2 * MIX), tile_spec(MIX)] + [row_spec(MIX)] * 4 + [
        row_spec(LANES),
        pl.BlockSpec((4 * tm // CHUNK, 8, CHUNK), lambda i: (i, 0, 0)),
    ]
    qt, k, vt, gq, gk, gv, gz, gb, rows = pl.pallas_call(
        functools.partial(_prep_kernel, tm=tm, tiles_per_seq=S // tm),
        grid=(n_tiles // 4,),
        in_specs=prep_specs,
        out_specs=out_specs,
        out_shape=out_shapes,
        scratch_shapes=2 * [
            pltpu.VMEM((2 * tm, 2 * LORA), F32),
            pltpu.VMEM((2 * tm, 3 * MIX), F32),
            pltpu.VMEM((2 * tm, LANES), F32),
            pltpu.VMEM((2 * tm, LANES), F32),
            pltpu.VMEM((2 * tm, MIX), BF16),
        ] + [
            pltpu.VMEM((8, 3 * MIX), F32),
            pltpu.VMEM((tm, tm), BF16),
        ],
        compiler_params=pltpu.CompilerParams(dimension_semantics=("arbitrary",),
                                             vmem_limit_bytes=VMEM_LIMIT),
        name="prep",
    )(*prep_args)

    tq = tm
    nq = S // tq
    assert nq % 4 == 0
    mla_o = pl.pallas_call(
        functools.partial(_attn_kernel, tq=tq, nq=nq),
        grid=(B,),
        in_specs=[
            pl.BlockSpec((nq, 2 * MIX, tq), lambda b: (b, 0, 0)),
            pl.BlockSpec((None, S, 2 * MIX), lambda b: (b, 0, 0)),
            pl.BlockSpec((None, nq, MIX, tq), lambda b: (b, 0, 0, 0)),
            _const_spec((N_HEADS, HEAD, tq)),
        ],
        out_specs=pl.BlockSpec((nq, MIX, tq), lambda b: (b, 0, 0)),
        out_shape=jax.ShapeDtypeStruct((B * nq, MIX, tq), BF16),
        scratch_shapes=[pltpu.VMEM((nq * N_HEADS, 1, tq), F32),
                        pltpu.VMEM((nq * N_HEADS, 1, tq), F32),
                        pltpu.VMEM((nq * N_HEADS, HEAD, tq), F32),
                        pltpu.VMEM((2, 2 * N_HEADS, tq, tq), F32)],
        compiler_params=pltpu.CompilerParams(
            dimension_semantics=("arbitrary",),
            vmem_limit_bytes=VMEM_LIMIT),
        name="attn",
    )(qt, k.reshape(B, S, 2 * MIX), vt.reshape(B, nq, MIX, tq),
      jnp.broadcast_to(p["mla_out_norm_w"].astype(F32)[:, :, None], (N_HEADS, HEAD, tq)))

    n_chunks = S // CHUNK
    nb = GDN_BATCHES
    n_seg = GDN_SEGMENTS
    seg_chunks = n_chunks // n_seg
    assert B % nb == 0 and n_chunks % n_seg == 0 and seg_chunks % GDN_UNROLL == 0
    seq_spec = lambda w: pl.BlockSpec((nb, S // n_seg, w), lambda b, j: (b, j, 0))
    ring_slots = 2 * GDN_UNROLL * N_HEADS
    gdn_o = pl.pallas_call(
        functools.partial(_gdn_kernel, n_chunks=seg_chunks, unroll=GDN_UNROLL, nb=nb),
        grid=(B // nb, n_seg),
        in_specs=[seq_spec(MIX)] * 4 + [
            seq_spec(LANES),
            pl.BlockSpec((nb, seg_chunks, 8, CHUNK), lambda b, j: (b, j, 0, 0)),
            _const_spec((1, HEAD)),
        ],
        out_specs=seq_spec(MIX),
        out_shape=jax.ShapeDtypeStruct((B, S, MIX), BF16),
        scratch_shapes=[
            pltpu.VMEM((nb, N_HEADS, HEAD, HEAD), F32),
            pltpu.VMEM((nb, ring_slots, CHUNK, HEAD), F32),
            pltpu.VMEM((nb, ring_slots, 2 * CHUNK, HEAD), BF16),
            pltpu.VMEM((nb, ring_slots, CHUNK, HEAD), BF16),
            pltpu.VMEM((nb, ring_slots, CHUNK, CHUNK), BF16),
        ],
        compiler_params=pltpu.CompilerParams(dimension_semantics=("arbitrary", "arbitrary"),
                                             vmem_limit_bytes=VMEM_LIMIT),
        name="gdn",
    )(gq.reshape(B, S, MIX), gk.reshape(B, S, MIX), gv.reshape(B, S, MIX), gz.reshape(B, S, MIX),
      gb.reshape(B, S, LANES), rows.reshape(B, n_chunks, 8, CHUNK),
      p["gdn_norm_w"].reshape(1, HEAD))

    tm2 = MLP_TM
    w_out = p["w_out"].astype(BF16)
    tok_spec = lambda w: pl.BlockSpec((tm2, w), lambda i: (i, 0))
    out = pl.pallas_call(
        _mlp_kernel,
        grid=(T // tm2,),
        in_specs=[
            tok_spec(D_MODEL),
            pl.BlockSpec((tm2 // tq, MIX, tq), lambda i: (i, 0, 0)),
            tok_spec(MIX),
            _const_spec((MIX, D_MODEL)), _const_spec((MIX, D_MODEL)),
            _const_spec((1, D_MODEL)),
            _const_spec((D_MODEL, D_FF)), _const_spec((D_FF, D_MODEL)),
        ],
        out_specs=tok_spec(D_MODEL),
        out_shape=jax.ShapeDtypeStruct((T, D_MODEL), F32),
        compiler_params=pltpu.CompilerParams(dimension_semantics=("arbitrary",),
                                             vmem_limit_bytes=VMEM_LIMIT),
        name="mlp",
    )(x2, mla_o, gdn_o.reshape(T, MIX), w_out[:MIX], w_out[MIX:],
      p["mlp_norm_w"].reshape(1, D_MODEL), p["w_up"].astype(BF16), p["w_down"].astype(BF16))
    return out.reshape(B, S, D_MODEL)


def kernel(x, positions, attn_norm_w, w_in, q_lat_norm_w, w_uq, kv_lat_norm_w, w_ukv, q_norm_w,
           k_norm_w, mla_out_norm_w, conv_w, a_log, dt_bias, gdn_norm_w, w_out, mlp_norm_w, w_up,
           w_down):
    B, S, _ = x.shape
    half = ROPE // 2
    inv_freq = ROPE_THETA ** (-jnp.arange(half, dtype=F32) / half)
    params = dict(attn_norm_w=attn_norm_w, w_in=w_in, q_lat_norm_w=q_lat_norm_w, w_uq=w_uq,
                  kv_lat_norm_w=kv_lat_norm_w, w_ukv=w_ukv, q_norm_w=q_norm_w, k_norm_w=k_norm_w,
                  mla_out_norm_w=mla_out_norm_w, conv_w=conv_w, a_log=a_log, dt_bias=dt_bias,
                  gdn_norm_w=gdn_norm_w, w_out=w_out, mlp_norm_w=mlp_norm_w, w_up=w_up,
                  w_down=w_down)
    h = x
    for l in range(attn_norm_w.shape[0]):
        h = _layer(h, positions, inv_freq, {name: val[l] for name, val in params.items()})
    return h
```

```python
import functools

import jax
import jax.numpy as jnp
from jax import lax
from jax.experimental import pallas as pl
from jax.experimental.pallas import tpu as pltpu

F32 = jnp.float32
BF16 = jnp.bfloat16

D_MODEL = 1024
N_HEADS = 4
LORA = 256
NOPE = 128
ROPE = 64
HEAD = 128
QK_HEAD = NOPE + ROPE
ROPE_THETA = 10000.0
CONV_W = 4
CHUNK = 64
D_FF = 4 * D_MODEL
EPS = 1e-6
LOG2E = 1.4426950408889634
LANES = 128
MIX = N_HEADS * HEAD

PREP_TM = 256
GDN_UNROLL = 4
GDN_BATCHES = 2
GDN_SEGMENTS = 2
MLP_TM = 1024
MLP_FC = 1024
VMEM_LIMIT = 56 * 1024 * 1024


def _mm(a, b):
    return jnp.dot(a.astype(BF16), b.astype(BF16), preferred_element_type=F32)


def _mm_nt(a, b):
    return lax.dot_general(a.astype(BF16), b.astype(BF16), (((1,), (1,)), ((), ())),
                           preferred_element_type=F32)


def _mm_tn(a, b):
    return lax.dot_general(a.astype(BF16), b.astype(BF16), (((0,), (0,)), ((), ())),
                           preferred_element_type=F32)


def _interleave(*gens):
    live = list(gens)
    while live:
        for g in list(live):
            try:
                next(g)
            except StopIteration:
                live.remove(g)


def _rms(x, w, n):
    return x * lax.rsqrt(jnp.sum(x * x, axis=-1, keepdims=True) * (1.0 / n) + EPS) * w


def _sigmoid(x):
    return 1.0 / (1.0 + jnp.exp(-x))


def _rope(x, cosf, sinf, lane):
    rot = jnp.where(lane < ROPE // 2, pltpu.roll(x, LANES - ROPE // 2, 1),
                    pltpu.roll(x, ROPE // 2, 1))
    return x * cosf + rot * sinf


def _prep_kernel(x0_ref, xa_ref, xb_ref, posr_ref, anw_ref, wlat_ref, wsmall_ref, wg_ref, wz_ref,
                 qlnw_ref, kvlnw_ref, wuqnt_ref, wuqpt_ref, wukk_ref, wukvt_ref,
                 qnwn_ref, qnwp_ref, knwn_ref, knwp_ref, invft_ref, convw_ref,
                 alog_ref, dtb_ref,
                 qt_out, k_out, vt_out, gq_out, gk_out, gv_out, z_out, gb_out, rows_out,
                 lat_s0, g_s0, kpe_s0, ab_s0, z_s0, lat_s1, g_s1, kpe_s1, ab_s1, z_s1, tail,
                 *, tm, tiles_per_seq):
    i = pl.program_id(0)
    half = ROPE // 2

    sets = ((lat_s0, g_s0, kpe_s0, ab_s0, z_s0), (lat_s1, g_s1, kpe_s1, ab_s1, z_s1))

    def front(x_ref, slot):
        lat_s, g_s, kpe_s, ab_s, z_s = sets[slot]
        xn = _rms(x_ref[...], anw_ref[...], D_MODEL).astype(BF16)
        yield
        step = 2 * LANES
        for w_ref, dst in ((wlat_ref, lat_s), (wg_ref, g_s)):
            for c0 in range(0, w_ref.shape[1], step):
                dst[:, c0:c0 + step] = jnp.dot(xn, w_ref[:, c0:c0 + step], preferred_element_type=F32)
                yield
        small = jnp.dot(xn, wsmall_ref[...], preferred_element_type=F32)
        kpe_s[...] = small[:, :LANES]
        ab_s[...] = small[:, LANES:]
        yield
        for c0 in range(0, MIX, step):
            z_s[:, c0:c0 + step] = jnp.dot(xn, wz_ref[:, c0:c0 + step],
                                           preferred_element_type=F32).astype(BF16)
            yield

    def back(slot, sub, a):
        lat_s, g_s, kpe_s, ab_s, z_s = sets[slot]
        rows = slice(a * tm, (a + 1) * tm)
        src = slice(sub * tm, (sub + 1) * tm)
        z_out[rows, :] = z_s[src, :]
        ang_t = invft_ref[...] * posr_ref[a].astype(F32)
        cos_t = jnp.cos(ang_t)
        sin_t = jnp.sin(ang_t)
        lane = lax.broadcasted_iota(jnp.int32, (tm, LANES), 1)
        table = jnp.concatenate([cos_t, sin_t, jnp.zeros((LANES - ROPE, tm), F32)], axis=0).T
        cosf = jnp.where(lane < half, table, jnp.where(lane < ROPE, pltpu.roll(table, half, 1), 0.0))
        sinf = jnp.where(lane < half, -pltpu.roll(table, LANES - half, 1),
                         jnp.where(lane < ROPE, table, 0.0))
        yield

        qn = _rms(lat_s[src, :LORA], qlnw_ref[...], LORA)
        kvn = _rms(lat_s[src, LORA:], kvlnw_ref[...], LORA)
        qn_t = qn.T.astype(BF16)
        kvn_t = kvn.T.astype(BF16)
        kvn = kvn.astype(BF16)
        qt_nope = jnp.dot(wuqnt_ref[...], qn_t, preferred_element_type=F32)
        qt_pe = jnp.dot(wuqpt_ref[...], qn_t, preferred_element_type=F32)
        vt_out[a] = jnp.dot(wukvt_ref[...], kvn_t, preferred_element_type=F32).astype(BF16)
        k_nope = jnp.dot(kvn, wukk_ref[...], preferred_element_type=F32)
        k_pe = _rope(_rms(kpe_s[src, :], knwp_ref[...], ROPE), cosf, sinf, lane).astype(BF16)
        yield

        scale = QK_HEAD ** -0.5 * LOG2E
        qwn = qnwn_ref[...] * scale
        qwp = qnwp_ref[...] * scale
        for h in range(N_HEADS):
            lo = h * HEAD
            base = h * 2 * HEAD
            xn = qt_nope[lo:lo + NOPE, :]
            xn = xn * lax.rsqrt(jnp.sum(xn * xn, axis=0, keepdims=True) * (1.0 / NOPE) + EPS) * qwn
            xp = qt_pe[h * ROPE:(h + 1) * ROPE, :]
            xp = xp * lax.rsqrt(jnp.sum(xp * xp, axis=0, keepdims=True) * (1.0 / ROPE) + EPS) * qwp
            t1, t2 = xp[:half], xp[half:]
            qt_out[a, base:base + NOPE, :] = xn.astype(BF16)
            qt_out[a, base + NOPE:base + NOPE + half, :] = (t1 * cos_t - t2 * sin_t).astype(BF16)
            qt_out[a, base + NOPE + half:base + QK_HEAD, :] = (t2 * cos_t + t1 * sin_t).astype(BF16)
            qt_out[a, base + QK_HEAD:base + 2 * HEAD, :] = jnp.zeros((2 * HEAD - QK_HEAD, tm), BF16)
            k_out[rows, base:base + HEAD] = _rms(k_nope[:, lo:lo + HEAD], knwn_ref[...], NOPE).astype(BF16)
            k_out[rows, base + HEAD:base + 2 * HEAD] = k_pe
        yield

        seq_start = (4 * i + a) % tiles_per_seq == 0
        row8 = lax.broadcasted_iota(jnp.int32, (8, HEAD), 0)
        outs = ((gq_out, HEAD ** -0.5), (gk_out, 1.0), (gv_out, None))
        for part, (dst, norm_scale) in enumerate(outs):
            for h in range(N_HEADS):
                cols = slice(part * MIX + h * HEAD, part * MIX + (h + 1) * HEAD)
                g_blk = g_s[src, cols]
                prev = jnp.where(seq_start, 0.0, tail[:, cols])
                conv = convw_ref[CONV_W - 1:CONV_W, cols] * g_blk
                for s in range(1, CONV_W):
                    rolled = pltpu.roll(g_blk, s, 0)
                    first = jnp.where(row8 < s, pltpu.roll(prev, s, 0), rolled[:8])
                    shifted = jnp.concatenate([first, rolled[8:]], axis=0)
                    conv = conv + convw_ref[CONV_W - 1 - s:CONV_W - s, cols] * shifted
                tail[:, cols] = g_blk[tm - 8:tm, :]
                half_conv = 0.5 * conv
                act = half_conv + half_conv * jnp.tanh(half_conv)
                if norm_scale is not None:
                    act = act * (lax.rsqrt(jnp.sum(act * act, axis=-1, keepdims=True) + EPS) * norm_scale)
                dst[rows, h * HEAD:(h + 1) * HEAD] = act.astype(BF16)
            yield

        ab = ab_s[src, :]
        sp_in = ab + dtb_ref[...]
        softplus = jnp.maximum(sp_in, 0.0) + jnp.log(1.0 + jnp.exp(-jnp.abs(sp_in)))
        g = -jnp.exp(alog_ref[...]) * softplus
        beta = _sigmoid(ab)
        packed = jnp.where(lane < N_HEADS, g, jnp.where(lane < 2 * N_HEADS, beta, 0.0))
        top = packed.T[0:8, :]
        row = lax.broadcasted_iota(jnp.int32, (8, LANES), 0)
        pos = lax.broadcasted_iota(jnp.int32, (8, LANES), 1) % CHUNK
        parts = []
        for hh in range(tm // LANES):
            v = top[:, hh * LANES:(hh + 1) * LANES]
            cum = v
            shift = 1
            while shift < CHUNK:
                cum = cum + jnp.where(pos >= shift, pltpu.roll(cum, shift, 1), 0.0)
                shift *= 2
            v = jnp.where(row < N_HEADS, cum, v)
            parts.append(v)
            for c2 in range(LANES // CHUNK):
                rows_out[a * (tm // CHUNK) + hh * (LANES // CHUNK) + c2] = v[:, c2 * CHUNK:(c2 + 1) * CHUNK]
        scal_t = jnp.concatenate(parts, axis=1)
        gb_out[rows, :] = jnp.concatenate([scal_t, jnp.zeros((LANES - 8, tm), F32)], axis=0).T

    @pl.when(i == 0)
    def _():
        tail[...] = jnp.zeros(tail.shape, F32)
        _interleave(front(x0_ref, 0))

    def backs(slot, first_tile):
        yield from back(slot, 0, first_tile)
        yield from back(slot, 1, first_tile + 1)

    _interleave(front(xa_ref, 1), backs(0, 0))
    _interleave(front(xb_ref, 0), backs(1, 2))


def _attend(pairs, qt_ref, k_ref, vt_ref, m_s, l_s, acc_s, st_s, *, tq, nq):
    heads = range(N_HEADS)
    steps = nq + 1
    kk = lax.broadcasted_iota(jnp.int32, (tq, tq), 0)
    qq = lax.broadcasted_iota(jnp.int32, (tq, tq), 1)
    ones = jnp.ones((16, tq), BF16)
    hk = lambda h: slice(2 * HEAD * h, 2 * HEAD * (h + 1))
    hv = lambda h: slice(HEAD * h, HEAD * (h + 1))

    def blocks(s):
        return [(jnp.where(s <= p, p, nq - 1 - p), jnp.where(s <= p, p - s, s - p - 1)) for p in pairs]

    def scores(s, c, h, q, kv):
        r0 = pl.multiple_of(kv * tq, tq)
        st_s[s % 2, c * N_HEADS + h] = jnp.dot(k_ref[pl.ds(r0, tq), hk(h)], qt_ref[q, hk(h), :],
                                               preferred_element_type=F32)

    for c, (q, kv) in enumerate(blocks(0)):
        for h in heads:
            scores(0, c, h, q, kv)
    yield
    for s in range(steps):
        nxt = blocks(s + 1) if s + 1 < steps else None
        for c, (q, kv) in enumerate(blocks(s)):
            for h in heads:
                if nxt is not None:
                    scores(s + 1, c, h, *nxt[c])
                i = q * N_HEADS + h
                st = st_s[s % 2, c * N_HEADS + h]
                if s in (0, steps - 1):
                    st = jnp.where(kk <= qq, st, -jnp.inf)
                m_old = m_s[i]
                m_new = jnp.maximum(m_old, jnp.max(st, axis=0, keepdims=True))
                alpha = jnp.exp2(m_old - m_new)
                p_t = jnp.exp2(st - m_new).astype(BF16)
                pv = jnp.dot(jnp.concatenate([vt_ref[kv, hv(h), :], ones], axis=0), p_t,
                             preferred_element_type=F32)
                m_s[i] = m_new
                l_s[i] = alpha * l_s[i] + pv[HEAD:HEAD + 1]
                acc_s[i] = alpha * acc_s[i] + pv[:HEAD]
            yield


def _attn_kernel(qt_ref, k_ref, vt_ref, w_ref, o_ref, m_s, l_s, acc_s, st_s, *, tq, nq):
    m_s[...] = jnp.full(m_s.shape, -jnp.inf, F32)
    l_s[...] = jnp.zeros(l_s.shape, F32)
    acc_s[...] = jnp.zeros(acc_s.shape, F32)
    n_pairs = nq // 2

    def body(t, carry):
        _interleave(_attend([t, n_pairs - 1 - t], qt_ref, k_ref, vt_ref, m_s, l_s, acc_s, st_s,
                            tq=tq, nq=nq))
        return carry

    lax.fori_loop(0, n_pairs // 2, body, 0)
    for q in range(nq):
        for h in range(N_HEADS):
            i = q * N_HEADS + h
            o_t = acc_s[i] / l_s[i]
            o_t = o_t * lax.rsqrt(jnp.sum(o_t * o_t, axis=0, keepdims=True) * (1.0 / HEAD) + EPS) * w_ref[h]
            o_ref[q, HEAD * h:HEAD * (h + 1), :] = o_t.astype(BF16)


def _gdn_kernel(q_ref, k_ref, v_ref, z_ref, gb_ref, rows_ref, gnw_ref, o_ref,
                s_ref, u_s, wq_s, kd_s, at_s, *, n_chunks, unroll, nb):
    ii = lax.broadcasted_iota(jnp.int32, (CHUNK, CHUNK), 0)
    jj = lax.broadcasted_iota(jnp.int32, (CHUNK, CHUNK), 1)
    eye = jnp.where(ii == jj, 1.0, 0.0)
    heads = range(N_HEADS)
    ring = 2 * unroll
    slot = lambda n, h: (n & (ring - 1)) * N_HEADS + h
    eye_p = jnp.concatenate([eye] * N_HEADS, axis=1)
    lane_head = lax.broadcasted_iota(jnp.int32, (CHUNK, N_HEADS * CHUNK), 1) // CHUNK

    def solve(t, bb):
        ns = [t * unroll + c for c in range(unroll)]
        r0 = [pl.multiple_of(n * CHUNK, CHUNK) for n in ns]
        scal = [gb_ref[bb, pl.ds(r, CHUNK), :] for r in r0]
        rows = [rows_ref[bb, n] for n in ns]
        items = [(c, h) for c in range(unroll) for h in heads]
        cols = lambda h: slice(h * HEAD, (h + 1) * HEAD)
        q = [q_ref[bb, pl.ds(r0[c], CHUNK), cols(h)] for c, h in items]
        k = [k_ref[bb, pl.ds(r0[c], CHUNK), cols(h)] for c, h in items]
        v = [v_ref[bb, pl.ds(r0[c], CHUNK), cols(h)] for c, h in items]
        g_col = [jnp.broadcast_to(scal[c][:, h:h + 1], (CHUNK, LANES)) for c, h in items]
        b_col = [jnp.broadcast_to(scal[c][:, N_HEADS + h:N_HEADS + h + 1], (CHUNK, LANES))
                 for c, h in items]
        g_row = [rows[c][h:h + 1, :] for c, h in items]
        n_it = range(len(items))
        decay = [jnp.exp(jnp.where(ii >= jj, g_col[x][:, :CHUNK] - g_row[x], -jnp.inf)) for x in n_it]
        qkk = [_mm_nt(jnp.concatenate([q[x], k[x]], axis=0), k[x]) for x in n_it]
        yield
        lmat = [jnp.where(ii > jj, b_col[x][:, :CHUNK] * qkk[x][CHUNK:] * decay[x], 0.0) for x in n_it]
        n_ch = range(unroll)
        packed = lambda vals, c: jnp.concatenate(vals[c * N_HEADS:(c + 1) * N_HEADS], axis=1)

        def per_head(x_p, y_p):
            y_b = y_p.astype(BF16)
            diag = jnp.concatenate([jnp.where(lane_head == h, y_b, jnp.zeros_like(y_b)) for h in heads], axis=0)
            return jnp.dot(x_p.astype(BF16), diag, preferred_element_type=F32)

        l_p = [packed(lmat, c) for c in n_ch]
        inv_p = [eye_p - l_p[c] for c in n_ch]
        pw_p = [per_head(l_p[c], l_p[c]) for c in n_ch]
        yield
        for step in range(4):
            both = [per_head(jnp.concatenate([inv_p[c], pw_p[c]], axis=0), pw_p[c]) for c in n_ch]
            inv_p = [inv_p[c] + both[c][:CHUNK] for c in n_ch]
            pw_p = [both[c][CHUNK:] for c in n_ch]
            yield
        inv_p = [inv_p[c] + per_head(inv_p[c], pw_p[c]) for c in n_ch]
        yield
        inv = [inv_p[c][:, h * CHUNK:(h + 1) * CHUNK] for c, h in items]
        e_col = [jnp.exp(g_col[x]) for x in n_it]
        rhs = [jnp.concatenate([v[x].astype(F32) * b_col[x], k[x].astype(F32) * (b_col[x] * e_col[x])],
                               axis=1) for x in n_it]
        uw = [_mm(inv[x], rhs[x]) for x in n_it]
        yield
        for x, (c, h) in enumerate(items):
            idx = slot(ns[c], h)
            g_last = g_col[x][CHUNK - 1:CHUNK, :]
            u_s[bb, idx] = uw[x][:, :HEAD]
            wq_s[bb, idx, 0:CHUNK, :] = uw[x][:, HEAD:].astype(BF16)
            wq_s[bb, idx, CHUNK:2 * CHUNK, :] = (q[x].astype(F32) * e_col[x]).astype(BF16)
            kd_s[bb, idx] = (k[x].astype(F32) * jnp.exp(g_last - g_col[x])).astype(BF16)
            at_s[bb, idx] = (qkk[x][:CHUNK] * decay[x]).astype(BF16)

    gnw = gnw_ref[...]

    def scan(t, bb):
        for c in range(unroll):
            n = t * unroll + c
            r0 = pl.multiple_of(n * CHUNK, CHUNK)
            g_last = gb_ref[bb, pl.ds(r0 + CHUNK - 1, 1), :]
            state = [s_ref[bb, h] for h in heads]
            sb = [x.astype(BF16) for x in state]
            ws = [_mm(wq_s[bb, slot(n, h)], sb[h]) for h in heads]
            yield
            v_new = [(u_s[bb, slot(n, h)] - ws[h][:CHUNK]).astype(BF16) for h in heads]
            o = [ws[h][CHUNK:] + _mm(at_s[bb, slot(n, h)], v_new[h]) for h in heads]
            upd = [_mm_tn(kd_s[bb, slot(n, h)], v_new[h]) for h in heads]
            yield
            for h in heads:
                c_dec = jnp.exp(jnp.broadcast_to(g_last[:, h:h + 1], (HEAD, HEAD)))
                s_ref[bb, h] = state[h] * c_dec + upd[h]
                zh = z_ref[bb, pl.ds(r0, CHUNK), h * HEAD:(h + 1) * HEAD].astype(F32)
                gated = _rms(o[h], gnw, HEAD) * (zh * _sigmoid(zh))
                o_ref[bb, pl.ds(r0, CHUNK), h * HEAD:(h + 1) * HEAD] = gated.astype(BF16)

    n_groups = n_chunks // unroll
    batches = range(nb)

    @pl.when(pl.program_id(1) == 0)
    def _():
        s_ref[...] = jnp.zeros_like(s_ref)

    _interleave(*[solve(0, bb) for bb in batches])

    def body(t, carry):
        _interleave(*[solve(t, bb) for bb in batches], *[scan(t - 1, bb) for bb in batches])
        return carry

    lax.fori_loop(1, n_groups, body, 0)
    _interleave(*[scan(n_groups - 1, bb) for bb in batches])


def _mlp_kernel(x_ref, mlat_ref, gdn_ref, woa_ref, wob_ref, nw_ref, wup_ref, wdn_ref, o_ref):
    mla_t = jnp.concatenate([mlat_ref[t] for t in range(mlat_ref.shape[0])], axis=1)
    h = (x_ref[...]
         + lax.dot_general(mla_t, woa_ref[...], (((0,), (0,)), ((), ())), preferred_element_type=F32)
         + jnp.dot(gdn_ref[...], wob_ref[...], preferred_element_type=F32))
    hn = _rms(h, nw_ref[...], D_MODEL).astype(BF16)
    o_ref[...] = h
    for c in range(D_FF // MLP_FC):
        u = jnp.dot(hn, wup_ref[:, c * MLP_FC:(c + 1) * MLP_FC], preferred_element_type=F32)
        a = jnp.square(jnp.maximum(u, 0.0)).astype(BF16)
        o_ref[...] += jnp.dot(a, wdn_ref[c * MLP_FC:(c + 1) * MLP_FC, :], preferred_element_type=F32)


def _const_spec(shape):
    nd = len(shape)
    return pl.BlockSpec(shape, lambda *_: (0,) * nd, pipeline_mode=pl.Buffered(1))


def _pad_lanes(v, width=LANES):
    v = v.reshape(1, -1).astype(F32)
    return jnp.pad(v, ((0, 0), (0, width - v.shape[1])))


def _layer(h, pos, inv_freq, p):
    B, S, _ = h.shape
    T = B * S
    x2 = h.reshape(T, D_MODEL)

    w_in = p["w_in"]
    o_q, o_kv, o_pe = 0, LORA, 2 * LORA
    o_g = o_pe + ROPE
    o_z = o_g + 3 * MIX
    o_a = o_z + MIX
    w_lat = w_in[:, o_q:o_pe].astype(BF16)
    w_kpe = jnp.pad(w_in[:, o_pe:o_g], ((0, 0), (0, LANES - ROPE))).astype(BF16)
    w_g = w_in[:, o_g:o_z].astype(BF16)
    w_z = w_in[:, o_z:o_a].astype(BF16)
    w_ab = jnp.pad(w_in[:, o_a:], ((0, 0), (0, LANES - 2 * N_HEADS))).astype(BF16)
    w_uq = p["w_uq"].reshape(LORA, N_HEADS, QK_HEAD)
    w_uq_nt = w_uq[:, :, :NOPE].reshape(LORA, N_HEADS * NOPE).T.astype(BF16)
    w_uq_pt = w_uq[:, :, NOPE:].reshape(LORA, N_HEADS * ROPE).T.astype(BF16)
    w_ukv = p["w_ukv"].reshape(LORA, N_HEADS, NOPE + HEAD)
    w_uk_k = w_ukv[:, :, :NOPE].reshape(LORA, N_HEADS * NOPE).astype(BF16)
    w_uk_vt = w_ukv[:, :, NOPE:].reshape(LORA, N_HEADS * HEAD).T.astype(BF16)

    tm = PREP_TM
    n_tiles = T // tm
    assert n_tiles % 4 == 0
    n_pairs = n_tiles // 2
    row_spec = lambda w: pl.BlockSpec((4 * tm, w), lambda i: (i, 0))
    col_bcast = lambda v: jnp.broadcast_to(v.astype(F32)[:, None], (v.shape[0], tm))
    x_pair = lambda index: pl.BlockSpec((2 * tm, D_MODEL), index)
    prep_in = [
        (x2, x_pair(lambda i: (0, 0))),
        (x2, x_pair(lambda i: (2 * i + 1, 0))),
        (x2, x_pair(lambda i: (jnp.minimum(2 * i + 2, n_pairs - 1), 0))),
        (pos.reshape(n_tiles, 1, tm), pl.BlockSpec((4, 1, tm), lambda i: (i, 0, 0))),
        (p["attn_norm_w"].reshape(1, D_MODEL), None),
        (w_lat, None), (jnp.concatenate([w_kpe, w_ab], axis=1), None), (w_g, None), (w_z, None),
        (p["q_lat_norm_w"].reshape(1, LORA), None),
        (p["kv_lat_norm_w"].reshape(1, LORA), None),
        (w_uq_nt, None), (w_uq_pt, None), (w_uk_k, None), (w_uk_vt, None),
        (col_bcast(p["q_norm_w"][:NOPE]), None),
        (col_bcast(p["q_norm_w"][NOPE:]), None),
        (p["k_norm_w"][:NOPE].reshape(1, NOPE), None),
        (_pad_lanes(p["k_norm_w"][NOPE:]), None),
        (col_bcast(inv_freq), None),
        (p["conv_w"], None),
        (_pad_lanes(p["a_log"]), None),
        (_pad_lanes(p["dt_bias"]), None),
    ]
    prep_args = [a for a, _ in prep_in]
    prep_specs = [s if s is not None else _const_spec(a.shape) for a, s in prep_in]
    out_shapes = [
        jax.ShapeDtypeStruct((n_tiles, 2 * MIX, tm), BF16),
        jax.ShapeDtypeStruct((T, 2 * MIX), BF16),
        jax.ShapeDtypeStruct((n_tiles, MIX, tm), BF16),
        jax.ShapeDtypeStruct((T, MIX), BF16),
        jax.ShapeDtypeStruct((T, MIX), BF16),
        jax.ShapeDtypeStruct((T, MIX), BF16),
        jax.ShapeDtypeStruct((T, MIX), BF16),
        jax.ShapeDtypeStruct((T, LANES), F32),
        jax.ShapeDtypeStruct((T // CHUNK, 8, CHUNK), F32),
    ]
    tile_spec = lambda r: pl.BlockSpec((4, r, tm), lambda i: (i, 0, 0))
    out_specs = [tile_spec(2 * MIX), row_spec(2 * MIX), tile_spec(MIX)] + [row_spec(MIX)] * 4 + [
        row_spec(LANES),
        pl.BlockSpec((4 * tm // CHUNK, 8, CHUNK), lambda i: (i, 0, 0)),
    ]
    qt, k, vt, gq, gk, gv, gz, gb, rows = pl.pallas_call(
        functools.partial(_prep_kernel, tm=tm, tiles_per_seq=S // tm),
        grid=(n_tiles // 4,),
        in_specs=prep_specs,
        out_specs=out_specs,
        out_shape=out_shapes,
        scratch_shapes=2 * [
            pltpu.VMEM((2 * tm, 2 * LORA), F32),
            pltpu.VMEM((2 * tm, 3 * MIX), F32),
            pltpu.VMEM((2 * tm, LANES), F32),
            pltpu.VMEM((2 * tm, LANES), F32),
            pltpu.VMEM((2 * tm, MIX), BF16),
        ] + [
            pltpu.VMEM((8, 3 * MIX), F32),
        ],
        compiler_params=pltpu.CompilerParams(dimension_semantics=("arbitrary",),
                                             vmem_limit_bytes=VMEM_LIMIT),
        name="prep",
    )(*prep_args)

    tq = tm
    nq = S // tq
    assert nq % 4 == 0
    mla_o = pl.pallas_call(
        functools.partial(_attn_kernel, tq=tq, nq=nq),
        grid=(B,),
        in_specs=[
            pl.BlockSpec((nq, 2 * MIX, tq), lambda b: (b, 0, 0)),
            pl.BlockSpec((None, S, 2 * MIX), lambda b: (b, 0, 0)),
            pl.BlockSpec((None, nq, MIX, tq), lambda b: (b, 0, 0, 0)),
            _const_spec((N_HEADS, HEAD, tq)),
        ],
        out_specs=pl.BlockSpec((nq, MIX, tq), lambda b: (b, 0, 0)),
        out_shape=jax.ShapeDtypeStruct((B * nq, MIX, tq), BF16),
        scratch_shapes=[pltpu.VMEM((nq * N_HEADS, 1, tq), F32),
                        pltpu.VMEM((nq * N_HEADS, 1, tq), F32),
                        pltpu.VMEM((nq * N_HEADS, HEAD, tq), F32),
                        pltpu.VMEM((2, 2 * N_HEADS, tq, tq), F32)],
        compiler_params=pltpu.CompilerParams(
            dimension_semantics=("arbitrary",),
            vmem_limit_bytes=VMEM_LIMIT),
        name="attn",
    )(qt, k.reshape(B, S, 2 * MIX), vt.reshape(B, nq, MIX, tq),
      jnp.broadcast_to(p["mla_out_norm_w"].astype(F32)[:, :, None], (N_HEADS, HEAD, tq)))

    n_chunks = S // CHUNK
    nb = GDN_BATCHES
    n_seg = GDN_SEGMENTS
    seg_chunks = n_chunks // n_seg
    assert B % nb == 0 and n_chunks % n_seg == 0 and seg_chunks % GDN_UNROLL == 0
    seq_spec = lambda w: pl.BlockSpec((nb, S // n_seg, w), lambda b, j: (b, j, 0))
    ring_slots = 2 * GDN_UNROLL * N_HEADS
    gdn_o = pl.pallas_call(
        functools.partial(_gdn_kernel, n_chunks=seg_chunks, unroll=GDN_UNROLL, nb=nb),
        grid=(B // nb, n_seg),
        in_specs=[seq_spec(MIX)] * 4 + [
            seq_spec(LANES),
            pl.BlockSpec((nb, seg_chunks, 8, CHUNK), lambda b, j: (b, j, 0, 0)),
            _const_spec((1, HEAD)),
        ],
        out_specs=seq_spec(MIX),
        out_shape=jax.ShapeDtypeStruct((B, S, MIX), BF16),
        scratch_shapes=[
            pltpu.VMEM((nb, N_HEADS, HEAD, HEAD), F32),
            pltpu.VMEM((nb, ring_slots, CHUNK, HEAD), F32),
            pltpu.VMEM((nb, ring_slots, 2 * CHUNK, HEAD), BF16),
            pltpu.VMEM((nb, ring_slots, CHUNK, HEAD), BF16),
            pltpu.VMEM((nb, ring_slots, CHUNK, CHUNK), BF16),
        ],
        compiler_params=pltpu.CompilerParams(dimension_semantics=("arbitrary", "arbitrary"),
                                             vmem_limit_bytes=VMEM_LIMIT),
        name="gdn",
    )(gq.reshape(B, S, MIX), gk.reshape(B, S, MIX), gv.reshape(B, S, MIX), gz.reshape(B, S, MIX),
      gb.reshape(B, S, LANES), rows.reshape(B, n_chunks, 8, CHUNK),
      p["gdn_norm_w"].reshape(1, HEAD))

    tm2 = MLP_TM
    w_out = p["w_out"].astype(BF16)
    tok_spec = lambda w: pl.BlockSpec((tm2, w), lambda i: (i, 0))
    out = pl.pallas_call(
        _mlp_kernel,
        grid=(T // tm2,),
        in_specs=[
            tok_spec(D_MODEL),
            pl.BlockSpec((tm2 // tq, MIX, tq), lambda i: (i, 0, 0)),
            tok_spec(MIX),
            _const_spec((MIX, D_MODEL)), _const_spec((MIX, D_MODEL)),
            _const_spec((1, D_MODEL)),
            _const_spec((D_MODEL, D_FF)), _const_spec((D_FF, D_MODEL)),
        ],
        out_specs=tok_spec(D_MODEL),
        out_shape=jax.ShapeDtypeStruct((T, D_MODEL), F32),
        compiler_params=pltpu.CompilerParams(dimension_semantics=("arbitrary",),
                                             vmem_limit_bytes=VMEM_LIMIT),
        name="mlp",
    )(x2, mla_o, gdn_o.reshape(T, MIX), w_out[:MIX], w_out[MIX:],
      p["mlp_norm_w"].reshape(1, D_MODEL), p["w_up"].astype(BF16), p["w_down"].astype(BF16))
    return out.reshape(B, S, D_MODEL)


def kernel(x, positions, attn_norm_w, w_in, q_lat_norm_w, w_uq, kv_lat_norm_w, w_ukv, q_norm_w,
           k_norm_w, mla_out_norm_w, conv_w, a_log, dt_bias, gdn_norm_w, w_out, mlp_norm_w, w_up,
           w_down):
    B, S, _ = x.shape
    half = ROPE // 2
    inv_freq = ROPE_THETA ** (-jnp.arange(half, dtype=F32) / half)
    params = dict(attn_norm_w=attn_norm_w, w_in=w_in, q_lat_norm_w=q_lat_norm_w, w_uq=w_uq,
                  kv_lat_norm_w=kv_lat_norm_w, w_ukv=w_ukv, q_norm_w=q_norm_w, k_norm_w=k_norm_w,
                  mla_out_norm_w=mla_out_norm_w, conv_w=conv_w, a_log=a_log, dt_bias=dt_bias,
                  gdn_norm_w=gdn_norm_w, w_out=w_out, mlp_norm_w=mlp_norm_w, w_up=w_up,
                  w_down=w_down)
    h = x
    for l in range(attn_norm_w.shape[0]):
        h = _layer(h, positions, inv_freq, {name: val[l] for name, val in params.items()})
    return h
```

```python
import functools

import jax
import jax.numpy as jnp
from jax import lax
from jax.experimental import pallas as pl
from jax.experimental.pallas import tpu as pltpu

F32 = jnp.float32
BF16 = jnp.bfloat16

D_MODEL = 1024
N_HEADS = 4
LORA = 256
NOPE = 128
ROPE = 64
HEAD = 128
QK_HEAD = NOPE + ROPE
ROPE_THETA = 10000.0
CONV_W = 4
CHUNK = 64
D_FF = 4 * D_MODEL
EPS = 1e-6
LOG2E = 1.4426950408889634
LANES = 128
MIX = N_HEADS * HEAD

PREP_TM = 256
GDN_UNROLL = 4
GDN_BATCHES = 4
GDN_SEGMENTS = 4
MLP_TM = 1024
MLP_FC = 1024
VMEM_LIMIT = 56 * 1024 * 1024


def _mm(a, b):
    return jnp.dot(a.astype(BF16), b.astype(BF16), preferred_element_type=F32)


def _mm_nt(a, b):
    return lax.dot_general(a.astype(BF16), b.astype(BF16), (((1,), (1,)), ((), ())),
                           preferred_element_type=F32)


def _mm_tn(a, b):
    return lax.dot_general(a.astype(BF16), b.astype(BF16), (((0,), (0,)), ((), ())),
                           preferred_element_type=F32)


def _interleave(*gens):
    live = list(gens)
    while live:
        for g in list(live):
            try:
                next(g)
            except StopIteration:
                live.remove(g)


def _rms(x, w, n):
    return x * lax.rsqrt(jnp.sum(x * x, axis=-1, keepdims=True) * (1.0 / n) + EPS) * w


def _sigmoid(x):
    return 1.0 / (1.0 + jnp.exp(-x))


def _rope(x, cosf, sinf, lane):
    rot = jnp.where(lane < ROPE // 2, pltpu.roll(x, LANES - ROPE // 2, 1),
                    pltpu.roll(x, ROPE // 2, 1))
    return x * cosf + rot * sinf


def _prep_kernel(x0_ref, xa_ref, xb_ref, posr_ref, anw_ref, wlat_ref, wsmall_ref, wg_ref, wz_ref,
                 qlnw_ref, kvlnw_ref, wuqnt_ref, wuqpt_ref, wukk_ref, wukvt_ref,
                 qnwn_ref, qnwp_ref, knwn_ref, knwp_ref, invft_ref, convw_ref,
                 alog_ref, dtb_ref,
                 qt_out, k_out, vt_out, gq_out, gk_out, gv_out, z_out, gb_out, rows_out,
                 lat_s0, g_s0, kpe_s0, ab_s0, z_s0, lat_s1, g_s1, kpe_s1, ab_s1, z_s1, tail,
                 *, tm, tiles_per_seq):
    i = pl.program_id(0)
    half = ROPE // 2

    sets = ((lat_s0, g_s0, kpe_s0, ab_s0, z_s0), (lat_s1, g_s1, kpe_s1, ab_s1, z_s1))

    def front(x_ref, slot):
        lat_s, g_s, kpe_s, ab_s, z_s = sets[slot]
        xn = _rms(x_ref[...], anw_ref[...], D_MODEL).astype(BF16)
        yield
        step = 2 * LANES
        for w_ref, dst in ((wlat_ref, lat_s), (wg_ref, g_s)):
            for c0 in range(0, w_ref.shape[1], step):
                dst[:, c0:c0 + step] = jnp.dot(xn, w_ref[:, c0:c0 + step], preferred_element_type=F32)
                yield
        small = jnp.dot(xn, wsmall_ref[...], preferred_element_type=F32)
        kpe_s[...] = small[:, :LANES]
        ab_s[...] = small[:, LANES:]
        yield
        for c0 in range(0, MIX, step):
            z_s[:, c0:c0 + step] = jnp.dot(xn, wz_ref[:, c0:c0 + step],
                                           preferred_element_type=F32).astype(BF16)
            yield

    def back(slot, sub, a):
        lat_s, g_s, kpe_s, ab_s, z_s = sets[slot]
        rows = slice(a * tm, (a + 1) * tm)
        src = slice(sub * tm, (sub + 1) * tm)
        z_out[rows, :] = z_s[src, :]
        ang_t = invft_ref[...] * posr_ref[a].astype(F32)
        cos_t = jnp.cos(ang_t)
        sin_t = jnp.sin(ang_t)
        lane = lax.broadcasted_iota(jnp.int32, (tm, LANES), 1)
        table = jnp.concatenate([cos_t, sin_t, jnp.zeros((LANES - ROPE, tm), F32)], axis=0).T
        cosf = jnp.where(lane < half, table, jnp.where(lane < ROPE, pltpu.roll(table, half, 1), 0.0))
        sinf = jnp.where(lane < half, -pltpu.roll(table, LANES - half, 1),
                         jnp.where(lane < ROPE, table, 0.0))
        yield

        qn = _rms(lat_s[src, :LORA], qlnw_ref[...], LORA)
        kvn = _rms(lat_s[src, LORA:], kvlnw_ref[...], LORA)
        qn_t = qn.T.astype(BF16)
        kvn_t = kvn.T.astype(BF16)
        kvn = kvn.astype(BF16)
        qt_nope = jnp.dot(wuqnt_ref[...], qn_t, preferred_element_type=F32)
        qt_pe = jnp.dot(wuqpt_ref[...], qn_t, preferred_element_type=F32)
        vt_out[a] = jnp.dot(wukvt_ref[...], kvn_t, preferred_element_type=F32).astype(BF16)
        k_nope = jnp.dot(kvn, wukk_ref[...], preferred_element_type=F32)
        k_pe = _rope(_rms(kpe_s[src, :], knwp_ref[...], ROPE), cosf, sinf, lane).astype(BF16)
        yield

        scale = QK_HEAD ** -0.5 * LOG2E
        qwn = qnwn_ref[...] * scale
        qwp = qnwp_ref[...] * scale
        for h in range(N_HEADS):
            lo = h * HEAD
            base = h * 2 * HEAD
            xn = qt_nope[lo:lo + NOPE, :]
            xn = xn * lax.rsqrt(jnp.sum(xn * xn, axis=0, keepdims=True) * (1.0 / NOPE) + EPS) * qwn
            xp = qt_pe[h * ROPE:(h + 1) * ROPE, :]
            xp = xp * lax.rsqrt(jnp.sum(xp * xp, axis=0, keepdims=True) * (1.0 / ROPE) + EPS) * qwp
            t1, t2 = xp[:half], xp[half:]
            qt_out[a, base:base + NOPE, :] = xn.astype(BF16)
            qt_out[a, base + NOPE:base + NOPE + half, :] = (t1 * cos_t - t2 * sin_t).astype(BF16)
            qt_out[a, base + NOPE + half:base + QK_HEAD, :] = (t2 * cos_t + t1 * sin_t).astype(BF16)
            qt_out[a, base + QK_HEAD:base + 2 * HEAD, :] = jnp.zeros((2 * HEAD - QK_HEAD, tm), BF16)
            k_out[rows, base:base + HEAD] = _rms(k_nope[:, lo:lo + HEAD], knwn_ref[...], NOPE).astype(BF16)
            k_out[rows, base + HEAD:base + 2 * HEAD] = k_pe
        yield

        seq_start = (4 * i + a) % tiles_per_seq == 0
        row8 = lax.broadcasted_iota(jnp.int32, (8, HEAD), 0)
        outs = ((gq_out, HEAD ** -0.5), (gk_out, 1.0), (gv_out, None))
        for part, (dst, norm_scale) in enumerate(outs):
            for h in range(N_HEADS):
                cols = slice(part * MIX + h * HEAD, part * MIX + (h + 1) * HEAD)
                g_blk = g_s[src, cols]
                prev = jnp.where(seq_start, 0.0, tail[:, cols])
                conv = convw_ref[CONV_W - 1:CONV_W, cols] * g_blk
                for s in range(1, CONV_W):
                    rolled = pltpu.roll(g_blk, s, 0)
                    first = jnp.where(row8 < s, pltpu.roll(prev, s, 0), rolled[:8])
                    shifted = jnp.concatenate([first, rolled[8:]], axis=0)
                    conv = conv + convw_ref[CONV_W - 1 - s:CONV_W - s, cols] * shifted
                tail[:, cols] = g_blk[tm - 8:tm, :]
                half_conv = 0.5 * conv
                act = half_conv + half_conv * jnp.tanh(half_conv)
                if norm_scale is not None:
                    act = act * (lax.rsqrt(jnp.sum(act * act, axis=-1, keepdims=True) + EPS) * norm_scale)
                dst[rows, h * HEAD:(h + 1) * HEAD] = act.astype(BF16)
            yield

        ab = ab_s[src, :]
        sp_in = ab + dtb_ref[...]
        softplus = jnp.maximum(sp_in, 0.0) + jnp.log(1.0 + jnp.exp(-jnp.abs(sp_in)))
        g = -jnp.exp(alog_ref[...]) * softplus
        beta = _sigmoid(ab)
        packed = jnp.where(lane < N_HEADS, g, jnp.where(lane < 2 * N_HEADS, beta, 0.0))
        top = packed.T[0:8, :]
        row = lax.broadcasted_iota(jnp.int32, (8, LANES), 0)
        pos = lax.broadcasted_iota(jnp.int32, (8, LANES), 1) % CHUNK
        parts = []
        for hh in range(tm // LANES):
            v = top[:, hh * LANES:(hh + 1) * LANES]
            cum = v
            shift = 1
            while shift < CHUNK:
                cum = cum + jnp.where(pos >= shift, pltpu.roll(cum, shift, 1), 0.0)
                shift *= 2
            v = jnp.where(row < N_HEADS, cum, v)
            parts.append(v)
            for c2 in range(LANES // CHUNK):
                rows_out[a * (tm // CHUNK) + hh * (LANES // CHUNK) + c2] = v[:, c2 * CHUNK:(c2 + 1) * CHUNK]
        scal_t = jnp.concatenate(parts, axis=1)
        gb_out[rows, :] = jnp.concatenate([scal_t, jnp.zeros((LANES - 8, tm), F32)], axis=0).T

    @pl.when(i == 0)
    def _():
        tail[...] = jnp.zeros(tail.shape, F32)
        _interleave(front(x0_ref, 0))

    def backs(slot, first_tile):
        yield from back(slot, 0, first_tile)
        yield from back(slot, 1, first_tile + 1)

    _interleave(front(xa_ref, 1), backs(0, 0))
    _interleave(front(xb_ref, 0), backs(1, 2))


def _attend(pairs, qt_ref, k_ref, vt_ref, m_s, l_s, acc_s, st_s, *, tq, nq):
    heads = range(N_HEADS)
    steps = nq + 1
    kk = lax.broadcasted_iota(jnp.int32, (tq, tq), 0)
    qq = lax.broadcasted_iota(jnp.int32, (tq, tq), 1)
    ones = jnp.ones((16, tq), BF16)
    hk = lambda h: slice(2 * HEAD * h, 2 * HEAD * (h + 1))
    hv = lambda h: slice(HEAD * h, HEAD * (h + 1))

    def blocks(s):
        return [(jnp.where(s <= p, p, nq - 1 - p), jnp.where(s <= p, p - s, s - p - 1)) for p in pairs]

    def scores(s, c, h, q, kv):
        r0 = pl.multiple_of(kv * tq, tq)
        st_s[s % 2, c * N_HEADS + h] = jnp.dot(k_ref[pl.ds(r0, tq), hk(h)], qt_ref[q, hk(h), :],
                                               preferred_element_type=F32)

    for c, (q, kv) in enumerate(blocks(0)):
        for h in heads:
            scores(0, c, h, q, kv)
    yield
    for s in range(steps):
        nxt = blocks(s + 1) if s + 1 < steps else None
        for c, (q, kv) in enumerate(blocks(s)):
            for h in heads:
                if nxt is not None:
                    scores(s + 1, c, h, *nxt[c])
                i = q * N_HEADS + h
                st = st_s[s % 2, c * N_HEADS + h]
                if s in (0, steps - 1):
                    st = jnp.where(kk <= qq, st, -jnp.inf)
                m_old = m_s[i]
                m_new = jnp.maximum(m_old, jnp.max(st, axis=0, keepdims=True))
                alpha = jnp.exp2(m_old - m_new)
                p_t = jnp.exp2(st - m_new).astype(BF16)
                pv = jnp.dot(jnp.concatenate([vt_ref[kv, hv(h), :], ones], axis=0), p_t,
                             preferred_element_type=F32)
                m_s[i] = m_new
                l_s[i] = alpha * l_s[i] + pv[HEAD:HEAD + 1]
                acc_s[i] = alpha * acc_s[i] + pv[:HEAD]
            yield


def _attn_kernel(qt_ref, k_ref, vt_ref, w_ref, o_ref, m_s, l_s, acc_s, st_s, *, tq, nq):
    m_s[...] = jnp.full(m_s.shape, -jnp.inf, F32)
    l_s[...] = jnp.zeros(l_s.shape, F32)
    acc_s[...] = jnp.zeros(acc_s.shape, F32)
    n_pairs = nq // 2

    def body(t, carry):
        _interleave(_attend([t, n_pairs - 1 - t], qt_ref, k_ref, vt_ref, m_s, l_s, acc_s, st_s,
                            tq=tq, nq=nq))
        return carry

    lax.fori_loop(0, n_pairs // 2, body, 0)
    for q in range(nq):
        for h in range(N_HEADS):
            i = q * N_HEADS + h
            o_t = acc_s[i] / l_s[i]
            o_t = o_t * lax.rsqrt(jnp.sum(o_t * o_t, axis=0, keepdims=True) * (1.0 / HEAD) + EPS) * w_ref[h]
            o_ref[q, HEAD * h:HEAD * (h + 1), :] = o_t.astype(BF16)


def _gdn_kernel(q_ref, k_ref, v_ref, z_ref, gb_ref, rows_ref, gnw_ref, o_ref,
                s_ref, u_s, wq_s, kd_s, at_s, *, n_chunks, unroll, nb):
    ii = lax.broadcasted_iota(jnp.int32, (CHUNK, CHUNK), 0)
    jj = lax.broadcasted_iota(jnp.int32, (CHUNK, CHUNK), 1)
    eye = jnp.where(ii == jj, 1.0, 0.0)
    heads = range(N_HEADS)
    ring = 2 * unroll
    slot = lambda n, h: (n & (ring - 1)) * N_HEADS + h
    eye_p = jnp.concatenate([eye] * N_HEADS, axis=1)
    lane_head = lax.broadcasted_iota(jnp.int32, (CHUNK, N_HEADS * CHUNK), 1) // CHUNK

    def solve(t, bb):
        ns = [t * unroll + c for c in range(unroll)]
        r0 = [pl.multiple_of(n * CHUNK, CHUNK) for n in ns]
        scal = [gb_ref[bb, pl.ds(r, CHUNK), :] for r in r0]
        rows = [rows_ref[bb, n] for n in ns]
        items = [(c, h) for c in range(unroll) for h in heads]
        cols = lambda h: slice(h * HEAD, (h + 1) * HEAD)
        q = [q_ref[bb, pl.ds(r0[c], CHUNK), cols(h)] for c, h in items]
        k = [k_ref[bb, pl.ds(r0[c], CHUNK), cols(h)] for c, h in items]
        v = [v_ref[bb, pl.ds(r0[c], CHUNK), cols(h)] for c, h in items]
        g_col = [jnp.broadcast_to(scal[c][:, h:h + 1], (CHUNK, LANES)) for c, h in items]
        b_col = [jnp.broadcast_to(scal[c][:, N_HEADS + h:N_HEADS + h + 1], (CHUNK, LANES))
                 for c, h in items]
        g_row = [rows[c][h:h + 1, :] for c, h in items]
        n_it = range(len(items))
        decay = [jnp.exp(jnp.where(ii >= jj, g_col[x][:, :CHUNK] - g_row[x], -jnp.inf)) for x in n_it]
        qkk = [_mm_nt(jnp.concatenate([q[x], k[x]], axis=0), k[x]) for x in n_it]
        yield
        lmat = [jnp.where(ii > jj, b_col[x][:, :CHUNK] * qkk[x][CHUNK:] * decay[x], 0.0) for x in n_it]
        n_ch = range(unroll)
        packed = lambda vals, c: jnp.concatenate(vals[c * N_HEADS:(c + 1) * N_HEADS], axis=1)

        def per_head(x_p, y_p):
            y_b = y_p.astype(BF16)
            diag = jnp.concatenate([jnp.where(lane_head == h, y_b, jnp.zeros_like(y_b)) for h in heads], axis=0)
            return jnp.dot(x_p.astype(BF16), diag, preferred_element_type=F32)

        l_p = [packed(lmat, c) for c in n_ch]
        inv_p = [eye_p - l_p[c] for c in n_ch]
        pw_p = [per_head(l_p[c], l_p[c]) for c in n_ch]
        yield
        for step in range(4):
            both = [per_head(jnp.concatenate([inv_p[c], pw_p[c]], axis=0), pw_p[c]) for c in n_ch]
            inv_p = [inv_p[c] + both[c][:CHUNK] for c in n_ch]
            pw_p = [both[c][CHUNK:] for c in n_ch]
            yield
        inv_p = [inv_p[c] + per_head(inv_p[c], pw_p[c]) for c in n_ch]
        yield
        inv = [inv_p[c][:, h * CHUNK:(h + 1) * CHUNK] for c, h in items]
        e_col = [jnp.exp(g_col[x]) for x in n_it]
        rhs = [jnp.concatenate([v[x].astype(F32) * b_col[x], k[x].astype(F32) * (b_col[x] * e_col[x])],
                               axis=1) for x in n_it]
        uw = [_mm(inv[x], rhs[x]) for x in n_it]
        yield
        for x, (c, h) in enumerate(items):
            idx = slot(ns[c], h)
            g_last = g_col[x][CHUNK - 1:CHUNK, :]
            u_s[bb, idx] = uw[x][:, :HEAD]
            wq_s[bb, idx, 0:CHUNK, :] = uw[x][:, HEAD:].astype(BF16)
            wq_s[bb, idx, CHUNK:2 * CHUNK, :] = (q[x].astype(F32) * e_col[x]).astype(BF16)
            kd_s[bb, idx] = (k[x].astype(F32) * jnp.exp(g_last - g_col[x])).astype(BF16)
            at_s[bb, idx] = (qkk[x][:CHUNK] * decay[x]).astype(BF16)

    gnw = gnw_ref[...]

    def scan(t, bb):
        for c in range(unroll):
            n = t * unroll + c
            r0 = pl.multiple_of(n * CHUNK, CHUNK)
            g_last = gb_ref[bb, pl.ds(r0 + CHUNK - 1, 1), :]
            state = [s_ref[bb, h] for h in heads]
            sb = [x.astype(BF16) for x in state]
            ws = [_mm(wq_s[bb, slot(n, h)], sb[h]) for h in heads]
            yield
            v_new = [(u_s[bb, slot(n, h)] - ws[h][:CHUNK]).astype(BF16) for h in heads]
            o = [ws[h][CHUNK:] + _mm(at_s[bb, slot(n, h)], v_new[h]) for h in heads]
            upd = [_mm_tn(kd_s[bb, slot(n, h)], v_new[h]) for h in heads]
            yield
            for h in heads:
                c_dec = jnp.exp(jnp.broadcast_to(g_last[:, h:h + 1], (HEAD, HEAD)))
                s_ref[bb, h] = state[h] * c_dec + upd[h]
                zh = z_ref[bb, pl.ds(r0, CHUNK), h * HEAD:(h + 1) * HEAD].astype(F32)
                gated = _rms(o[h], gnw, HEAD) * (zh * _sigmoid(zh))
                o_ref[bb, pl.ds(r0, CHUNK), h * HEAD:(h + 1) * HEAD] = gated.astype(BF16)

    n_groups = n_chunks // unroll
    batches = range(nb)

    @pl.when(pl.program_id(1) == 0)
    def _():
        s_ref[...] = jnp.zeros_like(s_ref)

    _interleave(*[solve(0, bb) for bb in batches])

    def body(t, carry):
        _interleave(*[solve(t, bb) for bb in batches], *[scan(t - 1, bb) for bb in batches])
        return carry

    lax.fori_loop(1, n_groups, body, 0)
    _interleave(*[scan(n_groups - 1, bb) for bb in batches])


def _mlp_kernel(x_ref, mlat_ref, gdn_ref, woa_ref, wob_ref, nw_ref, wup_ref, wdn_ref, o_ref):
    mla_t = jnp.concatenate([mlat_ref[t] for t in range(mlat_ref.shape[0])], axis=1)
    h = (x_ref[...]
         + lax.dot_general(mla_t, woa_ref[...], (((0,), (0,)), ((), ())), preferred_element_type=F32)
         + jnp.dot(gdn_ref[...], wob_ref[...], preferred_element_type=F32))
    hn = _rms(h, nw_ref[...], D_MODEL).astype(BF16)
    o_ref[...] = h
    for c in range(D_FF // MLP_FC):
        u = jnp.dot(hn, wup_ref[:, c * MLP_FC:(c + 1) * MLP_FC], preferred_element_type=F32)
        a = jnp.square(jnp.maximum(u, 0.0)).astype(BF16)
        o_ref[...] += jnp.dot(a, wdn_ref[c * MLP_FC:(c + 1) * MLP_FC, :], preferred_element_type=F32)


def _const_spec(shape):
    nd = len(shape)
    return pl.BlockSpec(shape, lambda *_: (0,) * nd, pipeline_mode=pl.Buffered(1))


def _pad_lanes(v, width=LANES):
    v = v.reshape(1, -1).astype(F32)
    return jnp.pad(v, ((0, 0), (0, width - v.shape[1])))


def _layer(h, pos, inv_freq, p):
    B, S, _ = h.shape
    T = B * S
    x2 = h.reshape(T, D_MODEL)

    w_in = p["w_in"]
    o_q, o_kv, o_pe = 0, LORA, 2 * LORA
    o_g = o_pe + ROPE
    o_z = o_g + 3 * MIX
    o_a = o_z + MIX
    w_lat = w_in[:, o_q:o_pe].astype(BF16)
    w_kpe = jnp.pad(w_in[:, o_pe:o_g], ((0, 0), (0, LANES - ROPE))).astype(BF16)
    w_g = w_in[:, o_g:o_z].astype(BF16)
    w_z = w_in[:, o_z:o_a].astype(BF16)
    w_ab = jnp.pad(w_in[:, o_a:], ((0, 0), (0, LANES - 2 * N_HEADS))).astype(BF16)
    w_uq = p["w_uq"].reshape(LORA, N_HEADS, QK_HEAD)
    w_uq_nt = w_uq[:, :, :NOPE].reshape(LORA, N_HEADS * NOPE).T.astype(BF16)
    w_uq_pt = w_uq[:, :, NOPE:].reshape(LORA, N_HEADS * ROPE).T.astype(BF16)
    w_ukv = p["w_ukv"].reshape(LORA, N_HEADS, NOPE + HEAD)
    w_uk_k = w_ukv[:, :, :NOPE].reshape(LORA, N_HEADS * NOPE).astype(BF16)
    w_uk_vt = w_ukv[:, :, NOPE:].reshape(LORA, N_HEADS * HEAD).T.astype(BF16)

    tm = PREP_TM
    n_tiles = T // tm
    assert n_tiles % 4 == 0
    n_pairs = n_tiles // 2
    row_spec = lambda w: pl.BlockSpec((4 * tm, w), lambda i: (i, 0))
    col_bcast = lambda v: jnp.broadcast_to(v.astype(F32)[:, None], (v.shape[0], tm))
    x_pair = lambda index: pl.BlockSpec((2 * tm, D_MODEL), index)
    prep_in = [
        (x2, x_pair(lambda i: (0, 0))),
        (x2, x_pair(lambda i: (2 * i + 1, 0))),
        (x2, x_pair(lambda i: (jnp.minimum(2 * i + 2, n_pairs - 1), 0))),
        (pos.reshape(n_tiles, 1, tm), pl.BlockSpec((4, 1, tm), lambda i: (i, 0, 0))),
        (p["attn_norm_w"].reshape(1, D_MODEL), None),
        (w_lat, None), (jnp.concatenate([w_kpe, w_ab], axis=1), None), (w_g, None), (w_z, None),
        (p["q_lat_norm_w"].reshape(1, LORA), None),
        (p["kv_lat_norm_w"].reshape(1, LORA), None),
        (w_uq_nt, None), (w_uq_pt, None), (w_uk_k, None), (w_uk_vt, None),
        (col_bcast(p["q_norm_w"][:NOPE]), None),
        (col_bcast(p["q_norm_w"][NOPE:]), None),
        (p["k_norm_w"][:NOPE].reshape(1, NOPE), None),
        (_pad_lanes(p["k_norm_w"][NOPE:]), None),
        (col_bcast(inv_freq), None),
        (p["conv_w"], None),
        (_pad_lanes(p["a_log"]), None),
        (_pad_lanes(p["dt_bias"]), None),
    ]
    prep_args = [a for a, _ in prep_in]
    prep_specs = [s if s is not None else _const_spec(a.shape) for a, s in prep_in]
    out_shapes = [
        jax.ShapeDtypeStruct((n_tiles, 2 * MIX, tm), BF16),
        jax.ShapeDtypeStruct((T, 2 * MIX), BF16),
        jax.ShapeDtypeStruct((n_tiles, MIX, tm), BF16),
        jax.ShapeDtypeStruct((T, MIX), BF16),
        jax.ShapeDtypeStruct((T, MIX), BF16),
        jax.ShapeDtypeStruct((T, MIX), BF16),
        jax.ShapeDtypeStruct((T, MIX), BF16),
        jax.ShapeDtypeStruct((T, LANES), F32),
        jax.ShapeDtypeStruct((T // CHUNK, 8, CHUNK), F32),
    ]
    tile_spec = lambda r: pl.BlockSpec((4, r, tm), lambda i: (i, 0, 0))
    out_specs = [tile_spec(2 * MIX), row_spec(2 * MIX), tile_spec(MIX)] + [row_spec(MIX)] * 4 + [
        row_spec(LANES),
        pl.BlockSpec((4 * tm // CHUNK, 8, CHUNK), lambda i: (i, 0, 0)),
    ]
    qt, k, vt, gq, gk, gv, gz, gb, rows = pl.pallas_call(
        functools.partial(_prep_kernel, tm=tm, tiles_per_seq=S // tm),
        grid=(n_tiles // 4,),
        in_specs=prep_specs,
        out_specs=out_specs,
        out_shape=out_shapes,
        scratch_shapes=2 * [
            pltpu.VMEM((2 * tm, 2 * LORA), F32),
            pltpu.VMEM((2 * tm, 3 * MIX), F32),
            pltpu.VMEM((2 * tm, LANES), F32),
            pltpu.VMEM((2 * tm, LANES), F32),
            pltpu.VMEM((2 * tm, MIX), BF16),
        ] + [
            pltpu.VMEM((8, 3 * MIX), F32),
        ],
        compiler_params=pltpu.CompilerParams(dimension_semantics=("arbitrary",),
                                             vmem_limit_bytes=VMEM_LIMIT),
        name="prep",
    )(*prep_args)

    tq = tm
    nq = S // tq
    assert nq % 4 == 0
    mla_o = pl.pallas_call(
        functools.partial(_attn_kernel, tq=tq, nq=nq),
        grid=(B,),
        in_specs=[
            pl.BlockSpec((nq, 2 * MIX, tq), lambda b: (b, 0, 0)),
            pl.BlockSpec((None, S, 2 * MIX), lambda b: (b, 0, 0)),
            pl.BlockSpec((None, nq, MIX, tq), lambda b: (b, 0, 0, 0)),
            _const_spec((N_HEADS, HEAD, tq)),
        ],
        out_specs=pl.BlockSpec((nq, MIX, tq), lambda b: (b, 0, 0)),
        out_shape=jax.ShapeDtypeStruct((B * nq, MIX, tq), BF16),
        scratch_shapes=[pltpu.VMEM((nq * N_HEADS, 1, tq), F32),
                        pltpu.VMEM((nq * N_HEADS, 1, tq), F32),
                        pltpu.VMEM((nq * N_HEADS, HEAD, tq), F32),
                        pltpu.VMEM((2, 2 * N_HEADS, tq, tq), F32)],
        compiler_params=pltpu.CompilerParams(
            dimension_semantics=("arbitrary",),
            vmem_limit_bytes=VMEM_LIMIT),
        name="attn",
    )(qt, k.reshape(B, S, 2 * MIX), vt.reshape(B, nq, MIX, tq),
      jnp.broadcast_to(p["mla_out_norm_w"].astype(F32)[:, :, None], (N_HEADS, HEAD, tq)))

    n_chunks = S // CHUNK
    nb = GDN_BATCHES
    n_seg = GDN_SEGMENTS
    seg_chunks = n_chunks // n_seg
    assert B % nb == 0 and n_chunks % n_seg == 0 and seg_chunks % GDN_UNROLL == 0
    seq_spec = lambda w: pl.BlockSpec((nb, S // n_seg, w), lambda b, j: (b, j, 0))
    ring_slots = 2 * GDN_UNROLL * N_HEADS
    gdn_o = pl.pallas_call(
        functools.partial(_gdn_kernel, n_chunks=seg_chunks, unroll=GDN_UNROLL, nb=nb),
        grid=(B // nb, n_seg),
        in_specs=[seq_spec(MIX)] * 4 + [
            seq_spec(LANES),
            pl.BlockSpec((nb, seg_chunks, 8, CHUNK), lambda b, j: (b, j, 0, 0)),
            _const_spec((1, HEAD)),
        ],
        out_specs=seq_spec(MIX),
        out_shape=jax.ShapeDtypeStruct((B, S, MIX), BF16),
        scratch_shapes=[
            pltpu.VMEM((nb, N_HEADS, HEAD, HEAD), F32),
            pltpu.VMEM((nb, ring_slots, CHUNK, HEAD), F32),
            pltpu.VMEM((nb, ring_slots, 2 * CHUNK, HEAD), BF16),
            pltpu.VMEM((nb, ring_slots, CHUNK, HEAD), BF16),
            pltpu.VMEM((nb, ring_slots, CHUNK, CHUNK), BF16),
        ],
        compiler_params=pltpu.CompilerParams(dimension_semantics=("arbitrary", "arbitrary"),
                                             vmem_limit_bytes=VMEM_LIMIT),
        name="gdn",
    )(gq.reshape(B, S, MIX), gk.reshape(B, S, MIX), gv.reshape(B, S, MIX), gz.reshape(B, S, MIX),
      gb.reshape(B, S, LANES), rows.reshape(B, n_chunks, 8, CHUNK),
      p["gdn_norm_w"].reshape(1, HEAD))

    tm2 = MLP_TM
    w_out = p["w_out"].astype(BF16)
    tok_spec = lambda w: pl.BlockSpec((tm2, w), lambda i: (i, 0))
    out = pl.pallas_call(
        _mlp_kernel,
        grid=(T // tm2,),
        in_specs=[
            tok_spec(D_MODEL),
            pl.BlockSpec((tm2 // tq, MIX, tq), lambda i: (i, 0, 0)),
            tok_spec(MIX),
            _const_spec((MIX, D_MODEL)), _const_spec((MIX, D_MODEL)),
            _const_spec((1, D_MODEL)),
            _const_spec((D_MODEL, D_FF)), _const_spec((D_FF, D_MODEL)),
        ],
        out_specs=tok_spec(D_MODEL),
        out_shape=jax.ShapeDtypeStruct((T, D_MODEL), F32),
        compiler_params=pltpu.CompilerParams(dimension_semantics=("arbitrary",),
                                             vmem_limit_bytes=VMEM_LIMIT),
        name="mlp",
    )(x2, mla_o, gdn_o.reshape(T, MIX), w_out[:MIX], w_out[MIX:],
      p["mlp_norm_w"].reshape(1, D_MODEL), p["w_up"].astype(BF16), p["w_down"].astype(BF16))
    return out.reshape(B, S, D_MODEL)


def kernel(x, positions, attn_norm_w, w_in, q_lat_norm_w, w_uq, kv_lat_norm_w, w_ukv, q_norm_w,
           k_norm_w, mla_out_norm_w, conv_w, a_log, dt_bias, gdn_norm_w, w_out, mlp_norm_w, w_up,
           w_down):
    B, S, _ = x.shape
    half = ROPE // 2
    inv_freq = ROPE_THETA ** (-jnp.arange(half, dtype=F32) / half)
    params = dict(attn_norm_w=attn_norm_w, w_in=w_in, q_lat_norm_w=q_lat_norm_w, w_uq=w_uq,
                  kv_lat_norm_w=kv_lat_norm_w, w_ukv=w_ukv, q_norm_w=q_norm_w, k_norm_w=k_norm_w,
                  mla_out_norm_w=mla_out_norm_w, conv_w=conv_w, a_log=a_log, dt_bias=dt_bias,
                  gdn_norm_w=gdn_norm_w, w_out=w_out, mlp_norm_w=mlp_norm_w, w_up=w_up,
                  w_down=w_down)
    h = x
    for l in range(attn_norm_w.shape[0]):
        h = _layer(h, positions, inv_freq, {name: val[l] for name, val in params.items()})
    return h
```
